```python
import jax, jax.numpy as jnp
from jax import lax
import numpy as np

D_MODEL = 1024
BATCH = 8
SEQ = 2048
DEPTH = 1
DEC_BATCH = 128
DEC_SEQ = 8
PAST_LEN = 16384
PAGE_SIZE = 128

HEAD_DIM = 128
N_RET_HEADS = 4
N_ML_HEADS = 4
RET_W = N_RET_HEADS * HEAD_DIM
ML_W = N_ML_HEADS * HEAD_DIM
MIX_W = RET_W + ML_W
IN_COLS = 4 * RET_W + 4 * ML_W + 2 * N_ML_HEADS
D_FF = 2816
CONV_W = 3
CHUNK = 128
ROPE_BASE = 10000.0
EPS = 1e-6
M_INIT = -1e30

kernel_name = "retnet_mlstm_parallel_heads_convffn_step"


def rmsnorm(x, g):
    xf = x.astype(jnp.float32)
    y = xf * lax.rsqrt(jnp.mean(xf * xf, axis=-1, keepdims=True) + EPS)
    return (y * g.astype(jnp.float32)).astype(x.dtype)


def head_groupnorm(h, g):
    mu = jnp.mean(h, axis=-1, keepdims=True)
    var = jnp.mean(jnp.square(h - mu), axis=-1, keepdims=True)
    y = (h - mu) * lax.rsqrt(var + EPS)
    B, L, H, D = h.shape
    return y.reshape(B, L, H * D) * g.astype(jnp.float32)


def rotary(x, pos):
    D = x.shape[-1]
    freqs = ROPE_BASE ** (-jnp.arange(0, D, 2, dtype=jnp.float32) / D)
    ang = pos.astype(jnp.float32)[:, None] * freqs[None, :]
    cos = jnp.cos(ang)[None, :, None, :]
    sin = jnp.sin(ang)[None, :, None, :]
    x1, x2 = x[..., : D // 2], x[..., D // 2:]
    return jnp.concatenate([x1 * cos - x2 * sin, x1 * sin + x2 * cos], axis=-1)


def to_chunks(x, c):
    B, L, H = x.shape[:3]
    x = x.reshape((B, L // c, c, H) + x.shape[3:])
    return jnp.swapaxes(jnp.moveaxis(x, 1, 0), 2, 3)


def from_chunks(x):
    nc, B, H, c, D = x.shape
    x = jnp.moveaxis(jnp.swapaxes(x, 2, 3), 0, 1)
    return x.reshape(B, nc * c, H, D)


def retention_chunked(q, k, v, S0):
    L = q.shape[1]
    H = q.shape[2]
    c = min(CHUNK, L)
    lg = jnp.log(1.0 - 2.0 ** (-5.0 - jnp.arange(H, dtype=jnp.float32)))
    idx = jnp.arange(c)
    causal = idx[:, None] >= idx[None, :]
    expo = (idx[:, None] - idx[None, :]).astype(jnp.float32)[None] * lg[:, None, None]
    dmat = jnp.where(causal[None], jnp.exp(jnp.where(causal[None], expo, 0.0)), 0.0)
    xi = jnp.exp((idx + 1).astype(jnp.float32)[None, :] * lg[:, None])
    zeta = jnp.exp((c - 1 - idx).astype(jnp.float32)[None, :] * lg[:, None])
    chunk_decay = jnp.exp(c * lg)

    def step(S, inp):
        qc, kc, vc = inp
        scores = jnp.einsum('bhnd,bhmd->bhnm', qc, kc) * dmat[None]
        out = (jnp.einsum('bhnm,bhmv->bhnv', scores, vc)
               + jnp.einsum('bhnd,bhdv->bhnv', qc, S) * xi[None, :, :, None])
        S_new = (S * chunk_decay[None, :, None, None]
                 + jnp.einsum('bhmd,bhmv->bhdv', kc * zeta[None, :, :, None], vc))
        return S_new, out

    S_fin, outs = lax.scan(step, S0, (to_chunks(q, c), to_chunks(k, c), to_chunks(v, c)))
    return from_chunks(outs), S_fin


def mlstm_chunked(q, k, v, ig, fg, C0, n0, m0):
    L = q.shape[1]
    c = min(CHUNK, L)
    logf = jax.nn.log_sigmoid(fg)
    idx = jnp.arange(c)
    causal = idx[:, None] >= idx[None, :]
    ig_c = jnp.moveaxis(ig.reshape(ig.shape[0], L // c, c, ig.shape[2]), 1, 0).swapaxes(2, 3)
    lf_c = jnp.moveaxis(logf.reshape(logf.shape[0], L // c, c, logf.shape[2]), 1, 0).swapaxes(2, 3)

    def step(carry, inp):
        C, n, m = carry
        qc, kc, vc, ic, lfc = inp
        b = jnp.cumsum(lfc, axis=-1)
        logw = b[..., :, None] - b[..., None, :] + ic[..., None, :]
        logw = jnp.where(causal, logw, -jnp.inf)
        m_t = jnp.maximum(b + m[..., None], jnp.max(logw, axis=-1))
        w = jnp.exp(logw - m_t[..., None])
        inter = jnp.exp(b + m[..., None] - m_t)
        s = jnp.einsum('bhtd,bhsd->bhts', qc, kc) * w
        num = (jnp.einsum('bhts,bhsv->bhtv', s, vc)
               + inter[..., None] * jnp.einsum('bhvd,bhtd->bhtv', C, qc))
        den_dot = jnp.sum(s, axis=-1) + inter * jnp.einsum('bhd,bhtd->bht', n, qc)
        den = jnp.maximum(jnp.abs(den_dot), jnp.exp(-m_t))
        h = num / den[..., None]
        m_new = m_t[..., -1]
        wl = jnp.exp(b[..., -1:] - b + ic - m_new[..., None])
        decay = jnp.exp(b[..., -1] + m - m_new)
        C_new = decay[..., None, None] * C + jnp.einsum('bhsv,bhsd->bhvd', vc * wl[..., None], kc)
        n_new = decay[..., None] * n + jnp.einsum('bhs,bhsd->bhd', wl, kc)
        return (C_new, n_new, m_new), h

    (C_f, n_f, m_f), outs = lax.scan(
        step, (C0, n0, m0),
        (to_chunks(q, c), to_chunks(k, c), to_chunks(v, c), ig_c, lf_c))
    return from_chunks(outs), C_f, n_f, m_f


def decoder_layer(x, pos, S_ret, C, n, m, conv_buf,
                  pre_mix_g, w_in, b_g, ret_g, ml_g, w_out, post_mix_g,
                  pre_ffn_g, w_up, conv_w, conv_b, w_down, post_ffn_g):
    B, L, _ = x.shape
    f32 = jnp.float32
    h = rmsnorm(x, pre_mix_g)
    proj = (h @ w_in).astype(f32)
    o = 0
    def take(width):
        nonlocal o
        seg = proj[..., o:o + width]
        o += width
        return seg
    q_r = take(RET_W).reshape(B, L, N_RET_HEADS, HEAD_DIM)
    k_r = take(RET_W).reshape(B, L, N_RET_HEADS, HEAD_DIM)
    v_r = take(RET_W).reshape(B, L, N_RET_HEADS, HEAD_DIM)
    g_r = take(RET_W)
    q_m = take(ML_W).reshape(B, L, N_ML_HEADS, HEAD_DIM)
    k_m = take(ML_W).reshape(B, L, N_ML_HEADS, HEAD_DIM)
    v_m = take(ML_W).reshape(B, L, N_ML_HEADS, HEAD_DIM)
    o_m = take(ML_W)
    gates = take(2 * N_ML_HEADS) + b_g.astype(f32)
    i_pre, f_pre = gates[..., :N_ML_HEADS], gates[..., N_ML_HEADS:]

    scale = HEAD_DIM ** -0.5
    q_r = rotary(q_r, pos)
    k_r = rotary(k_r, pos) * scale
    ret_out, S_new = retention_chunked(q_r, k_r, v_r, S_ret.astype(f32))
    ret_y = head_groupnorm(ret_out, ret_g) * jax.nn.silu(g_r)
    ml_out, C_new, n_new, m_new = mlstm_chunked(
        q_m, k_m * scale, v_m, i_pre, f_pre,
        C.astype(f32), n.astype(f32), m.astype(f32))
    ml_y = head_groupnorm(ml_out, ml_g) * jax.nn.sigmoid(o_m)

    mix = jnp.concatenate([ret_y, ml_y], axis=-1).astype(x.dtype) @ w_out
    x = x + rmsnorm(mix, post_mix_g)

    h2 = rmsnorm(x, pre_ffn_g)
    up = h2 @ w_up
    padded = jnp.concatenate([conv_buf.astype(up.dtype), up], axis=1)
    conv = sum(padded[:, j:j + L] * conv_w[j] for j in range(CONV_W)) + conv_b
    new_buf = padded[:, L:]
    gate, val = conv[..., :D_FF], conv[..., D_FF:]
    ffn = (jax.nn.gelu(gate.astype(f32), approximate=True) * val.astype(f32)).astype(x.dtype) @ w_down
    x = x + rmsnorm(ffn, post_ffn_g)
    return x, S_new, C_new, n_new, m_new, new_buf


def setup_inputs(seed: int = 0) -> dict:
    key = jax.random.key(seed)
    ks = jax.random.split(key, 24)
    f32 = jnp.float32
    nrm = lambda k, shape, s: jax.random.normal(k, shape, f32) * s
    gain = lambda k, shape: 1.0 + 0.05 * jax.random.normal(k, shape, f32)
    f_bias = jnp.linspace(3.0, 6.0, N_ML_HEADS, dtype=f32)
    b_gates = jnp.concatenate([
        nrm(ks[0], (DEPTH, N_ML_HEADS), 0.01),
        f_bias[None] + nrm(ks[1], (DEPTH, N_ML_HEADS), 0.01)], axis=-1)
    return {
        "x_prompt": nrm(ks[2], (BATCH, SEQ, D_MODEL), 1.0),
        "x_sample": nrm(ks[3], (DEC_BATCH, DEC_SEQ, D_MODEL), 1.0),
        "state_ret": nrm(ks[4], (DEPTH, DEC_BATCH, N_RET_HEADS, HEAD_DIM, HEAD_DIM), 0.1),
        "state_mlstm_C": nrm(ks[5], (DEPTH, DEC_BATCH, N_ML_HEADS, HEAD_DIM, HEAD_DIM), 0.1),
        "state_mlstm_n": nrm(ks[6], (DEPTH, DEC_BATCH, N_ML_HEADS, HEAD_DIM), 0.1),
        "state_mlstm_m": nrm(ks[7], (DEPTH, DEC_BATCH, N_ML_HEADS), 1.0),
        "cache_ffn_conv": nrm(ks[8], (DEPTH, DEC_BATCH, CONV_W - 1, 2 * D_FF), 1.0),
        "pre_mix_gain": gain(ks[9], (DEPTH, D_MODEL)),
        "w_in": nrm(ks[10], (DEPTH, D_MODEL, IN_COLS), D_MODEL ** -0.5),
        "b_gates": b_gates,
        "ret_head_gain": gain(ks[11], (DEPTH, RET_W)),
        "mlstm_head_gain": gain(ks[12], (DEPTH, ML_W)),
        "w_out": nrm(ks[13], (DEPTH, MIX_W, D_MODEL), MIX_W ** -0.5),
        "post_mix_gain": gain(ks[14], (DEPTH, D_MODEL)),
        "pre_ffn_gain": gain(ks[15], (DEPTH, D_MODEL)),
        "w_up": nrm(ks[16], (DEPTH, D_MODEL, 2 * D_FF), D_MODEL ** -0.5),
        "conv_w": nrm(ks[17], (DEPTH, CONV_W, 2 * D_FF), CONV_W ** -0.5),
        "conv_b": nrm(ks[18], (DEPTH, 2 * D_FF), 0.02),
        "w_down": nrm(ks[19], (DEPTH, D_FF, D_MODEL), D_FF ** -0.5),
        "post_ffn_gain": gain(ks[20], (DEPTH, D_MODEL)),
    }


def reference(x_prompt, x_sample, state_ret, state_mlstm_C, state_mlstm_n, state_mlstm_m,
              cache_ffn_conv, pre_mix_gain, w_in, b_gates, ret_head_gain, mlstm_head_gain,
              w_out, post_mix_gain, pre_ffn_gain, w_up, conv_w, conv_b, w_down, post_ffn_gain):
    f32 = jnp.float32
    Bp, Lp, _ = x_prompt.shape
    Ls = x_sample.shape[1]
    pos_p = jnp.arange(Lp, dtype=jnp.int32)
    pos_s = PAST_LEN + jnp.arange(Ls, dtype=jnp.int32)
    yp, ys = x_prompt, x_sample
    out_Sp, out_Ss, out_Cp, out_Cs = [], [], [], []
    out_np, out_ns, out_mp, out_ms, out_bp, out_bs = [], [], [], [], [], []
    for l in range(DEPTH):
        params = (pre_mix_gain[l], w_in[l], b_gates[l], ret_head_gain[l], mlstm_head_gain[l],
                  w_out[l], post_mix_gain[l], pre_ffn_gain[l], w_up[l], conv_w[l], conv_b[l],
                  w_down[l], post_ffn_gain[l])
        S0 = jnp.zeros((Bp, N_RET_HEADS, HEAD_DIM, HEAD_DIM), f32)
        C0 = jnp.zeros((Bp, N_ML_HEADS, HEAD_DIM, HEAD_DIM), f32)
        n0 = jnp.zeros((Bp, N_ML_HEADS, HEAD_DIM), f32)
        m0 = jnp.full((Bp, N_ML_HEADS), M_INIT, f32)
        buf0 = jnp.zeros((Bp, CONV_W - 1, 2 * D_FF), x_prompt.dtype)
        yp, Sp, Cp, np_, mp, bp = decoder_layer(yp, pos_p, S0, C0, n0, m0, buf0, *params)
        ys, Ss, Cs, ns, ms, bs = decoder_layer(ys, pos_s, state_ret[l], state_mlstm_C[l],
                                               state_mlstm_n[l], state_mlstm_m[l],
                                               cache_ffn_conv[l], *params)
        out_Sp.append(Sp); out_Ss.append(Ss)
        out_Cp.append(Cp); out_Cs.append(Cs)
        out_np.append(np_); out_ns.append(ns)
        out_mp.append(mp); out_ms.append(ms)
        out_bp.append(bp); out_bs.append(bs)
    return (yp, ys,
            jnp.stack(out_Sp), jnp.stack(out_Ss),
            jnp.stack(out_Cp), jnp.stack(out_Cs),
            jnp.stack(out_np), jnp.stack(out_ns),
            jnp.stack(out_mp), jnp.stack(out_ms),
            jnp.stack(out_bp), jnp.stack(out_bs))
```

```python
import functools
import math

import jax
import jax.numpy as jnp
from jax import lax
from jax.experimental import pallas as pl
from jax.experimental.pallas import tpu as pltpu

D_MODEL = 1024
HEAD_DIM = 128
N_RET_HEADS = 4
N_ML_HEADS = 4
RET_W = N_RET_HEADS * HEAD_DIM
ML_W = N_ML_HEADS * HEAD_DIM
D_FF = 2816
CONV_W = 3
PAST_LEN = 16384
ROPE_BASE = 10000.0
EPS = 1e-6
M_INIT = -1e30

BLK = 128
HEADS_PER_GROUP = 2
GROUP_W = HEADS_PER_GROUP * HEAD_DIM
FF_BLK = 256
VMEM_LIMIT_BYTES = 56 * 1024 * 1024

F32 = jnp.float32
BF16 = jnp.bfloat16


def _dot(a, b, precision=None):
    return jnp.dot(a, b, preferred_element_type=F32, precision=precision)


def _dot_nt(a, b):
    return lax.dot_general(a, b, (((1,), (1,)), ((), ())), preferred_element_type=F32)


def _rmsnorm(x, g):
    return x * lax.rsqrt(jnp.mean(x * x, axis=-1, keepdims=True) + EPS) * g


def _groupnorm(h, g):
    mu = jnp.mean(h, axis=-1, keepdims=True)
    d = h - mu
    var = jnp.mean(d * d, axis=-1, keepdims=True)
    return d * lax.rsqrt(var + EPS) * g


def _ret_log_gamma(h):
    return math.log(1.0 - 2.0 ** (-5.0 - h))


class _Masks:
    def __init__(self, seg):
        shift = seg.bit_length() - 1
        r = lax.broadcasted_iota(jnp.int32, (BLK, BLK), 0)
        c = lax.broadcasted_iota(jnp.int32, (BLK, BLK), 1)
        rseq = r >> shift
        self.causal = (rseq == (c >> shift)) & (r >= c)
        self.tpos = (r & (seg - 1)).astype(F32)
        self.diff = (r - c).astype(F32)
        self.col_is_last_of_rowseq = c == (rseq * seg + (seg - 1))
        self.col_is_rowseq = c == rseq
        self.row_is_last_of_colseq = r == (c * seg + (seg - 1))


def _rows_from_seq(mask_col_is_rowseq, seq_row):
    return jnp.sum(jnp.where(mask_col_is_rowseq, seq_row, 0.0), axis=1, keepdims=True)


def _seq_from_last_row(mask_row_is_last_of_colseq, col):
    return jnp.sum(jnp.where(mask_row_is_last_of_colseq, col, 0.0), axis=0, keepdims=True)


def _mixer_kernel(*refs, seg, n_blk, fresh_state):
    n_seq = BLK // seg
    if fresh_state:
        (x_ref, cos_ref, sin_ref, g_pre_ref, w_grp_ref, w_gate_ref, b_gate_ref, g_ret_ref, g_ml_ref,
         w_out_ref, g_post_ref,
         y_ref, s_out, c_out, n_out, m_out,
         h_ref, proj_ref, mix_ref, gate_ref, bcum_ref, gate_t_ref, bcum_t_ref) = refs
        s_in, c_in, n_in, m_in = s_out, c_out, n_out, m_out

        @pl.when(pl.program_id(1) == 0)
        def _():
            s_out[...] = jnp.zeros(s_out.shape, F32)
            c_out[...] = jnp.zeros(c_out.shape, F32)
            n_out[...] = jnp.zeros(n_out.shape, F32)
            m_out[...] = jnp.full(m_out.shape, M_INIT, F32)
    else:
        (x_ref, cos_ref, sin_ref, g_pre_ref, w_grp_ref, w_gate_ref, b_gate_ref, g_ret_ref, g_ml_ref,
         w_out_ref, g_post_ref, s_in, c_in, n_in, m_in,
         y_ref, s_out, c_out, n_out, m_out,
         h_ref, proj_ref, mix_ref, gate_ref, bcum_ref, gate_t_ref, bcum_t_ref) = refs

    masks = _Masks(seg)
    scale = HEAD_DIM ** -0.5
    tri = jnp.where(masks.causal, 1.0, 0.0).astype(F32)
    cos = cos_ref[...]
    sin = sin_ref[...]

    for blk in range(n_blk):
        rows = pl.ds(blk * BLK, BLK)
        h_ref[rows, :] = _rmsnorm(x_ref[blk], g_pre_ref[...]).astype(BF16)
    gate_ref[...] = _dot(h_ref[...], w_gate_ref[...]) + b_gate_ref[...]
    for blk in range(n_blk):
        rows = pl.ds(blk * BLK, BLK)
        gates = gate_ref[rows, :]
        bcum = _dot(tri, jax.nn.log_sigmoid(gates), precision=lax.Precision.HIGHEST)
        bcum_ref[rows, :] = bcum
        gate_t_ref[blk] = gates.T
        bcum_t_ref[blk] = bcum.T

    def rotary(t):
        return t * cos + pltpu.roll(t, HEAD_DIM // 2, axis=1) * sin

    def seq_rows(j):
        return slice(j * seg, (j + 1) * seg)

    def lane_mask(j):
        c = lax.broadcasted_iota(jnp.int32, (BLK, BLK), 1)
        return (c >= j * seg) & (c < (j + 1) * seg)

    def ret_head(blk, head, col):
        rows = pl.ds(pl.multiple_of(blk * BLK, BLK), BLK)
        lg = _ret_log_gamma(head)
        q = rotary(proj_ref[rows, col:col + HEAD_DIM])
        k = rotary(proj_ref[rows, GROUP_W + col:GROUP_W + col + HEAD_DIM]) * scale
        v = proj_ref[rows, 2 * GROUP_W + col:2 * GROUP_W + col + HEAD_DIM]
        g = proj_ref[rows, 3 * GROUP_W + col:3 * GROUP_W + col + HEAD_DIM]
        qb, kb, vb = q.astype(BF16), k.astype(BF16), v.astype(BF16)
        dmat = jnp.where(masks.causal, jnp.exp(jnp.where(masks.causal, masks.diff * lg, 0.0)), 0.0)
        xi = jnp.exp((masks.tpos + 1.0) * lg)
        zeta = jnp.exp((float(seg - 1) - masks.tpos) * lg)
        chunk_decay = math.exp(seg * lg)
        scores = _dot_nt(qb, kb) * dmat
        out = _dot(scores.astype(BF16), vb)
        kz_t = (k * zeta).T
        inter = []
        for j in range(n_seq):
            seq = blk * n_seq + j
            s_prev = s_in[seq, head]
            inter.append(_dot(q[seq_rows(j)].astype(BF16), s_prev.astype(BF16)))
            kz_j = kz_t if n_seq == 1 else jnp.where(lane_mask(j), kz_t, 0.0)
            s_out[seq, head] = s_prev * chunk_decay + _dot(kz_j.astype(BF16), vb)
        inter = inter[0] if n_seq == 1 else jnp.concatenate(inter, axis=0)
        out = out + inter * xi
        y = _groupnorm(out, g_ret_ref[:, head * HEAD_DIM:(head + 1) * HEAD_DIM]) * (g * jax.nn.sigmoid(g))
        mix_ref[rows, head * HEAD_DIM:(head + 1) * HEAD_DIM] = y.astype(BF16)

    def ml_head(blk, head, col):
        rows = pl.ds(pl.multiple_of(blk * BLK, BLK), BLK)
        q = proj_ref[rows, col:col + HEAD_DIM]
        k = proj_ref[rows, GROUP_W + col:GROUP_W + col + HEAD_DIM] * scale
        v = proj_ref[rows, 2 * GROUP_W + col:2 * GROUP_W + col + HEAD_DIM]
        o = proj_ref[rows, 3 * GROUP_W + col:3 * GROUP_W + col + HEAD_DIM]
        qb, kb, vb = q.astype(BF16), k.astype(BF16), v.astype(BF16)
        i_col = gate_ref[rows, head:head + 1]
        b_col = bcum_ref[rows, N_ML_HEADS + head:N_ML_HEADS + head + 1]
        i_row = gate_t_ref[blk, head:head + 1, :]
        b_row = bcum_t_ref[blk, N_ML_HEADS + head:N_ML_HEADS + head + 1, :]
        m_prev_seq = m_in[blk, head:head + 1, :]
        m_prev = _rows_from_seq(masks.col_is_rowseq, m_prev_seq)
        logw = jnp.where(masks.causal, b_col - b_row + i_row, -jnp.inf)
        m_t = jnp.maximum(b_col + m_prev, jnp.max(logw, axis=1, keepdims=True))
        w = jnp.exp(logw - m_t)
        inter_w = jnp.exp(b_col + m_prev - m_t)
        s = _dot_nt(qb, kb) * w
        num = _dot(s.astype(BF16), vb)
        den = jnp.sum(s, axis=1, keepdims=True)
        m_new_seq = _seq_from_last_row(masks.row_is_last_of_colseq, m_t)
        b_last_seq = _seq_from_last_row(masks.row_is_last_of_colseq, b_col)
        decay_seq = jnp.exp(b_last_seq + m_prev_seq - m_new_seq)
        m_new = _rows_from_seq(masks.col_is_rowseq, m_new_seq)
        b_last = jnp.sum(jnp.where(masks.col_is_last_of_rowseq, b_row, 0.0), axis=1, keepdims=True)
        wl = jnp.exp(b_last - b_col + i_col - m_new)
        vw_t = (v * wl).T
        kw = k * wl
        qc, qn = [], []
        for j in range(n_seq):
            seq = blk * n_seq + j
            c_prev = c_in[seq, head]
            n_prev = n_in[seq, head:head + 1, :]
            qc.append(_dot_nt(q[seq_rows(j)].astype(BF16), c_prev.astype(BF16)))
            qn.append(jnp.sum(q[seq_rows(j)] * n_prev, axis=1, keepdims=True))
            decay = decay_seq[:, j:j + 1]
            vw_j = vw_t if n_seq == 1 else jnp.where(lane_mask(j), vw_t, 0.0)
            c_out[seq, head] = decay * c_prev + _dot(vw_j.astype(BF16), kb)
            n_out[seq, head:head + 1, :] = decay * n_prev + jnp.sum(kw[seq_rows(j)], axis=0, keepdims=True)
        qc = qc[0] if n_seq == 1 else jnp.concatenate(qc, axis=0)
        qn = qn[0] if n_seq == 1 else jnp.concatenate(qn, axis=0)
        m_out[blk, head:head + 1, :] = m_new_seq
        num = num + inter_w * qc
        den = jnp.maximum(jnp.abs(den + inter_w * qn), jnp.exp(-m_t))
        hout = num / den
        y = _groupnorm(hout, g_ml_ref[:, head * HEAD_DIM:(head + 1) * HEAD_DIM]) * jax.nn.sigmoid(o)
        mix_ref[rows, RET_W + head * HEAD_DIM:RET_W + (head + 1) * HEAD_DIM] = y.astype(BF16)

    n_ret_groups = N_RET_HEADS // HEADS_PER_GROUP
    n_ml_groups = N_ML_HEADS // HEADS_PER_GROUP
    for grp in range(n_ret_groups + n_ml_groups):
        proj_ref[...] = _dot(h_ref[...], w_grp_ref[grp])
        is_ret = grp < n_ret_groups
        first_head = (grp if is_ret else grp - n_ret_groups) * HEADS_PER_GROUP
        head_fn = ret_head if is_ret else ml_head

        def blk_body(blk, carry, head_fn=head_fn, first_head=first_head):
            for hh in range(HEADS_PER_GROUP):
                head_fn(blk, first_head + hh, hh * HEAD_DIM)
            return carry

        if n_blk == 1:
            blk_body(0, 0)
        else:
            lax.fori_loop(0, n_blk, blk_body, 0)

    proj_ref[...] = _dot(mix_ref[...], w_out_ref[...])
    for blk in range(n_blk):
        rows = pl.ds(blk * BLK, BLK)
        y_ref[blk] = x_ref[blk] + _rmsnorm(proj_ref[rows, :], g_post_ref[...])


def _ffn_kernel(*refs, seg, n_blk, fresh_state):
    n_seq_tile = n_blk * BLK // seg
    rows_tile = n_blk * BLK
    if fresh_state:
        (x_ref, g_pre_ref, w_up_ref, conv_w_ref, conv_b_ref, w_down_ref, g_post_ref,
         y_ref, buf_out, h_ref, act_ref, ffn_ref) = refs
        buf_in = buf_out

        @pl.when(pl.program_id(1) == 0)
        def _():
            buf_out[...] = jnp.zeros(buf_out.shape, F32)
    else:
        (x_ref, g_pre_ref, w_up_ref, conv_w_ref, conv_b_ref, w_down_ref, g_post_ref, buf_in,
         y_ref, buf_out, h_ref, act_ref, ffn_ref) = refs

    for blk in range(n_blk):
        rows = pl.ds(blk * BLK, BLK)
        h_ref[rows, :] = _rmsnorm(x_ref[blk], g_pre_ref[...]).astype(BF16)

    tpos = lax.broadcasted_iota(jnp.int32, (n_seq_tile, seg, FF_BLK), 1)

    def conv(cols):
        up = _dot(h_ref[...], w_up_ref[:, cols])
        prev0 = buf_in[:, 0:1, cols]
        prev1 = buf_in[:, 1:2, cols]
        up3 = up.reshape(n_seq_tile, seg, FF_BLK)
        sh1 = pltpu.roll(up, 1, axis=0).reshape(n_seq_tile, seg, FF_BLK)
        sh2 = pltpu.roll(up, 2, axis=0).reshape(n_seq_tile, seg, FF_BLK)
        sh1 = jnp.where(tpos == 0, prev1, sh1)
        sh2 = jnp.where(tpos == 0, prev0, jnp.where(tpos == 1, prev1, sh2))
        buf_out[:, 0:1, cols] = up3[:, seg - 2:seg - 1, :]
        buf_out[:, 1:2, cols] = up3[:, seg - 1:seg, :]
        w = conv_w_ref[:, cols]
        out = sh2 * w[0:1, :] + sh1 * w[1:2, :] + up3 * w[2:3, :] + conv_b_ref[:, cols]
        return out.reshape(rows_tile, FF_BLK)

    for j in range(D_FF // FF_BLK):
        gate = conv(slice(j * FF_BLK, (j + 1) * FF_BLK))
        val = conv(slice(D_FF + j * FF_BLK, D_FF + (j + 1) * FF_BLK))
        act_ref[:, j * FF_BLK:(j + 1) * FF_BLK] = (jax.nn.gelu(gate, approximate=True) * val).astype(BF16)

    ffn_ref[...] = _dot(act_ref[...], w_down_ref[...])
    for blk in range(n_blk):
        rows = pl.ds(blk * BLK, BLK)
        y_ref[blk] = x_ref[blk] + _rmsnorm(ffn_ref[rows, :], g_post_ref[...])


def _const_spec(shape):
    zeros = (0,) * len(shape)
    return pl.BlockSpec(shape, lambda *_: zeros, pipeline_mode=pl.Buffered(1))


def _compiler_params(n_grid_dims):
    return pltpu.CompilerParams(
        dimension_semantics=("arbitrary",) * n_grid_dims,
        vmem_limit_bytes=VMEM_LIMIT_BYTES)


def _mixer_scratch(n_blk):
    rows = n_blk * BLK
    return [
        pltpu.VMEM((rows, D_MODEL), BF16),
        pltpu.VMEM((rows, 4 * GROUP_W), F32),
        pltpu.VMEM((rows, RET_W + ML_W), BF16),
        pltpu.VMEM((rows, BLK), F32),
        pltpu.VMEM((rows, BLK), F32),
        pltpu.VMEM((n_blk, BLK, BLK), F32),
        pltpu.VMEM((n_blk, BLK, BLK), F32),
    ]


def _mixer_weight_specs():
    return [
        _const_spec((1, D_MODEL)),
        _const_spec((4, D_MODEL, 4 * GROUP_W)),
        _const_spec((D_MODEL, BLK)),
        _const_spec((1, BLK)),
        _const_spec((1, RET_W)),
        _const_spec((1, ML_W)),
        _const_spec((RET_W + ML_W, D_MODEL)),
        _const_spec((1, D_MODEL)),
    ]


def _state_shapes(n_seqs, n_blocks):
    return [
        jax.ShapeDtypeStruct((n_seqs, N_RET_HEADS, HEAD_DIM, HEAD_DIM), F32),
        jax.ShapeDtypeStruct((n_seqs, N_ML_HEADS, HEAD_DIM, HEAD_DIM), F32),
        jax.ShapeDtypeStruct((n_seqs, N_ML_HEADS, HEAD_DIM), F32),
        jax.ShapeDtypeStruct((n_blocks, N_ML_HEADS, BLK), F32),
    ]


def _state_specs(n_seq_tile, n_blk, index):
    return [
        pl.BlockSpec((n_seq_tile, N_RET_HEADS, HEAD_DIM, HEAD_DIM), lambda *g: (index(*g), 0, 0, 0)),
        pl.BlockSpec((n_seq_tile, N_ML_HEADS, HEAD_DIM, HEAD_DIM), lambda *g: (index(*g), 0, 0, 0)),
        pl.BlockSpec((n_seq_tile, N_ML_HEADS, HEAD_DIM), lambda *g: (index(*g), 0, 0)),
        pl.BlockSpec((n_blk, N_ML_HEADS, BLK), lambda *g: (index(*g), 0, 0)),
    ]


def _prompt_mixer(x, cos, sin, weights, n_blk):
    batch, seq_len, _ = x.shape
    grid = (batch // n_blk, seq_len // BLK)
    x_spec = pl.BlockSpec((n_blk, BLK, D_MODEL), lambda g, c: (g, c, 0))
    rope_spec = pl.BlockSpec((BLK, HEAD_DIM), lambda g, c: (c, 0))
    return pl.pallas_call(
        functools.partial(_mixer_kernel, seg=BLK, n_blk=n_blk, fresh_state=True),
        grid=grid,
        in_specs=[x_spec, rope_spec, rope_spec] + _mixer_weight_specs(),
        out_specs=[x_spec] + _state_specs(n_blk, n_blk, lambda g, c: g),
        out_shape=[jax.ShapeDtypeStruct(x.shape, F32)] + _state_shapes(batch, batch),
        scratch_shapes=_mixer_scratch(n_blk),
        compiler_params=_compiler_params(2),
        name="prompt_mixer",
    )(x, cos, sin, *weights)


def _sample_mixer(x, cos, sin, weights, states, seg, n_blk):
    n_blocks = x.shape[0]
    n_seq_tile = n_blk * BLK // seg
    grid = (n_blocks // n_blk,)
    x_spec = pl.BlockSpec((n_blk, BLK, D_MODEL), lambda g: (g, 0, 0))
    rope_spec = _const_spec((BLK, HEAD_DIM))
    state_specs = _state_specs(n_seq_tile, n_blk, lambda g: g)
    return pl.pallas_call(
        functools.partial(_mixer_kernel, seg=seg, n_blk=n_blk, fresh_state=False),
        grid=grid,
        in_specs=[x_spec, rope_spec, rope_spec] + _mixer_weight_specs() + state_specs,
        out_specs=[x_spec] + state_specs,
        out_shape=[jax.ShapeDtypeStruct(x.shape, F32)]
        + _state_shapes(n_blocks * BLK // seg, n_blocks),
        scratch_shapes=_mixer_scratch(n_blk),
        compiler_params=_compiler_params(1),
        name="sample_mixer",
    )(x, cos, sin, *weights, *states)


def _ffn_scratch(n_blk):
    rows = n_blk * BLK
    return [
        pltpu.VMEM((rows, D_MODEL), BF16),
        pltpu.VMEM((rows, D_FF), BF16),
        pltpu.VMEM((rows, D_MODEL), F32),
    ]


def _ffn_weight_specs():
    return [
        _const_spec((1, D_MODEL)),
        _const_spec((D_MODEL, 2 * D_FF)),
        _const_spec((CONV_W, 2 * D_FF)),
        _const_spec((1, 2 * D_FF)),
        _const_spec((D_FF, D_MODEL)),
        _const_spec((1, D_MODEL)),
    ]


def _prompt_ffn(x, weights, n_blk):
    batch, seq_len, _ = x.shape
    grid = (batch // n_blk, seq_len // BLK)
    x_spec = pl.BlockSpec((n_blk, BLK, D_MODEL), lambda g, c: (g, c, 0))
    buf_spec = pl.BlockSpec((n_blk, CONV_W - 1, 2 * D_FF), lambda g, c: (g, 0, 0))
    return pl.pallas_call(
        functools.partial(_ffn_kernel, seg=BLK, n_blk=n_blk, fresh_state=True),
        grid=grid,
        in_specs=[x_spec] + _ffn_weight_specs(),
        out_specs=[x_spec, buf_spec],
        out_shape=[jax.ShapeDtypeStruct(x.shape, F32),
                   jax.ShapeDtypeStruct((batch, CONV_W - 1, 2 * D_FF), F32)],
        scratch_shapes=_ffn_scratch(n_blk),
        compiler_params=_compiler_params(2),
        name="prompt_ffn",
    )(x, *weights)


def _sample_ffn(x, weights, conv_buf, seg, n_blk):
    n_blocks = x.shape[0]
    n_seq_tile = n_blk * BLK // seg
    grid = (n_blocks // n_blk,)
    x_spec = pl.BlockSpec((n_blk, BLK, D_MODEL), lambda g: (g, 0, 0))
    buf_spec = pl.BlockSpec((n_seq_tile, CONV_W - 1, 2 * D_FF), lambda g: (g, 0, 0))
    return pl.pallas_call(
        functools.partial(_ffn_kernel, seg=seg, n_blk=n_blk, fresh_state=False),
        grid=grid,
        in_specs=[x_spec] + _ffn_weight_specs() + [buf_spec],
        out_specs=[x_spec, buf_spec],
        out_shape=[jax.ShapeDtypeStruct(x.shape, F32),
                   jax.ShapeDtypeStruct(conv_buf.shape, F32)],
        scratch_shapes=_ffn_scratch(n_blk),
        compiler_params=_compiler_params(1),
        name="sample_ffn",
    )(x, *weights, conv_buf)


def _rope_tables(pos):
    freqs = ROPE_BASE ** (-jnp.arange(0, HEAD_DIM, 2, dtype=F32) / HEAD_DIM)
    ang = pos.astype(F32)[:, None] * freqs[None, :]
    cos, sin = jnp.cos(ang), jnp.sin(ang)
    return jnp.concatenate([cos, cos], axis=-1), jnp.concatenate([-sin, sin], axis=-1)


def _group_in_proj(w_in):
    segs = w_in[:, :4 * RET_W + 4 * ML_W].reshape(D_MODEL, 2, 4, N_RET_HEADS // HEADS_PER_GROUP, GROUP_W)
    grouped = jnp.transpose(segs, (1, 3, 0, 2, 4)).reshape(4, D_MODEL, 4 * GROUP_W)
    return grouped.astype(BF16)


def kernel(x_prompt, x_sample, state_ret, state_mlstm_C, state_mlstm_n, state_mlstm_m, cache_ffn_conv, pre_mix_gain, w_in, b_gates, ret_head_gain, mlstm_head_gain, w_out, post_mix_gain, pre_ffn_gain, w_up, conv_w, conv_b, w_down, post_ffn_gain):
    depth = w_in.shape[0]
    assert depth == 1
    batch, seq_len, _ = x_prompt.shape
    dec_batch, dec_seq, _ = x_sample.shape
    assert seq_len % BLK == 0 and BLK % dec_seq == 0 and (dec_batch * dec_seq) % BLK == 0
    seqs_per_blk = BLK // dec_seq
    n_sample_blocks = dec_batch * dec_seq // BLK
    layer = 0

    n_gate = 2 * N_ML_HEADS
    w_gate = jnp.pad(w_in[layer][:, -n_gate:], ((0, 0), (0, BLK - n_gate))).astype(BF16)
    b_gate = jnp.pad(b_gates[layer], (0, BLK - n_gate)).reshape(1, BLK)
    mixer_weights = (
        pre_mix_gain[layer].reshape(1, D_MODEL), _group_in_proj(w_in[layer]), w_gate, b_gate,
        ret_head_gain[layer].reshape(1, RET_W), mlstm_head_gain[layer].reshape(1, ML_W),
        w_out[layer].astype(BF16), post_mix_gain[layer].reshape(1, D_MODEL))
    ffn_weights = (
        pre_ffn_gain[layer].reshape(1, D_MODEL), w_up[layer].astype(BF16), conv_w[layer],
        conv_b[layer].reshape(1, 2 * D_FF), w_down[layer].astype(BF16),
        post_ffn_gain[layer].reshape(1, D_MODEL))

    cos_p, sin_p = _rope_tables(jnp.arange(seq_len, dtype=jnp.int32))
    x1_p, s_p, c_p, n_p, m_p = _prompt_mixer(x_prompt, cos_p, sin_p, mixer_weights, n_blk=4)
    y_p, buf_p = _prompt_ffn(x1_p, ffn_weights, n_blk=4)
    m_p = m_p[:, :, 0]

    pos_s = PAST_LEN + jnp.arange(dec_seq, dtype=jnp.int32)
    cos_s, sin_s = _rope_tables(jnp.tile(pos_s, seqs_per_blk))
    m_blocks = jnp.transpose(state_mlstm_m[layer].reshape(n_sample_blocks, seqs_per_blk, N_ML_HEADS), (0, 2, 1))
    m_blocks = jnp.pad(m_blocks, ((0, 0), (0, 0), (0, BLK - seqs_per_blk)))
    xs = x_sample.reshape(n_sample_blocks, BLK, D_MODEL)
    x1_s, s_s, c_s, n_s, m_s = _sample_mixer(
        xs, cos_s, sin_s, mixer_weights,
        (state_ret[layer], state_mlstm_C[layer], state_mlstm_n[layer], m_blocks),
        seg=dec_seq, n_blk=1)
    y_s, buf_s = _sample_ffn(x1_s, ffn_weights, cache_ffn_conv[layer], seg=dec_seq, n_blk=1)
    y_s = y_s.reshape(x_sample.shape)
    m_s = jnp.transpose(m_s[:, :, :seqs_per_blk], (0, 2, 1)).reshape(dec_batch, N_ML_HEADS)

    return (y_p, y_s, s_p[None], s_s[None], c_p[None], c_s[None], n_p[None], n_s[None],
            m_p[None], m_s[None], buf_p[None], buf_s[None])
```

```python
import functools
import math

import jax
import jax.numpy as jnp
from jax import lax
from jax.experimental import pallas as pl
from jax.experimental.pallas import tpu as pltpu

D_MODEL = 1024
HEAD_DIM = 128
N_RET_HEADS = 4
N_ML_HEADS = 4
RET_W = N_RET_HEADS * HEAD_DIM
ML_W = N_ML_HEADS * HEAD_DIM
D_FF = 2816
CONV_W = 3
PAST_LEN = 16384
ROPE_BASE = 10000.0
EPS = 1e-6
M_INIT = -1e30

BLK = 128
HEADS_PER_GROUP = 2
GROUP_W = HEADS_PER_GROUP * HEAD_DIM
FF_BLK = 256
VMEM_LIMIT_BYTES = 56 * 1024 * 1024

F32 = jnp.float32
BF16 = jnp.bfloat16


def _dot(a, b, precision=None):
    return jnp.dot(a, b, preferred_element_type=F32, precision=precision)


def _dot_nt(a, b):
    return lax.dot_general(a, b, (((1,), (1,)), ((), ())), preferred_element_type=F32)


def _rmsnorm(x, g):
    return x * lax.rsqrt(jnp.mean(x * x, axis=-1, keepdims=True) + EPS) * g


def _groupnorm(h, g):
    mu = jnp.mean(h, axis=-1, keepdims=True)
    d = h - mu
    var = jnp.mean(d * d, axis=-1, keepdims=True)
    return d * lax.rsqrt(var + EPS) * g


def _ret_log_gamma(h):
    return math.log(1.0 - 2.0 ** (-5.0 - h))


class _Masks:
    def __init__(self, seg):
        shift = seg.bit_length() - 1
        r = lax.broadcasted_iota(jnp.int32, (BLK, BLK), 0)
        c = lax.broadcasted_iota(jnp.int32, (BLK, BLK), 1)
        rseq = r >> shift
        self.causal = (rseq == (c >> shift)) & (r >= c)
        self.tpos = (r & (seg - 1)).astype(F32)
        self.diff = (r - c).astype(F32)
        self.col_is_last_of_rowseq = c == (rseq * seg + (seg - 1))
        self.col_is_rowseq = c == rseq
        self.row_is_last_of_colseq = r == (c * seg + (seg - 1))


def _rows_from_seq(mask_col_is_rowseq, seq_row):
    return jnp.sum(jnp.where(mask_col_is_rowseq, seq_row, 0.0), axis=1, keepdims=True)


def _seq_from_last_row(mask_row_is_last_of_colseq, col):
    return jnp.sum(jnp.where(mask_row_is_last_of_colseq, col, 0.0), axis=0, keepdims=True)


def _mixer_kernel(*refs, seg, n_blk, fresh_state):
    n_seq = BLK // seg
    if fresh_state:
        (x_ref, cos_ref, sin_ref, g_pre_ref, w_grp_ref, w_gate_ref, b_gate_ref, g_ret_ref, g_ml_ref,
         w_out_ref, g_post_ref,
         y_ref, s_out, c_out, n_out, m_out,
         h_ref, proj_ref, mix_ref, gate_ref, bcum_ref, gate_t_ref, bcum_t_ref) = refs
        s_in, c_in, n_in, m_in = s_out, c_out, n_out, m_out

        @pl.when(pl.program_id(1) == 0)
        def _():
            s_out[...] = jnp.zeros(s_out.shape, F32)
            c_out[...] = jnp.zeros(c_out.shape, F32)
            n_out[...] = jnp.zeros(n_out.shape, F32)
            m_out[...] = jnp.full(m_out.shape, M_INIT, F32)
    else:
        (x_ref, cos_ref, sin_ref, g_pre_ref, w_grp_ref, w_gate_ref, b_gate_ref, g_ret_ref, g_ml_ref,
         w_out_ref, g_post_ref, s_in, c_in, n_in, m_in,
         y_ref, s_out, c_out, n_out, m_out,
         h_ref, proj_ref, mix_ref, gate_ref, bcum_ref, gate_t_ref, bcum_t_ref) = refs

    masks = _Masks(seg)
    scale = HEAD_DIM ** -0.5
    tri = jnp.where(masks.causal, 1.0, 0.0).astype(F32)
    cos = cos_ref[...]
    sin = sin_ref[...]

    for blk in range(n_blk):
        rows = pl.ds(blk * BLK, BLK)
        h_ref[rows, :] = _rmsnorm(x_ref[blk], g_pre_ref[...]).astype(BF16)
    gate_ref[...] = _dot(h_ref[...], w_gate_ref[...]) + b_gate_ref[...]
    for blk in range(n_blk):
        rows = pl.ds(blk * BLK, BLK)
        gates = gate_ref[rows, :]
        bcum = _dot(tri, jax.nn.log_sigmoid(gates), precision=lax.Precision.HIGHEST)
        bcum_ref[rows, :] = bcum
        gate_t_ref[blk] = gates.T
        bcum_t_ref[blk] = bcum.T

    def rotary(t):
        return t * cos + pltpu.roll(t, HEAD_DIM // 2, axis=1) * sin

    def seq_rows(j):
        return slice(j * seg, (j + 1) * seg)

    def lane_mask(j):
        c = lax.broadcasted_iota(jnp.int32, (BLK, BLK), 1)
        return (c >= j * seg) & (c < (j + 1) * seg)

    def ret_head(blk, head, col):
        rows = pl.ds(blk * BLK, BLK)
        lg = _ret_log_gamma(head)
        q = rotary(proj_ref[rows, col:col + HEAD_DIM])
        k = rotary(proj_ref[rows, GROUP_W + col:GROUP_W + col + HEAD_DIM]) * scale
        v = proj_ref[rows, 2 * GROUP_W + col:2 * GROUP_W + col + HEAD_DIM]
        g = proj_ref[rows, 3 * GROUP_W + col:3 * GROUP_W + col + HEAD_DIM]
        qb, kb, vb = q.astype(BF16), k.astype(BF16), v.astype(BF16)
        dmat = jnp.where(masks.causal, jnp.exp(jnp.where(masks.causal, masks.diff * lg, 0.0)), 0.0)
        xi = jnp.exp((masks.tpos + 1.0) * lg)
        zeta = jnp.exp((float(seg - 1) - masks.tpos) * lg)
        chunk_decay = math.exp(seg * lg)
        scores = _dot_nt(qb, kb) * dmat
        out = _dot(scores.astype(BF16), vb)
        kz_t = (k * zeta).T
        inter = []
        for j in range(n_seq):
            seq = blk * n_seq + j
            s_prev = s_in[seq, head]
            inter.append(_dot(q[seq_rows(j)].astype(BF16), s_prev.astype(BF16)))
            kz_j = kz_t if n_seq == 1 else jnp.where(lane_mask(j), kz_t, 0.0)
            s_out[seq, head] = s_prev * chunk_decay + _dot(kz_j.astype(BF16), vb)
        inter = inter[0] if n_seq == 1 else jnp.concatenate(inter, axis=0)
        out = out + inter * xi
        y = _groupnorm(out, g_ret_ref[:, head * HEAD_DIM:(head + 1) * HEAD_DIM]) * (g * jax.nn.sigmoid(g))
        mix_ref[rows, head * HEAD_DIM:(head + 1) * HEAD_DIM] = y.astype(BF16)

    def ml_head(blk, head, col):
        rows = pl.ds(blk * BLK, BLK)
        q = proj_ref[rows, col:col + HEAD_DIM]
        k = proj_ref[rows, GROUP_W + col:GROUP_W + col + HEAD_DIM] * scale
        v = proj_ref[rows, 2 * GROUP_W + col:2 * GROUP_W + col + HEAD_DIM]
        o = proj_ref[rows, 3 * GROUP_W + col:3 * GROUP_W + col + HEAD_DIM]
        qb, kb, vb = q.astype(BF16), k.astype(BF16), v.astype(BF16)
        i_col = gate_ref[rows, head:head + 1]
        b_col = bcum_ref[rows, N_ML_HEADS + head:N_ML_HEADS + head + 1]
        i_row = gate_t_ref[blk, head:head + 1, :]
        b_row = bcum_t_ref[blk, N_ML_HEADS + head:N_ML_HEADS + head + 1, :]
        m_prev_seq = m_in[blk, head:head + 1, :]
        if n_seq == 1:
            m_prev = m_prev_seq[:, 0:1]
        else:
            m_prev = _rows_from_seq(masks.col_is_rowseq, m_prev_seq)
        logw = jnp.where(masks.causal, b_col - b_row + i_row, -jnp.inf)
        m_t = jnp.maximum(b_col + m_prev, jnp.max(logw, axis=1, keepdims=True))
        w = jnp.exp(logw - m_t)
        inter_w = jnp.exp(b_col + m_prev - m_t)
        s = _dot_nt(qb, kb) * w
        num = _dot(s.astype(BF16), vb)
        den = jnp.sum(s, axis=1, keepdims=True)
        if n_seq == 1:
            m_new = m_t[BLK - 1:BLK, :]
            b_last = b_col[BLK - 1:BLK, :]
            decay_seq = jnp.exp(b_last + m_prev - m_new)
            m_new_seq = jnp.broadcast_to(m_new, (1, BLK))
        else:
            m_new_seq = _seq_from_last_row(masks.row_is_last_of_colseq, m_t)
            b_last_seq = _seq_from_last_row(masks.row_is_last_of_colseq, b_col)
            decay_seq = jnp.exp(b_last_seq + m_prev_seq - m_new_seq)
            m_new = _rows_from_seq(masks.col_is_rowseq, m_new_seq)
            b_last = jnp.sum(jnp.where(masks.col_is_last_of_rowseq, b_row, 0.0), axis=1, keepdims=True)
        wl = jnp.exp(b_last - b_col + i_col - m_new)
        vw_t = (v * wl).T
        kw = k * wl
        qc, qn = [], []
        for j in range(n_seq):
            seq = blk * n_seq + j
            c_prev = c_in[seq, head]
            n_prev = n_in[seq, head:head + 1, :]
            qc.append(_dot_nt(q[seq_rows(j)].astype(BF16), c_prev.astype(BF16)))
            qn.append(jnp.sum(q[seq_rows(j)] * n_prev, axis=1, keepdims=True))
            decay = decay_seq[:, j:j + 1]
            vw_j = vw_t if n_seq == 1 else jnp.where(lane_mask(j), vw_t, 0.0)
            c_out[seq, head] = decay * c_prev + _dot(vw_j.astype(BF16), kb)
            n_out[seq, head:head + 1, :] = decay * n_prev + jnp.sum(kw[seq_rows(j)], axis=0, keepdims=True)
        qc = qc[0] if n_seq == 1 else jnp.concatenate(qc, axis=0)
        qn = qn[0] if n_seq == 1 else jnp.concatenate(qn, axis=0)
        m_out[blk, head:head + 1, :] = m_new_seq
        num = num + inter_w * qc
        den = jnp.maximum(jnp.abs(den + inter_w * qn), jnp.exp(-m_t))
        hout = num / den
        y = _groupnorm(hout, g_ml_ref[:, head * HEAD_DIM:(head + 1) * HEAD_DIM]) * jax.nn.sigmoid(o)
        mix_ref[rows, RET_W + head * HEAD_DIM:RET_W + (head + 1) * HEAD_DIM] = y.astype(BF16)

    n_ret_groups = N_RET_HEADS // HEADS_PER_GROUP
    n_ml_groups = N_ML_HEADS // HEADS_PER_GROUP
    for grp in range(n_ret_groups + n_ml_groups):
        proj_ref[...] = _dot(h_ref[...], w_grp_ref[grp])
        is_ret = grp < n_ret_groups
        first_head = (grp if is_ret else grp - n_ret_groups) * HEADS_PER_GROUP
        head_fn = ret_head if is_ret else ml_head

        for blk in range(n_blk):
            for hh in range(HEADS_PER_GROUP):
                head_fn(blk, first_head + hh, hh * HEAD_DIM)

    proj_ref[...] = _dot(mix_ref[...], w_out_ref[...])
    for blk in range(n_blk):
        rows = pl.ds(blk * BLK, BLK)
        y_ref[blk] = x_ref[blk] + _rmsnorm(proj_ref[rows, :], g_post_ref[...])


def _ffn_kernel(*refs, seg, n_blk, fresh_state):
    n_seq_tile = n_blk * BLK // seg
    rows_tile = n_blk * BLK
    if fresh_state:
        (x_ref, g_pre_ref, w_up_ref, conv_w_ref, conv_b_ref, w_down_ref, g_post_ref,
         y_ref, buf_out, h_ref, act_ref, ffn_ref) = refs
        buf_in = buf_out

        @pl.when(pl.program_id(1) == 0)
        def _():
            buf_out[...] = jnp.zeros(buf_out.shape, F32)
    else:
        (x_ref, g_pre_ref, w_up_ref, conv_w_ref, conv_b_ref, w_down_ref, g_post_ref, buf_in,
         y_ref, buf_out, h_ref, act_ref, ffn_ref) = refs

    for blk in range(n_blk):
        rows = pl.ds(blk * BLK, BLK)
        h_ref[rows, :] = _rmsnorm(x_ref[blk], g_pre_ref[...]).astype(BF16)

    tpos = lax.broadcasted_iota(jnp.int32, (n_seq_tile, seg, FF_BLK), 1)

    def conv(cols):
        up = _dot(h_ref[...], w_up_ref[:, cols])
        prev0 = buf_in[:, 0:1, cols]
        prev1 = buf_in[:, 1:2, cols]
        up3 = up.reshape(n_seq_tile, seg, FF_BLK)
        sh1 = pltpu.roll(up, 1, axis=0).reshape(n_seq_tile, seg, FF_BLK)
        sh2 = pltpu.roll(up, 2, axis=0).reshape(n_seq_tile, seg, FF_BLK)
        sh1 = jnp.where(tpos == 0, prev1, sh1)
        sh2 = jnp.where(tpos == 0, prev0, jnp.where(tpos == 1, prev1, sh2))
        buf_out[:, 0:1, cols] = up3[:, seg - 2:seg - 1, :]
        buf_out[:, 1:2, cols] = up3[:, seg - 1:seg, :]
        w = conv_w_ref[:, cols]
        out = sh2 * w[0:1, :] + sh1 * w[1:2, :] + up3 * w[2:3, :] + conv_b_ref[:, cols]
        return out.reshape(rows_tile, FF_BLK)

    for j in range(D_FF // FF_BLK):
        gate = conv(slice(j * FF_BLK, (j + 1) * FF_BLK))
        val = conv(slice(D_FF + j * FF_BLK, D_FF + (j + 1) * FF_BLK))
        act_ref[:, j * FF_BLK:(j + 1) * FF_BLK] = (jax.nn.gelu(gate, approximate=True) * val).astype(BF16)

    ffn_ref[...] = _dot(act_ref[...], w_down_ref[...])
    for blk in range(n_blk):
        rows = pl.ds(blk * BLK, BLK)
        y_ref[blk] = x_ref[blk] + _rmsnorm(ffn_ref[rows, :], g_post_ref[...])


def _const_spec(shape):
    zeros = (0,) * len(shape)
    return pl.BlockSpec(shape, lambda *_: zeros, pipeline_mode=pl.Buffered(1))


def _compiler_params(n_grid_dims):
    return pltpu.CompilerParams(
        dimension_semantics=("arbitrary",) * n_grid_dims,
        vmem_limit_bytes=VMEM_LIMIT_BYTES)


def _mixer_scratch(n_blk):
    rows = n_blk * BLK
    return [
        pltpu.VMEM((rows, D_MODEL), BF16),
        pltpu.VMEM((rows, 4 * GROUP_W), F32),
        pltpu.VMEM((rows, RET_W + ML_W), BF16),
        pltpu.VMEM((rows, BLK), F32),
        pltpu.VMEM((rows, BLK), F32),
        pltpu.VMEM((n_blk, BLK, BLK), F32),
        pltpu.VMEM((n_blk, BLK, BLK), F32),
    ]


def _mixer_weight_specs():
    return [
        _const_spec((1, D_MODEL)),
        _const_spec((4, D_MODEL, 4 * GROUP_W)),
        _const_spec((D_MODEL, BLK)),
        _const_spec((1, BLK)),
        _const_spec((1, RET_W)),
        _const_spec((1, ML_W)),
        _const_spec((RET_W + ML_W, D_MODEL)),
        _const_spec((1, D_MODEL)),
    ]


def _state_shapes(n_seqs, n_blocks):
    return [
        jax.ShapeDtypeStruct((n_seqs, N_RET_HEADS, HEAD_DIM, HEAD_DIM), F32),
        jax.ShapeDtypeStruct((n_seqs, N_ML_HEADS, HEAD_DIM, HEAD_DIM), F32),
        jax.ShapeDtypeStruct((n_seqs, N_ML_HEADS, HEAD_DIM), F32),
        jax.ShapeDtypeStruct((n_blocks, N_ML_HEADS, BLK), F32),
    ]


def _state_specs(n_seq_tile, n_blk, index):
    return [
        pl.BlockSpec((n_seq_tile, N_RET_HEADS, HEAD_DIM, HEAD_DIM), lambda *g: (index(*g), 0, 0, 0)),
        pl.BlockSpec((n_seq_tile, N_ML_HEADS, HEAD_DIM, HEAD_DIM), lambda *g: (index(*g), 0, 0, 0)),
        pl.BlockSpec((n_seq_tile, N_ML_HEADS, HEAD_DIM), lambda *g: (index(*g), 0, 0)),
        pl.BlockSpec((n_blk, N_ML_HEADS, BLK), lambda *g: (index(*g), 0, 0)),
    ]


def _prompt_mixer(x, cos, sin, weights, n_blk):
    batch, seq_len, _ = x.shape
    grid = (batch // n_blk, seq_len // BLK)
    x_spec = pl.BlockSpec((n_blk, BLK, D_MODEL), lambda g, c: (g, c, 0))
    rope_spec = pl.BlockSpec((BLK, HEAD_DIM), lambda g, c: (c, 0))
    return pl.pallas_call(
        functools.partial(_mixer_kernel, seg=BLK, n_blk=n_blk, fresh_state=True),
        grid=grid,
        in_specs=[x_spec, rope_spec, rope_spec] + _mixer_weight_specs(),
        out_specs=[x_spec] + _state_specs(n_blk, n_blk, lambda g, c: g),
        out_shape=[jax.ShapeDtypeStruct(x.shape, F32)] + _state_shapes(batch, batch),
        scratch_shapes=_mixer_scratch(n_blk),
        compiler_params=_compiler_params(2),
        name="prompt_mixer",
    )(x, cos, sin, *weights)


def _sample_mixer(x, cos, sin, weights, states, seg, n_blk):
    n_blocks = x.shape[0]
    n_seq_tile = n_blk * BLK // seg
    grid = (n_blocks // n_blk,)
    x_spec = pl.BlockSpec((n_blk, BLK, D_MODEL), lambda g: (g, 0, 0))
    rope_spec = _const_spec((BLK, HEAD_DIM))
    state_specs = _state_specs(n_seq_tile, n_blk, lambda g: g)
    return pl.pallas_call(
        functools.partial(_mixer_kernel, seg=seg, n_blk=n_blk, fresh_state=False),
        grid=grid,
        in_specs=[x_spec, rope_spec, rope_spec] + _mixer_weight_specs() + state_specs,
        out_specs=[x_spec] + state_specs,
        out_shape=[jax.ShapeDtypeStruct(x.shape, F32)]
        + _state_shapes(n_blocks * BLK // seg, n_blocks),
        scratch_shapes=_mixer_scratch(n_blk),
        compiler_params=_compiler_params(1),
        name="sample_mixer",
    )(x, cos, sin, *weights, *states)


def _ffn_scratch(n_blk):
    rows = n_blk * BLK
    return [
        pltpu.VMEM((rows, D_MODEL), BF16),
        pltpu.VMEM((rows, D_FF), BF16),
        pltpu.VMEM((rows, D_MODEL), F32),
    ]


def _ffn_weight_specs():
    return [
        _const_spec((1, D_MODEL)),
        _const_spec((D_MODEL, 2 * D_FF)),
        _const_spec((CONV_W, 2 * D_FF)),
        _const_spec((1, 2 * D_FF)),
        _const_spec((D_FF, D_MODEL)),
        _const_spec((1, D_MODEL)),
    ]


def _prompt_ffn(x, weights, n_blk):
    batch, seq_len, _ = x.shape
    grid = (batch // n_blk, seq_len // BLK)
    x_spec = pl.BlockSpec((n_blk, BLK, D_MODEL), lambda g, c: (g, c, 0))
    buf_spec = pl.BlockSpec((n_blk, CONV_W - 1, 2 * D_FF), lambda g, c: (g, 0, 0))
    return pl.pallas_call(
        functools.partial(_ffn_kernel, seg=BLK, n_blk=n_blk, fresh_state=True),
        grid=grid,
        in_specs=[x_spec] + _ffn_weight_specs(),
        out_specs=[x_spec, buf_spec],
        out_shape=[jax.ShapeDtypeStruct(x.shape, F32),
                   jax.ShapeDtypeStruct((batch, CONV_W - 1, 2 * D_FF), F32)],
        scratch_shapes=_ffn_scratch(n_blk),
        compiler_params=_compiler_params(2),
        name="prompt_ffn",
    )(x, *weights)


def _sample_ffn(x, weights, conv_buf, seg, n_blk):
    n_blocks = x.shape[0]
    n_seq_tile = n_blk * BLK // seg
    grid = (n_blocks // n_blk,)
    x_spec = pl.BlockSpec((n_blk, BLK, D_MODEL), lambda g: (g, 0, 0))
    buf_spec = pl.BlockSpec((n_seq_tile, CONV_W - 1, 2 * D_FF), lambda g: (g, 0, 0))
    return pl.pallas_call(
        functools.partial(_ffn_kernel, seg=seg, n_blk=n_blk, fresh_state=False),
        grid=grid,
        in_specs=[x_spec] + _ffn_weight_specs() + [buf_spec],
        out_specs=[x_spec, buf_spec],
        out_shape=[jax.ShapeDtypeStruct(x.shape, F32),
                   jax.ShapeDtypeStruct(conv_buf.shape, F32)],
        scratch_shapes=_ffn_scratch(n_blk),
        compiler_params=_compiler_params(1),
        name="sample_ffn",
    )(x, *weights, conv_buf)


def _rope_tables(pos):
    freqs = ROPE_BASE ** (-jnp.arange(0, HEAD_DIM, 2, dtype=F32) / HEAD_DIM)
    ang = pos.astype(F32)[:, None] * freqs[None, :]
    cos, sin = jnp.cos(ang), jnp.sin(ang)
    return jnp.concatenate([cos, cos], axis=-1), jnp.concatenate([-sin, sin], axis=-1)


def _group_in_proj(w_in):
    segs = w_in[:, :4 * RET_W + 4 * ML_W].reshape(D_MODEL, 2, 4, N_RET_HEADS // HEADS_PER_GROUP, GROUP_W)
    grouped = jnp.transpose(segs, (1, 3, 0, 2, 4)).reshape(4, D_MODEL, 4 * GROUP_W)
    return grouped.astype(BF16)


def kernel(x_prompt, x_sample, state_ret, state_mlstm_C, state_mlstm_n, state_mlstm_m, cache_ffn_conv, pre_mix_gain, w_in, b_gates, ret_head_gain, mlstm_head_gain, w_out, post_mix_gain, pre_ffn_gain, w_up, conv_w, conv_b, w_down, post_ffn_gain):
    depth = w_in.shape[0]
    assert depth == 1
    batch, seq_len, _ = x_prompt.shape
    dec_batch, dec_seq, _ = x_sample.shape
    assert seq_len % BLK == 0 and BLK % dec_seq == 0 and (dec_batch * dec_seq) % BLK == 0
    seqs_per_blk = BLK // dec_seq
    n_sample_blocks = dec_batch * dec_seq // BLK
    layer = 0

    n_gate = 2 * N_ML_HEADS
    w_gate = jnp.pad(w_in[layer][:, -n_gate:], ((0, 0), (0, BLK - n_gate))).astype(BF16)
    b_gate = jnp.pad(b_gates[layer], (0, BLK - n_gate)).reshape(1, BLK)
    mixer_weights = (
        pre_mix_gain[layer].reshape(1, D_MODEL), _group_in_proj(w_in[layer]), w_gate, b_gate,
        ret_head_gain[layer].reshape(1, RET_W), mlstm_head_gain[layer].reshape(1, ML_W),
        w_out[layer].astype(BF16), post_mix_gain[layer].reshape(1, D_MODEL))
    ffn_weights = (
        pre_ffn_gain[layer].reshape(1, D_MODEL), w_up[layer].astype(BF16), conv_w[layer],
        conv_b[layer].reshape(1, 2 * D_FF), w_down[layer].astype(BF16),
        post_ffn_gain[layer].reshape(1, D_MODEL))

    cos_p, sin_p = _rope_tables(jnp.arange(seq_len, dtype=jnp.int32))
    x1_p, s_p, c_p, n_p, m_p = _prompt_mixer(x_prompt, cos_p, sin_p, mixer_weights, n_blk=4)
    y_p, buf_p = _prompt_ffn(x1_p, ffn_weights, n_blk=4)
    m_p = m_p[:, :, 0]

    pos_s = PAST_LEN + jnp.arange(dec_seq, dtype=jnp.int32)
    cos_s, sin_s = _rope_tables(jnp.tile(pos_s, seqs_per_blk))
    m_blocks = jnp.transpose(state_mlstm_m[layer].reshape(n_sample_blocks, seqs_per_blk, N_ML_HEADS), (0, 2, 1))
    m_blocks = jnp.pad(m_blocks, ((0, 0), (0, 0), (0, BLK - seqs_per_blk)))
    xs = x_sample.reshape(n_sample_blocks, BLK, D_MODEL)
    x1_s, s_s, c_s, n_s, m_s = _sample_mixer(
        xs, cos_s, sin_s, mixer_weights,
        (state_ret[layer], state_mlstm_C[layer], state_mlstm_n[layer], m_blocks),
        seg=dec_seq, n_blk=1)
    y_s, buf_s = _sample_ffn(x1_s, ffn_weights, cache_ffn_conv[layer], seg=dec_seq, n_blk=1)
    y_s = y_s.reshape(x_sample.shape)
    m_s = jnp.transpose(m_s[:, :, :seqs_per_blk], (0, 2, 1)).reshape(dec_batch, N_ML_HEADS)

    return (y_p, y_s, s_p[None], s_s[None], c_p[None], c_s[None], n_p[None], n_s[None],
            m_p[None], m_s[None], buf_p[None], buf_s[None])
```

```python
import functools
import math

import jax
import jax.numpy as jnp
from jax import lax
from jax.experimental import pallas as pl
from jax.experimental.pallas import tpu as pltpu

D_MODEL = 1024
HEAD_DIM = 128
N_RET_HEADS = 4
N_ML_HEADS = 4
RET_W = N_RET_HEADS * HEAD_DIM
ML_W = N_ML_HEADS * HEAD_DIM
D_FF = 2816
CONV_W = 3
PAST_LEN = 16384
ROPE_BASE = 10000.0
EPS = 1e-6
M_INIT = -1e30

BLK = 128
HEADS_PER_GROUP = 2
GROUP_W = HEADS_PER_GROUP * HEAD_DIM
FF_BLK = 256
SUBLANES = 8
VMEM_LIMIT_BYTES = 56 * 1024 * 1024

F32 = jnp.float32
BF16 = jnp.bfloat16


def _dot(a, b, precision=None):
    return jnp.dot(a, b, preferred_element_type=F32, precision=precision)


def _dot_nt(a, b):
    return lax.dot_general(a, b, (((1,), (1,)), ((), ())), preferred_element_type=F32)


def _rmsnorm(x, g):
    return x * lax.rsqrt(jnp.mean(x * x, axis=-1, keepdims=True) + EPS) * g


def _groupnorm(h, g):
    mu = jnp.mean(h, axis=-1, keepdims=True)
    d = h - mu
    var = jnp.mean(d * d, axis=-1, keepdims=True)
    return d * lax.rsqrt(var + EPS) * g


def _ret_log_gamma(h):
    return math.log(1.0 - 2.0 ** (-5.0 - h))


class _Masks:
    def __init__(self, seg):
        shift = seg.bit_length() - 1
        r = lax.broadcasted_iota(jnp.int32, (BLK, BLK), 0)
        c = lax.broadcasted_iota(jnp.int32, (BLK, BLK), 1)
        rseq = r >> shift
        self.causal = (rseq == (c >> shift)) & (r >= c)
        self.tpos = (r & (seg - 1)).astype(F32)
        self.diff = (r - c).astype(F32)
        self.col_is_last_of_rowseq = c == (rseq * seg + (seg - 1))
        self.col_is_rowseq = c == rseq
        self.row_is_last_of_colseq = r == (c * seg + (seg - 1))


def _rows_from_seq(mask_col_is_rowseq, seq_row):
    return jnp.sum(jnp.where(mask_col_is_rowseq, seq_row, 0.0), axis=1, keepdims=True)


def _seq_from_last_row(mask_row_is_last_of_colseq, col):
    return jnp.sum(jnp.where(mask_row_is_last_of_colseq, col, 0.0), axis=0, keepdims=True)


def _mixer_kernel(*refs, seg, n_blk, fresh_state):
    n_seq = BLK // seg
    if fresh_state:
        (x_ref, cos_ref, sin_ref, g_pre_ref, w_in_ref, w_gate_ref, b_gate_ref, g_ret_ref, g_ml_ref,
         w_out_ref, g_post_ref,
         y_ref, s_out, c_out, n_out, m_out,
         h_ref, proj_ref, mix_ref, mixed_ref, gate_ref, bcum_ref, gate_t_ref, bcum_t_ref) = refs
        s_in, c_in, n_in, m_in = s_out, c_out, n_out, m_out

        @pl.when(pl.program_id(1) == 0)
        def _():
            s_out[...] = jnp.zeros(s_out.shape, F32)
            c_out[...] = jnp.zeros(c_out.shape, F32)
            n_out[...] = jnp.zeros(n_out.shape, F32)
            m_out[...] = jnp.full(m_out.shape, M_INIT, F32)
    else:
        (x_ref, cos_ref, sin_ref, g_pre_ref, w_in_ref, w_gate_ref, b_gate_ref, g_ret_ref, g_ml_ref,
         w_out_ref, g_post_ref, s_in, c_in, n_in, m_in,
         y_ref, s_out, c_out, n_out, m_out,
         h_ref, proj_ref, mix_ref, mixed_ref, gate_ref, bcum_ref, gate_t_ref, bcum_t_ref) = refs

    masks = _Masks(seg)
    scale = HEAD_DIM ** -0.5
    tri = jnp.where(masks.causal, 1.0, 0.0).astype(F32)
    cos = cos_ref[...]
    sin = sin_ref[...]

    for blk in range(n_blk):
        rows = pl.ds(blk * BLK, BLK)
        h_ref[rows, :] = _rmsnorm(x_ref[blk], g_pre_ref[...]).astype(BF16)
    gate_ref[...] = _dot(h_ref[...], w_gate_ref[...]) + b_gate_ref[...]
    for blk in range(n_blk):
        rows = pl.ds(blk * BLK, BLK)
        gates = gate_ref[rows, :]
        bcum = _dot(tri, jax.nn.log_sigmoid(gates), precision=lax.Precision.HIGHEST)
        bcum_ref[rows, :] = bcum
        gate_t_ref[blk] = gates.T
        bcum_t_ref[blk] = bcum.T

    def rotary(t):
        return t * cos + pltpu.roll(t, HEAD_DIM // 2, axis=1) * sin

    def seq_rows(j):
        return slice(j * seg, (j + 1) * seg)

    def lane_mask(j):
        c = lax.broadcasted_iota(jnp.int32, (BLK, BLK), 1)
        return (c >= j * seg) & (c < (j + 1) * seg)

    def ret_head(proj, blk, head, col):
        rows = pl.ds(blk * BLK, BLK)
        lg = _ret_log_gamma(head)
        q = rotary(proj[rows,col:col + HEAD_DIM])
        k = rotary(proj[rows,GROUP_W + col:GROUP_W + col + HEAD_DIM]) * scale
        v = proj[rows,2 * GROUP_W + col:2 * GROUP_W + col + HEAD_DIM]
        g = proj[rows,3 * GROUP_W + col:3 * GROUP_W + col + HEAD_DIM]
        qb, kb, vb = q.astype(BF16), k.astype(BF16), v.astype(BF16)
        dmat = jnp.where(masks.causal, jnp.exp(jnp.where(masks.causal, masks.diff * lg, 0.0)), 0.0)
        xi = jnp.exp((masks.tpos + 1.0) * lg)
        zeta = jnp.exp((float(seg - 1) - masks.tpos) * lg)
        chunk_decay = math.exp(seg * lg)
        scores = _dot_nt(qb, kb)
        kz_t = (k * zeta).T
        inter = []
        for j in range(n_seq):
            seq = blk * n_seq + j
            s_prev = s_in[seq, head]
            inter.append(_dot(q[seq_rows(j)].astype(BF16), s_prev.astype(BF16)))
            kz_j = kz_t if n_seq == 1 else jnp.where(lane_mask(j), kz_t, 0.0)
            s_out[seq, head] = s_prev * chunk_decay + _dot(kz_j.astype(BF16), vb)
        inter = inter[0] if n_seq == 1 else jnp.concatenate(inter, axis=0)
        yield
        out = _dot((scores * dmat).astype(BF16), vb)
        yield
        out = out + inter * xi
        y = _groupnorm(out, g_ret_ref[:, head * HEAD_DIM:(head + 1) * HEAD_DIM]) * (g * jax.nn.sigmoid(g))
        mix_ref[rows, head * HEAD_DIM:(head + 1) * HEAD_DIM] = y.astype(BF16)

    def ml_head(proj, blk, head, col):
        rows = pl.ds(blk * BLK, BLK)
        q = proj[rows,col:col + HEAD_DIM]
        k = proj[rows,GROUP_W + col:GROUP_W + col + HEAD_DIM] * scale
        v = proj[rows,2 * GROUP_W + col:2 * GROUP_W + col + HEAD_DIM]
        o = proj[rows,3 * GROUP_W + col:3 * GROUP_W + col + HEAD_DIM]
        qb, kb, vb = q.astype(BF16), k.astype(BF16), v.astype(BF16)
        i_col = gate_ref[rows, head:head + 1]
        b_col = bcum_ref[rows, N_ML_HEADS + head:N_ML_HEADS + head + 1]
        i_row = gate_t_ref[blk, head:head + 1, :]
        b_row = bcum_t_ref[blk, N_ML_HEADS + head:N_ML_HEADS + head + 1, :]
        m_prev_seq = m_in[blk, head:head + 1, :]
        if n_seq == 1:
            m_prev = m_prev_seq[:, 0:1]
        else:
            m_prev = _rows_from_seq(masks.col_is_rowseq, m_prev_seq)
        logw = jnp.where(masks.causal, b_col - b_row + i_row, -jnp.inf)
        m_t = jnp.maximum(b_col + m_prev, jnp.max(logw, axis=1, keepdims=True))
        w = jnp.exp(logw - m_t)
        inter_w = jnp.exp(b_col + m_prev - m_t)
        s = _dot_nt(qb, kb)
        if n_seq == 1:
            m_new = m_t[BLK - 1:BLK, :]
            b_last = b_col[BLK - 1:BLK, :]
            decay_seq = jnp.exp(b_last + m_prev - m_new)
            m_new_seq = jnp.broadcast_to(m_new, (1, BLK))
        else:
            m_new_seq = _seq_from_last_row(masks.row_is_last_of_colseq, m_t)
            b_last_seq = _seq_from_last_row(masks.row_is_last_of_colseq, b_col)
            decay_seq = jnp.exp(b_last_seq + m_prev_seq - m_new_seq)
            m_new = _rows_from_seq(masks.col_is_rowseq, m_new_seq)
            b_last = jnp.sum(jnp.where(masks.col_is_last_of_rowseq, b_row, 0.0), axis=1, keepdims=True)
        wl = jnp.exp(b_last - b_col + i_col - m_new)
        vw_t = (v * wl).T
        kw = k * wl
        qc, qn = [], []
        for j in range(n_seq):
            seq = blk * n_seq + j
            c_prev = c_in[seq, head]
            n_prev = n_in[seq, head:head + 1, :]
            qc.append(_dot_nt(q[seq_rows(j)].astype(BF16), c_prev.astype(BF16)))
            qn.append(jnp.sum(q[seq_rows(j)] * n_prev, axis=1, keepdims=True))
            decay = decay_seq[:, j:j + 1]
            n_out[seq, head:head + 1, :] = decay * n_prev + jnp.sum(kw[seq_rows(j)], axis=0, keepdims=True)
        qc = qc[0] if n_seq == 1 else jnp.concatenate(qc, axis=0)
        qn = qn[0] if n_seq == 1 else jnp.concatenate(qn, axis=0)
        m_out[blk, head:head + 1, :] = m_new_seq
        yield
        s = s * w
        num = _dot(s.astype(BF16), vb)
        den = jnp.sum(s, axis=1, keepdims=True)
        for j in range(n_seq):
            seq = blk * n_seq + j
            vw_j = vw_t if n_seq == 1 else jnp.where(lane_mask(j), vw_t, 0.0)
            c_out[seq, head] = decay_seq[:, j:j + 1] * c_in[seq, head] + _dot(vw_j.astype(BF16), kb)
        yield
        num = num + inter_w * qc
        den = jnp.maximum(jnp.abs(den + inter_w * qn), jnp.exp(-m_t))
        hout = num / den
        y = _groupnorm(hout, g_ml_ref[:, head * HEAD_DIM:(head + 1) * HEAD_DIM]) * jax.nn.sigmoid(o)
        mix_ref[rows, RET_W + head * HEAD_DIM:RET_W + (head + 1) * HEAD_DIM] = y.astype(BF16)

    n_ret_groups = N_RET_HEADS // HEADS_PER_GROUP
    n_groups = n_ret_groups + N_ML_HEADS // HEADS_PER_GROUP

    def first_head_of(grp):
        return (grp if grp < n_ret_groups else grp - n_ret_groups) * HEADS_PER_GROUP

    def project(grp, kind):
        is_ret = grp < n_ret_groups
        c0 = (0 if is_ret else 4 * RET_W) + kind * (RET_W if is_ret else ML_W) + first_head_of(grp) * HEAD_DIM
        proj_ref[grp % 2, :, kind * GROUP_W:(kind + 1) * GROUP_W] = _dot(h_ref[...], w_in_ref[:, c0:c0 + GROUP_W])

    def out_project(grp):
        cols = slice(grp * GROUP_W, (grp + 1) * GROUP_W)
        part = _dot(mix_ref[:, cols], w_out_ref[cols, :])
        if grp == 0:
            mixed_ref[...] = part
        else:
            mixed_ref[...] += part

    for kind in range(4):
        project(0, kind)
    for grp in range(n_groups):
        head_fn = ret_head if grp < n_ret_groups else ml_head
        heads = [head_fn(proj_ref.at[grp % 2], blk, first_head_of(grp) + hh, hh * HEAD_DIM)
                 for blk in range(n_blk) for hh in range(HEADS_PER_GROUP)]
        for phase in range(3):
            if phase == 2:
                if grp + 1 < n_groups:
                    for kind in range(4):
                        project(grp + 1, kind)
                if grp >= 1:
                    out_project(grp - 1)
            for head in heads:
                next(head, None)
    out_project(n_groups - 1)

    for blk in range(n_blk):
        rows = pl.ds(blk * BLK, BLK)
        y_ref[blk] = x_ref[blk] + _rmsnorm(mixed_ref[rows, :], g_post_ref[...])


def _ffn_kernel(*refs, seg, n_blk, fresh_state):
    n_seq_tile = n_blk * BLK // seg
    rows_tile = n_blk * BLK
    if fresh_state:
        (x_ref, g_pre_ref, w_up_ref, conv_w_ref, conv_b_ref, w_down_ref, g_post_ref,
         y_ref, buf_out, h_ref, act_ref, ffn_ref) = refs
        buf_in = buf_out

        @pl.when(pl.program_id(1) == 0)
        def _():
            buf_out[...] = jnp.zeros(buf_out.shape, F32)
    else:
        (x_ref, g_pre_ref, w_up_ref, conv_w_ref, conv_b_ref, w_down_ref, g_post_ref, buf_in,
         y_ref, buf_out, h_ref, act_ref, ffn_ref) = refs

    for blk in range(n_blk):
        rows = pl.ds(blk * BLK, BLK)
        h_ref[rows, :] = _rmsnorm(x_ref[blk], g_pre_ref[...]).astype(BF16)

    tpos = lax.broadcasted_iota(jnp.int32, (n_seq_tile, seg, FF_BLK), 1)

    def conv(cols):
        up = _dot(h_ref[...], w_up_ref[:, cols])
        prev0 = buf_in[:, 0:1, cols]
        prev1 = buf_in[:, 1:2, cols]
        up3 = up.reshape(n_seq_tile, seg, FF_BLK)
        sh1 = pltpu.roll(up, 1, axis=0).reshape(n_seq_tile, seg, FF_BLK)
        sh2 = pltpu.roll(up, 2, axis=0).reshape(n_seq_tile, seg, FF_BLK)
        sh1 = jnp.where(tpos == 0, prev1, sh1)
        sh2 = jnp.where(tpos == 0, prev0, jnp.where(tpos == 1, prev1, sh2))
        buf_out[:, 0:1, cols] = up3[:, seg - 2:seg - 1, :]
        buf_out[:, 1:2, cols] = up3[:, seg - 1:seg, :]
        w = conv_w_ref[:, cols]
        out = sh2 * w[0:1, :] + sh1 * w[1:2, :] + up3 * w[2:3, :] + conv_b_ref[:, cols]
        return out.reshape(rows_tile, FF_BLK)

    def activation(gate, val):
        c0 = math.sqrt(2.0 / math.pi)
        c1 = 0.044715 * c0
        t = jnp.tanh(gate * (gate * gate * c1 + c0))
        half = 0.5 * gate
        return ((half + half * t) * val).astype(BF16)

    for j in range(D_FF // FF_BLK):
        cols = slice(j * FF_BLK, (j + 1) * FF_BLK)
        gate = conv(cols)
        val = conv(slice(D_FF + j * FF_BLK, D_FF + (j + 1) * FF_BLK))
        act_ref[:, cols] = activation(gate, val)

    ffn_ref[...] = _dot(act_ref[...], w_down_ref[...])
    for blk in range(n_blk):
        rows = pl.ds(blk * BLK, BLK)
        y_ref[blk] = x_ref[blk] + _rmsnorm(ffn_ref[rows, :], g_post_ref[...])


def _const_spec(shape):
    zeros = (0,) * len(shape)
    return pl.BlockSpec(shape, lambda *_: zeros, pipeline_mode=pl.Buffered(1))


def _compiler_params(n_grid_dims):
    return pltpu.CompilerParams(
        dimension_semantics=("arbitrary",) * n_grid_dims,
        vmem_limit_bytes=VMEM_LIMIT_BYTES)


def _mixer_scratch(n_blk):
    rows = n_blk * BLK
    return [
        pltpu.VMEM((rows, D_MODEL), BF16),
        pltpu.VMEM((2, rows, 4 * GROUP_W), F32),
        pltpu.VMEM((rows, RET_W + ML_W), BF16),
        pltpu.VMEM((rows, D_MODEL), F32),
        pltpu.VMEM((rows, BLK), F32),
        pltpu.VMEM((rows, BLK), F32),
        pltpu.VMEM((n_blk, BLK, BLK), F32),
        pltpu.VMEM((n_blk, BLK, BLK), F32),
    ]


def _mixer_weight_specs():
    return [
        _const_spec((1, D_MODEL)),
        _const_spec((D_MODEL, 4 * RET_W + 4 * ML_W)),
        _const_spec((D_MODEL, BLK)),
        _const_spec((1, BLK)),
        _const_spec((1, RET_W)),
        _const_spec((1, ML_W)),
        _const_spec((RET_W + ML_W, D_MODEL)),
        _const_spec((1, D_MODEL)),
    ]


def _state_shapes(n_seqs, n_blocks):
    return [
        jax.ShapeDtypeStruct((n_seqs, N_RET_HEADS, HEAD_DIM, HEAD_DIM), F32),
        jax.ShapeDtypeStruct((n_seqs, N_ML_HEADS, HEAD_DIM, HEAD_DIM), F32),
        jax.ShapeDtypeStruct((n_seqs, N_ML_HEADS, HEAD_DIM), F32),
        jax.ShapeDtypeStruct((n_blocks, N_ML_HEADS, BLK), F32),
    ]


def _state_specs(n_seq_tile, n_blk, index):
    return [
        pl.BlockSpec((n_seq_tile, N_RET_HEADS, HEAD_DIM, HEAD_DIM), lambda *g: (index(*g), 0, 0, 0)),
        pl.BlockSpec((n_seq_tile, N_ML_HEADS, HEAD_DIM, HEAD_DIM), lambda *g: (index(*g), 0, 0, 0)),
        pl.BlockSpec((n_seq_tile, N_ML_HEADS, HEAD_DIM), lambda *g: (index(*g), 0, 0)),
        pl.BlockSpec((n_blk, N_ML_HEADS, BLK), lambda *g: (index(*g), 0, 0)),
    ]


def _prompt_mixer(x, cos, sin, weights, n_blk):
    batch, seq_len, _ = x.shape
    grid = (batch // n_blk, seq_len // BLK)
    x_spec = pl.BlockSpec((n_blk, BLK, D_MODEL), lambda g, c: (g, c, 0))
    rope_spec = pl.BlockSpec((BLK, HEAD_DIM), lambda g, c: (c, 0))
    return pl.pallas_call(
        functools.partial(_mixer_kernel, seg=BLK, n_blk=n_blk, fresh_state=True),
        grid=grid,
        in_specs=[x_spec, rope_spec, rope_spec] + _mixer_weight_specs(),
        out_specs=[x_spec] + _state_specs(n_blk, n_blk, lambda g, c: g),
        out_shape=[jax.ShapeDtypeStruct(x.shape, F32)] + _state_shapes(batch, batch),
        scratch_shapes=_mixer_scratch(n_blk),
        compiler_params=_compiler_params(2),
        name="prompt_mixer",
    )(x, cos, sin, *weights)


def _sample_mixer(x, cos, sin, weights, states, seg, n_blk):
    n_blocks = x.shape[0]
    n_seq_tile = n_blk * BLK // seg
    grid = (n_blocks // n_blk,)
    x_spec = pl.BlockSpec((n_blk, BLK, D_MODEL), lambda g: (g, 0, 0))
    rope_spec = _const_spec((BLK, HEAD_DIM))
    state_specs = _state_specs(n_seq_tile, n_blk, lambda g: g)
    return pl.pallas_call(
        functools.partial(_mixer_kernel, seg=seg, n_blk=n_blk, fresh_state=False),
        grid=grid,
        in_specs=[x_spec, rope_spec, rope_spec] + _mixer_weight_specs() + state_specs,
        out_specs=[x_spec] + state_specs,
        out_shape=[jax.ShapeDtypeStruct(x.shape, F32)]
        + _state_shapes(n_blocks * BLK // seg, n_blocks),
        scratch_shapes=_mixer_scratch(n_blk),
        compiler_params=_compiler_params(1),
        name="sample_mixer",
    )(x, cos, sin, *weights, *states)


def _ffn_scratch(n_blk):
    rows = n_blk * BLK
    return [
        pltpu.VMEM((rows, D_MODEL), BF16),
        pltpu.VMEM((rows, D_FF), BF16),
        pltpu.VMEM((rows, D_MODEL), F32),
    ]


def _ffn_weight_specs():
    return [
        _const_spec((1, D_MODEL)),
        _const_spec((D_MODEL, 2 * D_FF)),
        _const_spec((CONV_W, 2 * D_FF)),
        _const_spec((1, 2 * D_FF)),
        _const_spec((D_FF, D_MODEL)),
        _const_spec((1, D_MODEL)),
    ]


def _prompt_ffn(x, weights, n_blk):
    batch, seq_len, _ = x.shape
    grid = (batch // n_blk, seq_len // BLK)
    x_spec = pl.BlockSpec((n_blk, BLK, D_MODEL), lambda g, c: (g, c, 0))
    buf_spec = pl.BlockSpec((n_blk, CONV_W - 1, 2 * D_FF), lambda g, c: (g, 0, 0))
    return pl.pallas_call(
        functools.partial(_ffn_kernel, seg=BLK, n_blk=n_blk, fresh_state=True),
        grid=grid,
        in_specs=[x_spec] + _ffn_weight_specs(),
        out_specs=[x_spec, buf_spec],
        out_shape=[jax.ShapeDtypeStruct(x.shape, F32),
                   jax.ShapeDtypeStruct((batch, CONV_W - 1, 2 * D_FF), F32)],
        scratch_shapes=_ffn_scratch(n_blk),
        compiler_params=_compiler_params(2),
        name="prompt_ffn",
    )(x, *weights)


def _sample_ffn(x, weights, conv_buf, seg, n_blk):
    n_blocks = x.shape[0]
    n_seq_tile = n_blk * BLK // seg
    grid = (n_blocks // n_blk,)
    x_spec = pl.BlockSpec((n_blk, BLK, D_MODEL), lambda g: (g, 0, 0))
    buf_spec = pl.BlockSpec((n_seq_tile, CONV_W - 1, 2 * D_FF), lambda g: (g, 0, 0))
    return pl.pallas_call(
        functools.partial(_ffn_kernel, seg=seg, n_blk=n_blk, fresh_state=False),
        grid=grid,
        in_specs=[x_spec] + _ffn_weight_specs() + [buf_spec],
        out_specs=[x_spec, buf_spec],
        out_shape=[jax.ShapeDtypeStruct(x.shape, F32),
                   jax.ShapeDtypeStruct(conv_buf.shape, F32)],
        scratch_shapes=_ffn_scratch(n_blk),
        compiler_params=_compiler_params(1),
        name="sample_ffn",
    )(x, *weights, conv_buf)


def _rope_tables(pos):
    freqs = ROPE_BASE ** (-jnp.arange(0, HEAD_DIM, 2, dtype=F32) / HEAD_DIM)
    ang = pos.astype(F32)[:, None] * freqs[None, :]
    cos, sin = jnp.cos(ang), jnp.sin(ang)
    return jnp.concatenate([cos, cos], axis=-1), jnp.concatenate([-sin, sin], axis=-1)


def kernel(x_prompt, x_sample, state_ret, state_mlstm_C, state_mlstm_n, state_mlstm_m, cache_ffn_conv, pre_mix_gain, w_in, b_gates, ret_head_gain, mlstm_head_gain, w_out, post_mix_gain, pre_ffn_gain, w_up, conv_w, conv_b, w_down, post_ffn_gain):
    depth = w_in.shape[0]
    assert depth == 1
    batch, seq_len, _ = x_prompt.shape
    dec_batch, dec_seq, _ = x_sample.shape
    assert seq_len % BLK == 0 and BLK % dec_seq == 0 and (dec_batch * dec_seq) % BLK == 0
    seqs_per_blk = BLK // dec_seq
    n_sample_blocks = dec_batch * dec_seq // BLK
    layer = 0

    n_gate = 2 * N_ML_HEADS
    w_gate = jnp.pad(w_in[layer][:, -n_gate:], ((0, 0), (0, BLK - n_gate))).astype(BF16)
    b_gate = jnp.pad(b_gates[layer], (0, BLK - n_gate)).reshape(1, BLK)
    mixer_weights = (
        pre_mix_gain[layer].reshape(1, D_MODEL), w_in[layer][:, :-n_gate].astype(BF16), w_gate, b_gate,
        ret_head_gain[layer].reshape(1, RET_W), mlstm_head_gain[layer].reshape(1, ML_W),
        w_out[layer].astype(BF16), post_mix_gain[layer].reshape(1, D_MODEL))
    ffn_weights = (
        pre_ffn_gain[layer].reshape(1, D_MODEL), w_up[layer].astype(BF16), conv_w[layer],
        conv_b[layer].reshape(1, 2 * D_FF), w_down[layer].astype(BF16),
        post_ffn_gain[layer].reshape(1, D_MODEL))

    cos_p, sin_p = _rope_tables(jnp.arange(seq_len, dtype=jnp.int32))
    x1_p, s_p, c_p, n_p, m_p = _prompt_mixer(x_prompt, cos_p, sin_p, mixer_weights, n_blk=4)
    y_p, buf_p = _prompt_ffn(x1_p, ffn_weights, n_blk=4)
    m_p = m_p[:, :, 0]

    pos_s = PAST_LEN + jnp.arange(dec_seq, dtype=jnp.int32)
    cos_s, sin_s = _rope_tables(jnp.tile(pos_s, seqs_per_blk))
    m_blocks = jnp.transpose(state_mlstm_m[layer].reshape(n_sample_blocks, seqs_per_blk, N_ML_HEADS), (0, 2, 1))
    m_blocks = jnp.pad(m_blocks, ((0, 0), (0, 0), (0, BLK - seqs_per_blk)))
    xs = x_sample.reshape(n_sample_blocks, BLK, D_MODEL)
    x1_s, s_s, c_s, n_s, m_s = _sample_mixer(
        xs, cos_s, sin_s, mixer_weights,
        (state_ret[layer], state_mlstm_C[layer], state_mlstm_n[layer], m_blocks),
        seg=dec_seq, n_blk=1)
    y_s, buf_s = _sample_ffn(x1_s, ffn_weights, cache_ffn_conv[layer], seg=dec_seq, n_blk=1)
    y_s = y_s.reshape(x_sample.shape)
    m_s = jnp.transpose(m_s[:, :, :seqs_per_blk], (0, 2, 1)).reshape(dec_batch, N_ML_HEADS)

    return (y_p, y_s, s_p[None], s_s[None], c_p[None], c_s[None], n_p[None], n_s[None],
            m_p[None], m_s[None], buf_p[None], buf_s[None])
```

```python
import functools
import math

import jax
import jax.numpy as jnp
from jax import lax
from jax.experimental import pallas as pl
from jax.experimental.pallas import tpu as pltpu

D_MODEL = 1024
HEAD_DIM = 128
N_RET_HEADS = 4
N_ML_HEADS = 4
RET_W = N_RET_HEADS * HEAD_DIM
ML_W = N_ML_HEADS * HEAD_DIM
D_FF = 2816
CONV_W = 3
PAST_LEN = 16384
ROPE_BASE = 10000.0
EPS = 1e-6
M_INIT = -1e30

BLK = 128
HEADS_PER_GROUP = 2
GROUP_W = HEADS_PER_GROUP * HEAD_DIM
FF_BLK = 256
SUBLANES = 8
VMEM_LIMIT_BYTES = 56 * 1024 * 1024

F32 = jnp.float32
BF16 = jnp.bfloat16


def _dot(a, b, precision=None):
    return jnp.dot(a, b, preferred_element_type=F32, precision=precision)


def _dot_nt(a, b):
    return lax.dot_general(a, b, (((1,), (1,)), ((), ())), preferred_element_type=F32)


def _rmsnorm(x, g):
    return x * lax.rsqrt(jnp.mean(x * x, axis=-1, keepdims=True) + EPS) * g


def _groupnorm(h, g):
    mu = jnp.mean(h, axis=-1, keepdims=True)
    d = h - mu
    var = jnp.mean(d * d, axis=-1, keepdims=True)
    return d * lax.rsqrt(var + EPS) * g


def _ret_log_gamma(h):
    return math.log(1.0 - 2.0 ** (-5.0 - h))


class _Masks:
    def __init__(self, seg):
        shift = seg.bit_length() - 1
        r = lax.broadcasted_iota(jnp.int32, (BLK, BLK), 0)
        c = lax.broadcasted_iota(jnp.int32, (BLK, BLK), 1)
        rseq = r >> shift
        self.causal = (rseq == (c >> shift)) & (r >= c)
        self.tpos = (r & (seg - 1)).astype(F32)
        self.diff = (r - c).astype(F32)
        self.col_is_last_of_rowseq = c == (rseq * seg + (seg - 1))
        self.col_is_rowseq = c == rseq
        self.row_is_last_of_colseq = r == (c * seg + (seg - 1))


def _rows_from_seq(mask_col_is_rowseq, seq_row):
    return jnp.sum(jnp.where(mask_col_is_rowseq, seq_row, 0.0), axis=1, keepdims=True)


def _seq_from_last_row(mask_row_is_last_of_colseq, col):
    return jnp.sum(jnp.where(mask_row_is_last_of_colseq, col, 0.0), axis=0, keepdims=True)


def _mixer_kernel(*refs, seg, n_blk, fresh_state):
    n_seq = BLK // seg
    assert fresh_state == (n_seq == 1)
    if fresh_state:
        (x_ref, cos_ref, sin_ref, g_pre_ref, w_in_ref, w_gate_ref, b_gate_ref, g_ret_ref, g_ml_ref,
         w_out_ref, g_post_ref,
         y_ref, s_out, c_out, n_out, m_out,
         h_ref, proj_ref, mix_ref, mixed_ref, gate_ref, bcum_ref, gate_t_ref, bcum_t_ref, st_ref) = refs
        c_in, n_in, m_in = c_out, n_out, m_out

        @pl.when(pl.program_id(1) == 0)
        def _():
            st_ref[...] = jnp.zeros(st_ref.shape, F32)
            c_out[...] = jnp.zeros(c_out.shape, F32)
            n_out[...] = jnp.zeros(n_out.shape, F32)
            m_out[...] = jnp.full(m_out.shape, M_INIT, F32)
    else:
        (x_ref, cos_ref, sin_ref, g_pre_ref, w_in_ref, w_gate_ref, b_gate_ref, g_ret_ref, g_ml_ref,
         w_out_ref, g_post_ref, s_in, c_in, n_in, m_in,
         y_ref, s_out, c_out, n_out, m_out,
         h_ref, proj_ref, mix_ref, mixed_ref, gate_ref, bcum_ref, gate_t_ref, bcum_t_ref) = refs

    masks = _Masks(seg)
    scale = HEAD_DIM ** -0.5
    tri = jnp.where(masks.causal, 1.0, 0.0).astype(F32)
    cos = cos_ref[...]
    sin = sin_ref[...]

    for blk in range(n_blk):
        rows = pl.ds(blk * BLK, BLK)
        h_ref[rows, :] = _rmsnorm(x_ref[blk], g_pre_ref[...]).astype(BF16)
    gate_ref[...] = _dot(h_ref[...], w_gate_ref[...]) + b_gate_ref[...]
    for blk in range(n_blk):
        rows = pl.ds(blk * BLK, BLK)
        gates = gate_ref[rows, :]
        bcum = _dot(tri, jax.nn.log_sigmoid(gates), precision=lax.Precision.HIGHEST)
        bcum_ref[rows, :] = bcum
        gate_t_ref[blk] = gates.T
        bcum_t_ref[blk] = bcum.T

    def rotary(t):
        return t * cos + pltpu.roll(t, HEAD_DIM // 2, axis=1) * sin

    def seq_rows(j):
        return slice(j * seg, (j + 1) * seg)

    def lane_mask(j):
        c = lax.broadcasted_iota(jnp.int32, (BLK, BLK), 1)
        return (c >= j * seg) & (c < (j + 1) * seg)

    def ret_head(proj, blk, head, col):
        rows = pl.ds(blk * BLK, BLK)
        lg = _ret_log_gamma(head)
        q = rotary(proj[rows,col:col + HEAD_DIM])
        k = rotary(proj[rows,GROUP_W + col:GROUP_W + col + HEAD_DIM]) * scale
        v = proj[rows,2 * GROUP_W + col:2 * GROUP_W + col + HEAD_DIM]
        g = proj[rows,3 * GROUP_W + col:3 * GROUP_W + col + HEAD_DIM]
        qb, kb, vb = q.astype(BF16), k.astype(BF16), v.astype(BF16)
        dmat = jnp.where(masks.causal, jnp.exp(jnp.where(masks.causal, masks.diff * lg, 0.0)), 0.0)
        xi = jnp.exp((masks.tpos + 1.0) * lg)
        zeta = jnp.exp((float(seg - 1) - masks.tpos) * lg)
        chunk_decay = math.exp(seg * lg)
        scores = _dot_nt(qb, kb)
        kz_t = (k * zeta).T
        inter = []
        for j in range(n_seq):
            seq = blk * n_seq + j
            s_prev = s_in[seq, head]
            inter.append(_dot(q[seq_rows(j)].astype(BF16), s_prev.astype(BF16)))
            kz_j = jnp.where(lane_mask(j), kz_t, 0.0)
            s_out[seq, head] = s_prev * chunk_decay + _dot(kz_j.astype(BF16), vb)
        inter = jnp.concatenate(inter, axis=0)
        yield
        out = _dot((scores * dmat).astype(BF16), vb)
        yield
        out = out + inter * xi
        y = _groupnorm(out, g_ret_ref[:, head * HEAD_DIM:(head + 1) * HEAD_DIM]) * (g * jax.nn.sigmoid(g))
        mix_ref[rows, head * HEAD_DIM:(head + 1) * HEAD_DIM] = y.astype(BF16)

    def ml_head(proj, blk, head, col):
        rows = pl.ds(blk * BLK, BLK)
        q = proj[rows,col:col + HEAD_DIM]
        k = proj[rows,GROUP_W + col:GROUP_W + col + HEAD_DIM] * scale
        v = proj[rows,2 * GROUP_W + col:2 * GROUP_W + col + HEAD_DIM]
        o = proj[rows,3 * GROUP_W + col:3 * GROUP_W + col + HEAD_DIM]
        qb, kb, vb = q.astype(BF16), k.astype(BF16), v.astype(BF16)
        i_col = gate_ref[rows, head:head + 1]
        b_col = bcum_ref[rows, N_ML_HEADS + head:N_ML_HEADS + head + 1]
        i_row = gate_t_ref[blk, head:head + 1, :]
        b_row = bcum_t_ref[blk, N_ML_HEADS + head:N_ML_HEADS + head + 1, :]
        m_prev_seq = m_in[blk, head:head + 1, :]
        m_prev = _rows_from_seq(masks.col_is_rowseq, m_prev_seq)
        logw = jnp.where(masks.causal, b_col - b_row + i_row, -jnp.inf)
        m_t = jnp.maximum(b_col + m_prev, jnp.max(logw, axis=1, keepdims=True))
        w = jnp.exp(logw - m_t)
        inter_w = jnp.exp(b_col + m_prev - m_t)
        s = _dot_nt(qb, kb)
        m_new_seq = _seq_from_last_row(masks.row_is_last_of_colseq, m_t)
        b_last_seq = _seq_from_last_row(masks.row_is_last_of_colseq, b_col)
        decay_seq = jnp.exp(b_last_seq + m_prev_seq - m_new_seq)
        m_new = _rows_from_seq(masks.col_is_rowseq, m_new_seq)
        b_last = jnp.sum(jnp.where(masks.col_is_last_of_rowseq, b_row, 0.0), axis=1, keepdims=True)
        wl = jnp.exp(b_last - b_col + i_col - m_new)
        vw_t = (v * wl).T
        kw = k * wl
        qc, qn = [], []
        for j in range(n_seq):
            seq = blk * n_seq + j
            c_prev = c_in[seq, head]
            n_prev = n_in[seq, head:head + 1, :]
            qc.append(_dot_nt(q[seq_rows(j)].astype(BF16), c_prev.astype(BF16)))
            qn.append(jnp.sum(q[seq_rows(j)] * n_prev, axis=1, keepdims=True))
            decay = decay_seq[:, j:j + 1]
            n_out[seq, head:head + 1, :] = decay * n_prev + jnp.sum(kw[seq_rows(j)], axis=0, keepdims=True)
        qc = jnp.concatenate(qc, axis=0)
        qn = jnp.concatenate(qn, axis=0)
        m_out[blk, head:head + 1, :] = m_new_seq
        yield
        s = s * w
        num = _dot(s.astype(BF16), vb)
        den = jnp.sum(s, axis=1, keepdims=True)
        for j in range(n_seq):
            seq = blk * n_seq + j
            vw_j = jnp.where(lane_mask(j), vw_t, 0.0)
            c_out[seq, head] = decay_seq[:, j:j + 1] * c_in[seq, head] + _dot(vw_j.astype(BF16), kb)
        yield
        num = num + inter_w * qc
        den = jnp.maximum(jnp.abs(den + inter_w * qn), jnp.exp(-m_t))
        hout = num / den
        y = _groupnorm(hout, g_ml_ref[:, head * HEAD_DIM:(head + 1) * HEAD_DIM]) * jax.nn.sigmoid(o)
        mix_ref[rows, RET_W + head * HEAD_DIM:RET_W + (head + 1) * HEAD_DIM] = y.astype(BF16)

    r_idx = lax.broadcasted_iota(jnp.int32, (BLK, BLK), 0)
    c_idx = lax.broadcasted_iota(jnp.int32, (BLK, BLK), 1)
    causal_t = r_idx <= c_idx
    lane_pos = lax.broadcasted_iota(jnp.int32, (1, BLK), 1).astype(F32)
    head_consts = {}

    def ret_consts(head):
        if head not in head_consts:
            lg = _ret_log_gamma(head)
            dmat_t = jnp.where(causal_t, jnp.exp(jnp.where(causal_t, (c_idx - r_idx).astype(F32) * lg, 0.0)), 0.0)
            xi_row = jnp.exp((lane_pos + 1.0) * lg)
            zeta = jnp.exp((float(BLK - 1) - r_idx.astype(F32)) * lg)
            head_consts[head] = (dmat_t, xi_row, zeta, math.exp(BLK * lg))
        return head_consts[head]

    def channel_norm_t(h_t):
        mu = jnp.mean(h_t, axis=0, keepdims=True)
        d = h_t - mu
        var = jnp.mean(d * d, axis=0, keepdims=True)
        return d * lax.rsqrt(var + EPS)

    def ret_head_single(proj, blk, head, col):
        rows = pl.ds(blk * BLK, BLK)
        dmat_t, xi_row, zeta, chunk_decay = ret_consts(head)
        q = rotary(proj[rows, col:col + HEAD_DIM])
        k = rotary(proj[rows, GROUP_W + col:GROUP_W + col + HEAD_DIM]) * scale
        v_t = proj[rows, 2 * GROUP_W + col:2 * GROUP_W + col + HEAD_DIM].T
        qb, kb, v_tb = q.astype(BF16), k.astype(BF16), v_t.astype(BF16)
        scores_t = _dot_nt(kb, qb)
        st_prev = st_ref[blk, head]
        inter_t = _dot_nt(st_prev.astype(BF16), qb)
        st_ref[blk, head] = st_prev * chunk_decay + _dot(v_tb, (k * zeta).astype(BF16))
        yield
        out_t = _dot(v_tb, (scores_t * dmat_t).astype(BF16))
        yield
        out_t = out_t + inter_t * xi_row
        g = proj[rows, 3 * GROUP_W + col:3 * GROUP_W + col + HEAD_DIM]
        y = channel_norm_t(out_t).T * g_ret_ref[:, head * HEAD_DIM:(head + 1) * HEAD_DIM] * (g * jax.nn.sigmoid(g))
        mix_ref[rows, head * HEAD_DIM:(head + 1) * HEAD_DIM] = y.astype(BF16)

    def ml_head_single(proj, blk, head, col):
        rows = pl.ds(blk * BLK, BLK)
        q = proj[rows, col:col + HEAD_DIM]
        k = proj[rows, GROUP_W + col:GROUP_W + col + HEAD_DIM] * scale
        v_t = proj[rows, 2 * GROUP_W + col:2 * GROUP_W + col + HEAD_DIM].T
        qb, kb = q.astype(BF16), k.astype(BF16)
        s_t = _dot_nt(kb, qb)
        c_prev = c_in[blk, head]
        n_prev = n_in[blk, head:head + 1, :]
        qc_t = _dot_nt(c_prev.astype(BF16), qb)
        qn = _dot_nt(jnp.broadcast_to(n_prev, (2 * SUBLANES, HEAD_DIM)).astype(BF16), qb)[0:1, :]
        i_row = gate_t_ref[blk, head:head + 1, :]
        b_row = bcum_t_ref[blk, N_ML_HEADS + head:N_ML_HEADS + head + 1, :]
        key_term = gate_ref[rows, head:head + 1] - bcum_ref[rows, N_ML_HEADS + head:N_ML_HEADS + head + 1]
        m_prev = m_in[blk, head:head + 1, 0:1]
        logw_t = jnp.where(causal_t, b_row + key_term, -jnp.inf)
        m_t = jnp.maximum(b_row + m_prev, jnp.max(logw_t, axis=0, keepdims=True))
        w_t = jnp.exp(logw_t - m_t)
        inter_w = jnp.exp(b_row + m_prev - m_t)
        m_new = m_t[:, BLK - 1:BLK]
        b_last = b_row[:, BLK - 1:BLK]
        decay = jnp.exp(b_last + m_prev - m_new)
        wl = jnp.exp(b_last - b_row + i_row - m_new)
        m_out[blk, head:head + 1, :] = jnp.broadcast_to(m_new, (1, BLK))
        wk = _dot(jnp.broadcast_to(wl, (2 * SUBLANES, BLK)).astype(BF16), kb)[0:1, :]
        n_out[blk, head:head + 1, :] = decay * n_prev + wk
        vw_tb = (v_t * wl).astype(BF16)
        v_tb = v_t.astype(BF16)
        yield
        s_t = s_t * w_t
        num_t = _dot(v_tb, s_t.astype(BF16))
        den = jnp.sum(s_t, axis=0, keepdims=True)
        c_out[blk, head] = decay * c_in[blk, head] + _dot(vw_tb, kb)
        yield
        num_t = num_t + inter_w * qc_t
        den = jnp.maximum(jnp.abs(den + inter_w * qn), jnp.exp(-m_t))
        o = proj[rows, 3 * GROUP_W + col:3 * GROUP_W + col + HEAD_DIM]
        y = (channel_norm_t(num_t / den).T * g_ml_ref[:, head * HEAD_DIM:(head + 1) * HEAD_DIM]
             * jax.nn.sigmoid(o))
        mix_ref[rows, RET_W + head * HEAD_DIM:RET_W + (head + 1) * HEAD_DIM] = y.astype(BF16)

    n_ret_groups = N_RET_HEADS // HEADS_PER_GROUP
    n_groups = n_ret_groups + N_ML_HEADS // HEADS_PER_GROUP

    def first_head_of(grp):
        return (grp if grp < n_ret_groups else grp - n_ret_groups) * HEADS_PER_GROUP

    def project(grp, kind):
        is_ret = grp < n_ret_groups
        c0 = (0 if is_ret else 4 * RET_W) + kind * (RET_W if is_ret else ML_W) + first_head_of(grp) * HEAD_DIM
        proj_ref[grp % 2, :, kind * GROUP_W:(kind + 1) * GROUP_W] = _dot(h_ref[...], w_in_ref[:, c0:c0 + GROUP_W])

    def out_project(grp):
        cols = slice(grp * GROUP_W, (grp + 1) * GROUP_W)
        part = _dot(mix_ref[:, cols], w_out_ref[cols, :])
        if grp == 0:
            mixed_ref[...] = part
        else:
            mixed_ref[...] += part

    for kind in range(4):
        project(0, kind)
    for grp in range(n_groups):
        if n_seq == 1:
            head_fn = ret_head_single if grp < n_ret_groups else ml_head_single
        else:
            head_fn = ret_head if grp < n_ret_groups else ml_head
        heads = [head_fn(proj_ref.at[grp % 2], blk, first_head_of(grp) + hh, hh * HEAD_DIM)
                 for blk in range(n_blk) for hh in range(HEADS_PER_GROUP)]
        for phase in range(3):
            if phase == 2:
                if grp + 1 < n_groups:
                    for kind in range(4):
                        project(grp + 1, kind)
                if grp >= 1:
                    out_project(grp - 1)
            for head in heads:
                next(head, None)
    out_project(n_groups - 1)

    if fresh_state:
        @pl.when(pl.program_id(1) == pl.num_programs(1) - 1)
        def _():
            for blk in range(n_blk):
                for head in range(N_RET_HEADS):
                    s_out[blk, head] = st_ref[blk, head].T

    for blk in range(n_blk):
        rows = pl.ds(blk * BLK, BLK)
        y_ref[blk] = x_ref[blk] + _rmsnorm(mixed_ref[rows, :], g_post_ref[...])


def _ffn_kernel(*refs, seg, n_blk, fresh_state):
    n_seq_tile = n_blk * BLK // seg
    rows_tile = n_blk * BLK
    if fresh_state:
        (x_ref, g_pre_ref, w_up_ref, conv_w_ref, conv_b_ref, w_down_ref, g_post_ref,
         y_ref, buf_out, h_ref, act_ref, ffn_ref) = refs
        buf_in = buf_out

        @pl.when(pl.program_id(1) == 0)
        def _():
            buf_out[...] = jnp.zeros(buf_out.shape, F32)
    else:
        (x_ref, g_pre_ref, w_up_ref, conv_w_ref, conv_b_ref, w_down_ref, g_post_ref, buf_in,
         y_ref, buf_out, h_ref, act_ref, ffn_ref) = refs

    for blk in range(n_blk):
        rows = pl.ds(blk * BLK, BLK)
        h_ref[rows, :] = _rmsnorm(x_ref[blk], g_pre_ref[...]).astype(BF16)

    tpos = lax.broadcasted_iota(jnp.int32, (n_seq_tile, seg, FF_BLK), 1)

    def conv(cols):
        up = _dot(h_ref[...], w_up_ref[:, cols])
        prev0 = buf_in[:, 0:1, cols]
        prev1 = buf_in[:, 1:2, cols]
        up3 = up.reshape(n_seq_tile, seg, FF_BLK)
        sh1 = pltpu.roll(up, 1, axis=0).reshape(n_seq_tile, seg, FF_BLK)
        sh2 = pltpu.roll(up, 2, axis=0).reshape(n_seq_tile, seg, FF_BLK)
        sh1 = jnp.where(tpos == 0, prev1, sh1)
        sh2 = jnp.where(tpos == 0, prev0, jnp.where(tpos == 1, prev1, sh2))
        buf_out[:, 0:1, cols] = up3[:, seg - 2:seg - 1, :]
        buf_out[:, 1:2, cols] = up3[:, seg - 1:seg, :]
        w = conv_w_ref[:, cols]
        out = sh2 * w[0:1, :] + sh1 * w[1:2, :] + up3 * w[2:3, :] + conv_b_ref[:, cols]
        return out.reshape(rows_tile, FF_BLK)

    def activation(gate, val):
        c0 = math.sqrt(2.0 / math.pi)
        c1 = 0.044715 * c0
        t = jnp.tanh(gate * (gate * gate * c1 + c0))
        half = 0.5 * gate
        return ((half + half * t) * val).astype(BF16)

    for j in range(D_FF // FF_BLK):
        cols = slice(j * FF_BLK, (j + 1) * FF_BLK)
        gate = conv(cols)
        val = conv(slice(D_FF + j * FF_BLK, D_FF + (j + 1) * FF_BLK))
        act_ref[:, cols] = activation(gate, val)

    ffn_ref[...] = _dot(act_ref[...], w_down_ref[...])
    for blk in range(n_blk):
        rows = pl.ds(blk * BLK, BLK)
        y_ref[blk] = x_ref[blk] + _rmsnorm(ffn_ref[rows, :], g_post_ref[...])


def _const_spec(shape):
    zeros = (0,) * len(shape)
    return pl.BlockSpec(shape, lambda *_: zeros, pipeline_mode=pl.Buffered(1))


def _compiler_params(n_grid_dims):
    return pltpu.CompilerParams(
        dimension_semantics=("arbitrary",) * n_grid_dims,
        vmem_limit_bytes=VMEM_LIMIT_BYTES)


def _mixer_scratch(n_blk, fresh_state):
    rows = n_blk * BLK
    carried = [pltpu.VMEM((n_blk, N_RET_HEADS, HEAD_DIM, HEAD_DIM), F32)] if fresh_state else []
    return [
        pltpu.VMEM((rows, D_MODEL), BF16),
        pltpu.VMEM((2, rows, 4 * GROUP_W), F32),
        pltpu.VMEM((rows, RET_W + ML_W), BF16),
        pltpu.VMEM((rows, D_MODEL), F32),
        pltpu.VMEM((rows, BLK), F32),
        pltpu.VMEM((rows, BLK), F32),
        pltpu.VMEM((n_blk, BLK, BLK), F32),
        pltpu.VMEM((n_blk, BLK, BLK), F32),
    ] + carried


def _mixer_weight_specs():
    return [
        _const_spec((1, D_MODEL)),
        _const_spec((D_MODEL, 4 * RET_W + 4 * ML_W)),
        _const_spec((D_MODEL, BLK)),
        _const_spec((1, BLK)),
        _const_spec((1, RET_W)),
        _const_spec((1, ML_W)),
        _const_spec((RET_W + ML_W, D_MODEL)),
        _const_spec((1, D_MODEL)),
    ]


def _state_shapes(n_seqs, n_blocks):
    return [
        jax.ShapeDtypeStruct((n_seqs, N_RET_HEADS, HEAD_DIM, HEAD_DIM), F32),
        jax.ShapeDtypeStruct((n_seqs, N_ML_HEADS, HEAD_DIM, HEAD_DIM), F32),
        jax.ShapeDtypeStruct((n_seqs, N_ML_HEADS, HEAD_DIM), F32),
        jax.ShapeDtypeStruct((n_blocks, N_ML_HEADS, BLK), F32),
    ]


def _state_specs(n_seq_tile, n_blk, index):
    return [
        pl.BlockSpec((n_seq_tile, N_RET_HEADS, HEAD_DIM, HEAD_DIM), lambda *g: (index(*g), 0, 0, 0)),
        pl.BlockSpec((n_seq_tile, N_ML_HEADS, HEAD_DIM, HEAD_DIM), lambda *g: (index(*g), 0, 0, 0)),
        pl.BlockSpec((n_seq_tile, N_ML_HEADS, HEAD_DIM), lambda *g: (index(*g), 0, 0)),
        pl.BlockSpec((n_blk, N_ML_HEADS, BLK), lambda *g: (index(*g), 0, 0)),
    ]


def _prompt_mixer(x, cos, sin, weights, n_blk):
    batch, seq_len, _ = x.shape
    grid = (batch // n_blk, seq_len // BLK)
    x_spec = pl.BlockSpec((n_blk, BLK, D_MODEL), lambda g, c: (g, c, 0))
    rope_spec = pl.BlockSpec((BLK, HEAD_DIM), lambda g, c: (c, 0))
    return pl.pallas_call(
        functools.partial(_mixer_kernel, seg=BLK, n_blk=n_blk, fresh_state=True),
        grid=grid,
        in_specs=[x_spec, rope_spec, rope_spec] + _mixer_weight_specs(),
        out_specs=[x_spec] + _state_specs(n_blk, n_blk, lambda g, c: g),
        out_shape=[jax.ShapeDtypeStruct(x.shape, F32)] + _state_shapes(batch, batch),
        scratch_shapes=_mixer_scratch(n_blk, fresh_state=True),
        compiler_params=_compiler_params(2),
        name="prompt_mixer",
    )(x, cos, sin, *weights)


def _sample_mixer(x, cos, sin, weights, states, seg, n_blk):
    n_blocks = x.shape[0]
    n_seq_tile = n_blk * BLK // seg
    grid = (n_blocks // n_blk,)
    x_spec = pl.BlockSpec((n_blk, BLK, D_MODEL), lambda g: (g, 0, 0))
    rope_spec = _const_spec((BLK, HEAD_DIM))
    state_specs = _state_specs(n_seq_tile, n_blk, lambda g: g)
    return pl.pallas_call(
        functools.partial(_mixer_kernel, seg=seg, n_blk=n_blk, fresh_state=False),
        grid=grid,
        in_specs=[x_spec, rope_spec, rope_spec] + _mixer_weight_specs() + state_specs,
        out_specs=[x_spec] + state_specs,
        out_shape=[jax.ShapeDtypeStruct(x.shape, F32)]
        + _state_shapes(n_blocks * BLK // seg, n_blocks),
        scratch_shapes=_mixer_scratch(n_blk, fresh_state=False),
        compiler_params=_compiler_params(1),
        name="sample_mixer",
    )(x, cos, sin, *weights, *states)


def _ffn_scratch(n_blk):
    rows = n_blk * BLK
    return [
        pltpu.VMEM((rows, D_MODEL), BF16),
        pltpu.VMEM((rows, D_FF), BF16),
        pltpu.VMEM((rows, D_MODEL), F32),
    ]


def _ffn_weight_specs():
    return [
        _const_spec((1, D_MODEL)),
        _const_spec((D_MODEL, 2 * D_FF)),
        _const_spec((CONV_W, 2 * D_FF)),
        _const_spec((1, 2 * D_FF)),
        _const_spec((D_FF, D_MODEL)),
        _const_spec((1, D_MODEL)),
    ]


def _prompt_ffn(x, weights, n_blk):
    batch, seq_len, _ = x.shape
    grid = (batch // n_blk, seq_len // BLK)
    x_spec = pl.BlockSpec((n_blk, BLK, D_MODEL), lambda g, c: (g, c, 0))
    buf_spec = pl.BlockSpec((n_blk, CONV_W - 1, 2 * D_FF), lambda g, c: (g, 0, 0))
    return pl.pallas_call(
        functools.partial(_ffn_kernel, seg=BLK, n_blk=n_blk, fresh_state=True),
        grid=grid,
        in_specs=[x_spec] + _ffn_weight_specs(),
        out_specs=[x_spec, buf_spec],
        out_shape=[jax.ShapeDtypeStruct(x.shape, F32),
                   jax.ShapeDtypeStruct((batch, CONV_W - 1, 2 * D_FF), F32)],
        scratch_shapes=_ffn_scratch(n_blk),
        compiler_params=_compiler_params(2),
        name="prompt_ffn",
    )(x, *weights)


def _sample_ffn(x, weights, conv_buf, seg, n_blk):
    n_blocks = x.shape[0]
    n_seq_tile = n_blk * BLK // seg
    grid = (n_blocks // n_blk,)
    x_spec = pl.BlockSpec((n_blk, BLK, D_MODEL), lambda g: (g, 0, 0))
    buf_spec = pl.BlockSpec((n_seq_tile, CONV_W - 1, 2 * D_FF), lambda g: (g, 0, 0))
    return pl.pallas_call(
        functools.partial(_ffn_kernel, seg=seg, n_blk=n_blk, fresh_state=False),
        grid=grid,
        in_specs=[x_spec] + _ffn_weight_specs() + [buf_spec],
        out_specs=[x_spec, buf_spec],
        out_shape=[jax.ShapeDtypeStruct(x.shape, F32),
                   jax.ShapeDtypeStruct(conv_buf.shape, F32)],
        scratch_shapes=_ffn_scratch(n_blk),
        compiler_params=_compiler_params(1),
        name="sample_ffn",
    )(x, *weights, conv_buf)


def _rope_tables(pos):
    freqs = ROPE_BASE ** (-jnp.arange(0, HEAD_DIM, 2, dtype=F32) / HEAD_DIM)
    ang = pos.astype(F32)[:, None] * freqs[None, :]
    cos, sin = jnp.cos(ang), jnp.sin(ang)
    return jnp.concatenate([cos, cos], axis=-1), jnp.concatenate([-sin, sin], axis=-1)


def kernel(x_prompt, x_sample, state_ret, state_mlstm_C, state_mlstm_n, state_mlstm_m, cache_ffn_conv, pre_mix_gain, w_in, b_gates, ret_head_gain, mlstm_head_gain, w_out, post_mix_gain, pre_ffn_gain, w_up, conv_w, conv_b, w_down, post_ffn_gain):
    depth = w_in.shape[0]
    assert depth == 1
    batch, seq_len, _ = x_prompt.shape
    dec_batch, dec_seq, _ = x_sample.shape
    assert seq_len % BLK == 0 and BLK % dec_seq == 0 and (dec_batch * dec_seq) % BLK == 0
    seqs_per_blk = BLK // dec_seq
    n_sample_blocks = dec_batch * dec_seq // BLK
    layer = 0

    n_gate = 2 * N_ML_HEADS
    w_gate = jnp.pad(w_in[layer][:, -n_gate:], ((0, 0), (0, BLK - n_gate))).astype(BF16)
    b_gate = jnp.pad(b_gates[layer], (0, BLK - n_gate)).reshape(1, BLK)
    mixer_weights = (
        pre_mix_gain[layer].reshape(1, D_MODEL), w_in[layer][:, :-n_gate].astype(BF16), w_gate, b_gate,
        ret_head_gain[layer].reshape(1, RET_W), mlstm_head_gain[layer].reshape(1, ML_W),
        w_out[layer].astype(BF16), post_mix_gain[layer].reshape(1, D_MODEL))
    ffn_weights = (
        pre_ffn_gain[layer].reshape(1, D_MODEL), w_up[layer].astype(BF16), conv_w[layer],
        conv_b[layer].reshape(1, 2 * D_FF), w_down[layer].astype(BF16),
        post_ffn_gain[layer].reshape(1, D_MODEL))

    cos_p, sin_p = _rope_tables(jnp.arange(seq_len, dtype=jnp.int32))
    x1_p, s_p, c_p, n_p, m_p = _prompt_mixer(x_prompt, cos_p, sin_p, mixer_weights, n_blk=4)
    y_p, buf_p = _prompt_ffn(x1_p, ffn_weights, n_blk=4)
    m_p = m_p[:, :, 0]

    pos_s = PAST_LEN + jnp.arange(dec_seq, dtype=jnp.int32)
    cos_s, sin_s = _rope_tables(jnp.tile(pos_s, seqs_per_blk))
    m_blocks = jnp.transpose(state_mlstm_m[layer].reshape(n_sample_blocks, seqs_per_blk, N_ML_HEADS), (0, 2, 1))
    m_blocks = jnp.pad(m_blocks, ((0, 0), (0, 0), (0, BLK - seqs_per_blk)))
    xs = x_sample.reshape(n_sample_blocks, BLK, D_MODEL)
    x1_s, s_s, c_s, n_s, m_s = _sample_mixer(
        xs, cos_s, sin_s, mixer_weights,
        (state_ret[layer], state_mlstm_C[layer], state_mlstm_n[layer], m_blocks),
        seg=dec_seq, n_blk=1)
    y_s, buf_s = _sample_ffn(x1_s, ffn_weights, cache_ffn_conv[layer], seg=dec_seq, n_blk=1)
    y_s = y_s.reshape(x_sample.shape)
    m_s = jnp.transpose(m_s[:, :, :seqs_per_blk], (0, 2, 1)).reshape(dec_batch, N_ML_HEADS)

    return (y_p, y_s, s_p[None], s_s[None], c_p[None], c_s[None], n_p[None], n_s[None],
            m_p[None], m_s[None], buf_p[None], buf_s[None])
```

```python
import functools
import math

import jax
import jax.numpy as jnp
from jax import lax
from jax.experimental import pallas as pl
from jax.experimental.pallas import tpu as pltpu

D_MODEL = 1024
HEAD_DIM = 128
N_RET_HEADS = 4
N_ML_HEADS = 4
RET_W = N_RET_HEADS * HEAD_DIM
ML_W = N_ML_HEADS * HEAD_DIM
D_FF = 2816
CONV_W = 3
PAST_LEN = 16384
ROPE_BASE = 10000.0
EPS = 1e-6
M_INIT = -1e30

BLK = 128
HEADS_PER_GROUP = 2
GROUP_W = HEADS_PER_GROUP * HEAD_DIM
FF_BLK = 256
SUBLANES = 8
VMEM_LIMIT_BYTES = 56 * 1024 * 1024

F32 = jnp.float32
BF16 = jnp.bfloat16


def _dot(a, b, precision=None):
    return jnp.dot(a, b, preferred_element_type=F32, precision=precision)


def _dot_nt(a, b):
    return lax.dot_general(a, b, (((1,), (1,)), ((), ())), preferred_element_type=F32)


def _rmsnorm(x, g):
    return x * lax.rsqrt(jnp.mean(x * x, axis=-1, keepdims=True) + EPS) * g


def _groupnorm(h, g):
    mu = jnp.mean(h, axis=-1, keepdims=True)
    d = h - mu
    var = jnp.mean(d * d, axis=-1, keepdims=True)
    return d * lax.rsqrt(var + EPS) * g


def _ret_log_gamma(h):
    return math.log(1.0 - 2.0 ** (-5.0 - h))


class _Masks:
    def __init__(self, seg):
        shift = seg.bit_length() - 1
        r = lax.broadcasted_iota(jnp.int32, (BLK, BLK), 0)
        c = lax.broadcasted_iota(jnp.int32, (BLK, BLK), 1)
        rseq = r >> shift
        self.causal = (rseq == (c >> shift)) & (r >= c)
        self.tpos = (r & (seg - 1)).astype(F32)
        self.diff = (r - c).astype(F32)
        self.col_is_last_of_rowseq = c == (rseq * seg + (seg - 1))
        self.col_is_rowseq = c == rseq
        self.row_is_last_of_colseq = r == (c * seg + (seg - 1))


def _rows_from_seq(mask_col_is_rowseq, seq_row):
    return jnp.sum(jnp.where(mask_col_is_rowseq, seq_row, 0.0), axis=1, keepdims=True)


def _seq_from_last_row(mask_row_is_last_of_colseq, col):
    return jnp.sum(jnp.where(mask_row_is_last_of_colseq, col, 0.0), axis=0, keepdims=True)


def _mixer_kernel(*refs, seg, n_blk, fresh_state):
    n_seq = BLK // seg
    assert fresh_state == (n_seq == 1)
    if fresh_state:
        (x_ref, cos_ref, sin_ref, g_pre_ref, w_in_ref, w_gate_ref, b_gate_ref, g_ret_ref, g_ml_ref,
         w_out_ref, g_post_ref,
         y_ref, s_out, c_out, n_out, m_out,
         h_ref, proj_ref, mix_ref, mixed_ref, gate_ref, bcum_ref, gate_t_ref, bcum_t_ref, st_ref) = refs
        c_in, n_in, m_in = c_out, n_out, m_out

        @pl.when(pl.program_id(1) == 0)
        def _():
            st_ref[...] = jnp.zeros(st_ref.shape, F32)
            c_out[...] = jnp.zeros(c_out.shape, F32)
            n_out[...] = jnp.zeros(n_out.shape, F32)
            m_out[...] = jnp.full(m_out.shape, M_INIT, F32)
    else:
        (x_ref, cos_ref, sin_ref, g_pre_ref, w_in_ref, w_gate_ref, b_gate_ref, g_ret_ref, g_ml_ref,
         w_out_ref, g_post_ref, s_in, c_in, n_in, m_in,
         y_ref, s_out, c_out, n_out, m_out,
         h_ref, proj_ref, mix_ref, mixed_ref, gate_ref, bcum_ref, gate_t_ref, bcum_t_ref) = refs

    masks = _Masks(seg)
    scale = HEAD_DIM ** -0.5
    tri = jnp.where(masks.causal, 1.0, 0.0).astype(F32)
    cos = cos_ref[...]
    sin = sin_ref[...]

    for blk in range(n_blk):
        rows = pl.ds(blk * BLK, BLK)
        h_ref[rows, :] = _rmsnorm(x_ref[blk], g_pre_ref[...]).astype(BF16)
    gate_ref[...] = _dot(h_ref[...], w_gate_ref[...]) + b_gate_ref[...]
    for blk in range(n_blk):
        rows = pl.ds(blk * BLK, BLK)
        gates = gate_ref[rows, :]
        bcum = _dot(tri, jax.nn.log_sigmoid(gates), precision=lax.Precision.HIGHEST)
        bcum_ref[rows, :] = bcum
        gate_t_ref[blk] = gates.T
        bcum_t_ref[blk] = bcum.T

    def rotary(t):
        return t * cos + pltpu.roll(t, HEAD_DIM // 2, axis=1) * sin

    def seq_rows(j):
        return slice(j * seg, (j + 1) * seg)

    def lane_mask(j):
        c = lax.broadcasted_iota(jnp.int32, (BLK, BLK), 1)
        return (c >= j * seg) & (c < (j + 1) * seg)

    def ret_head(proj, blk, head, col):
        rows = pl.ds(blk * BLK, BLK)
        lg = _ret_log_gamma(head)
        q = rotary(proj[rows,col:col + HEAD_DIM])
        k = rotary(proj[rows,GROUP_W + col:GROUP_W + col + HEAD_DIM]) * scale
        v = proj[rows,2 * GROUP_W + col:2 * GROUP_W + col + HEAD_DIM]
        g = proj[rows,3 * GROUP_W + col:3 * GROUP_W + col + HEAD_DIM]
        qb, kb, vb = q.astype(BF16), k.astype(BF16), v.astype(BF16)
        dmat = jnp.where(masks.causal, jnp.exp(jnp.where(masks.causal, masks.diff * lg, 0.0)), 0.0)
        xi = jnp.exp((masks.tpos + 1.0) * lg)
        zeta = jnp.exp((float(seg - 1) - masks.tpos) * lg)
        chunk_decay = math.exp(seg * lg)
        scores = _dot_nt(qb, kb)
        kz_t = (k * zeta).T
        inter = []
        for j in range(n_seq):
            seq = blk * n_seq + j
            s_prev = s_in[seq, head]
            inter.append(_dot(q[seq_rows(j)].astype(BF16), s_prev.astype(BF16)))
            kz_j = jnp.where(lane_mask(j), kz_t, 0.0)
            s_out[seq, head] = s_prev * chunk_decay + _dot(kz_j.astype(BF16), vb)
        inter = jnp.concatenate(inter, axis=0)
        yield
        out = _dot((scores * dmat).astype(BF16), vb)
        yield
        out = out + inter * xi
        y = _groupnorm(out, g_ret_ref[:, head * HEAD_DIM:(head + 1) * HEAD_DIM]) * (g * jax.nn.sigmoid(g))
        mix_ref[rows, head * HEAD_DIM:(head + 1) * HEAD_DIM] = y.astype(BF16)

    def ml_head(proj, blk, head, col):
        rows = pl.ds(blk * BLK, BLK)
        q = proj[rows,col:col + HEAD_DIM]
        k = proj[rows,GROUP_W + col:GROUP_W + col + HEAD_DIM] * scale
        v = proj[rows,2 * GROUP_W + col:2 * GROUP_W + col + HEAD_DIM]
        o = proj[rows,3 * GROUP_W + col:3 * GROUP_W + col + HEAD_DIM]
        qb, kb, vb = q.astype(BF16), k.astype(BF16), v.astype(BF16)
        i_col = gate_ref[rows, head:head + 1]
        b_col = bcum_ref[rows, N_ML_HEADS + head:N_ML_HEADS + head + 1]
        i_row = gate_t_ref[blk, head:head + 1, :]
        b_row = bcum_t_ref[blk, N_ML_HEADS + head:N_ML_HEADS + head + 1, :]
        m_prev_seq = m_in[blk, head:head + 1, :]
        m_prev = _rows_from_seq(masks.col_is_rowseq, m_prev_seq)
        logw = jnp.where(masks.causal, b_col - b_row + i_row, -jnp.inf)
        m_t = jnp.maximum(b_col + m_prev, jnp.max(logw, axis=1, keepdims=True))
        w = jnp.exp(logw - m_t)
        inter_w = jnp.exp(b_col + m_prev - m_t)
        s = _dot_nt(qb, kb)
        m_new_seq = _seq_from_last_row(masks.row_is_last_of_colseq, m_t)
        b_last_seq = _seq_from_last_row(masks.row_is_last_of_colseq, b_col)
        decay_seq = jnp.exp(b_last_seq + m_prev_seq - m_new_seq)
        m_new = _rows_from_seq(masks.col_is_rowseq, m_new_seq)
        b_last = jnp.sum(jnp.where(masks.col_is_last_of_rowseq, b_row, 0.0), axis=1, keepdims=True)
        wl = jnp.exp(b_last - b_col + i_col - m_new)
        vw_t = (v * wl).T
        kw = k * wl
        qc, qn = [], []
        for j in range(n_seq):
            seq = blk * n_seq + j
            c_prev = c_in[seq, head]
            n_prev = n_in[seq, head:head + 1, :]
            qc.append(_dot_nt(q[seq_rows(j)].astype(BF16), c_prev.astype(BF16)))
            qn.append(jnp.sum(q[seq_rows(j)] * n_prev, axis=1, keepdims=True))
            decay = decay_seq[:, j:j + 1]
            n_out[seq, head:head + 1, :] = decay * n_prev + jnp.sum(kw[seq_rows(j)], axis=0, keepdims=True)
        qc = jnp.concatenate(qc, axis=0)
        qn = jnp.concatenate(qn, axis=0)
        m_out[blk, head:head + 1, :] = m_new_seq
        yield
        s = s * w
        num = _dot(s.astype(BF16), vb)
        den = jnp.sum(s, axis=1, keepdims=True)
        for j in range(n_seq):
            seq = blk * n_seq + j
            vw_j = jnp.where(lane_mask(j), vw_t, 0.0)
            c_out[seq, head] = decay_seq[:, j:j + 1] * c_in[seq, head] + _dot(vw_j.astype(BF16), kb)
        yield
        num = num + inter_w * qc
        den = jnp.maximum(jnp.abs(den + inter_w * qn), jnp.exp(-m_t))
        hout = num / den
        y = _groupnorm(hout, g_ml_ref[:, head * HEAD_DIM:(head + 1) * HEAD_DIM]) * jax.nn.sigmoid(o)
        mix_ref[rows, RET_W + head * HEAD_DIM:RET_W + (head + 1) * HEAD_DIM] = y.astype(BF16)

    r_idx = lax.broadcasted_iota(jnp.int32, (BLK, BLK), 0)
    c_idx = lax.broadcasted_iota(jnp.int32, (BLK, BLK), 1)
    causal_t = r_idx <= c_idx
    lane_pos = lax.broadcasted_iota(jnp.int32, (1, BLK), 1).astype(F32)
    head_consts = {}

    def ret_consts(head):
        if head not in head_consts:
            lg = _ret_log_gamma(head)
            dmat_t = jnp.where(causal_t, jnp.exp(jnp.where(causal_t, (c_idx - r_idx).astype(F32) * lg, 0.0)), 0.0)
            xi_row = jnp.exp((lane_pos + 1.0) * lg)
            zeta = jnp.exp((float(BLK - 1) - r_idx.astype(F32)) * lg)
            head_consts[head] = (dmat_t, xi_row, zeta, math.exp(BLK * lg))
        return head_consts[head]

    def channel_norm_t(h_t):
        mu = jnp.mean(h_t, axis=0, keepdims=True)
        d = h_t - mu
        var = jnp.mean(d * d, axis=0, keepdims=True)
        return d * lax.rsqrt(var + EPS)

    def ret_head_single(proj, blk, head, col):
        rows = pl.ds(blk * BLK, BLK)
        dmat_t, xi_row, zeta, chunk_decay = ret_consts(head)
        q = rotary(proj[rows, col:col + HEAD_DIM])
        k = rotary(proj[rows, GROUP_W + col:GROUP_W + col + HEAD_DIM]) * scale
        v_t = proj[rows, 2 * GROUP_W + col:2 * GROUP_W + col + HEAD_DIM].T
        qb, kb, v_tb = q.astype(BF16), k.astype(BF16), v_t.astype(BF16)
        scores_t = _dot_nt(kb, qb)
        st_prev = st_ref[blk, head]
        inter_t = _dot_nt(st_prev.astype(BF16), qb)
        st_ref[blk, head] = st_prev * chunk_decay + _dot(v_tb, (k * zeta).astype(BF16))
        yield
        out_t = _dot(v_tb, (scores_t * dmat_t).astype(BF16))
        yield
        out_t = out_t + inter_t * xi_row
        g = proj[rows, 3 * GROUP_W + col:3 * GROUP_W + col + HEAD_DIM]
        y = channel_norm_t(out_t).T * g_ret_ref[:, head * HEAD_DIM:(head + 1) * HEAD_DIM] * (g * jax.nn.sigmoid(g))
        mix_ref[rows, head * HEAD_DIM:(head + 1) * HEAD_DIM] = y.astype(BF16)

    def ml_head_single(proj, blk, head, col):
        rows = pl.ds(blk * BLK, BLK)
        q = proj[rows, col:col + HEAD_DIM]
        k = proj[rows, GROUP_W + col:GROUP_W + col + HEAD_DIM] * scale
        v_t = proj[rows, 2 * GROUP_W + col:2 * GROUP_W + col + HEAD_DIM].T
        qb, kb = q.astype(BF16), k.astype(BF16)
        s_t = _dot_nt(kb, qb)
        c_prev = c_in[blk, head]
        n_prev = n_in[blk, head:head + 1, :]
        qc_t = _dot_nt(c_prev.astype(BF16), qb)
        qn = _dot_nt(jnp.broadcast_to(n_prev, (2 * SUBLANES, HEAD_DIM)).astype(BF16), qb)[0:1, :]
        i_row = gate_t_ref[blk, head:head + 1, :]
        b_row = bcum_t_ref[blk, N_ML_HEADS + head:N_ML_HEADS + head + 1, :]
        key_term = gate_ref[rows, head:head + 1] - bcum_ref[rows, N_ML_HEADS + head:N_ML_HEADS + head + 1]
        m_prev = m_in[blk, head:head + 1, 0:1]
        logw_t = jnp.where(causal_t, b_row + key_term, -jnp.inf)
        m_t = jnp.maximum(b_row + m_prev, jnp.max(logw_t, axis=0, keepdims=True))
        w_t = jnp.exp(logw_t - m_t)
        inter_w = jnp.exp(b_row + m_prev - m_t)
        m_new = m_t[:, BLK - 1:BLK]
        b_last = b_row[:, BLK - 1:BLK]
        decay = jnp.exp(b_last + m_prev - m_new)
        wl = jnp.exp(b_last - b_row + i_row - m_new)
        m_out[blk, head:head + 1, :] = jnp.broadcast_to(m_new, (1, BLK))
        wk = _dot(jnp.broadcast_to(wl, (2 * SUBLANES, BLK)).astype(BF16), kb)[0:1, :]
        n_out[blk, head:head + 1, :] = decay * n_prev + wk
        vw_tb = (v_t * wl).astype(BF16)
        v_tb = v_t.astype(BF16)
        yield
        s_t = s_t * w_t
        num_t = _dot(v_tb, s_t.astype(BF16))
        den = jnp.sum(s_t, axis=0, keepdims=True)
        c_out[blk, head] = decay * c_in[blk, head] + _dot(vw_tb, kb)
        yield
        num_t = num_t + inter_w * qc_t
        den = jnp.maximum(jnp.abs(den + inter_w * qn), jnp.exp(-m_t))
        o = proj[rows, 3 * GROUP_W + col:3 * GROUP_W + col + HEAD_DIM]
        y = (channel_norm_t(num_t / den).T * g_ml_ref[:, head * HEAD_DIM:(head + 1) * HEAD_DIM]
             * jax.nn.sigmoid(o))
        mix_ref[rows, RET_W + head * HEAD_DIM:RET_W + (head + 1) * HEAD_DIM] = y.astype(BF16)

    n_ret_groups = N_RET_HEADS // HEADS_PER_GROUP
    n_groups = n_ret_groups + N_ML_HEADS // HEADS_PER_GROUP

    def first_head_of(grp):
        return (grp if grp < n_ret_groups else grp - n_ret_groups) * HEADS_PER_GROUP

    def project(grp, kind):
        is_ret = grp < n_ret_groups
        c0 = (0 if is_ret else 4 * RET_W) + kind * (RET_W if is_ret else ML_W) + first_head_of(grp) * HEAD_DIM
        proj_ref[grp % 2, :, kind * GROUP_W:(kind + 1) * GROUP_W] = _dot(h_ref[...], w_in_ref[:, c0:c0 + GROUP_W])

    def out_project(grp):
        cols = slice(grp * GROUP_W, (grp + 1) * GROUP_W)
        part = _dot(mix_ref[:, cols], w_out_ref[cols, :])
        if grp == 0:
            mixed_ref[...] = part
        else:
            mixed_ref[...] += part

    for kind in range(4):
        project(0, kind)
    for grp in range(n_groups):
        if n_seq == 1:
            head_fn = ret_head_single if grp < n_ret_groups else ml_head_single
        else:
            head_fn = ret_head if grp < n_ret_groups else ml_head
        heads = [head_fn(proj_ref.at[grp % 2], blk, first_head_of(grp) + hh, hh * HEAD_DIM)
                 for blk in range(n_blk) for hh in range(HEADS_PER_GROUP)]
        for phase in range(3):
            if phase == 2:
                if grp + 1 < n_groups:
                    for kind in range(4):
                        project(grp + 1, kind)
                if grp >= 1:
                    out_project(grp - 1)
            for head in heads:
                next(head, None)
    out_project(n_groups - 1)

    if fresh_state:
        @pl.when(pl.program_id(1) == pl.num_programs(1) - 1)
        def _():
            for blk in range(n_blk):
                for head in range(N_RET_HEADS):
                    s_out[blk, head] = st_ref[blk, head].T

    for blk in range(n_blk):
        rows = pl.ds(blk * BLK, BLK)
        y_ref[blk] = x_ref[blk] + _rmsnorm(mixed_ref[rows, :], g_post_ref[...])


def _ffn_kernel(*refs, seg, n_blk, fresh_state):
    n_seq_tile = n_blk * BLK // seg
    rows_tile = n_blk * BLK
    if fresh_state:
        (x_ref, g_pre_ref, w_up_ref, conv_w_ref, conv_b_ref, w_down_ref, g_post_ref,
         y_ref, buf_out, h_ref, act_ref, ffn_ref) = refs
        buf_in = buf_out

        @pl.when(pl.program_id(1) == 0)
        def _():
            buf_out[...] = jnp.zeros(buf_out.shape, F32)
    else:
        (x_ref, g_pre_ref, w_up_ref, conv_w_ref, conv_b_ref, w_down_ref, g_post_ref, buf_in,
         y_ref, buf_out, h_ref, act_ref, ffn_ref) = refs

    for blk in range(n_blk):
        rows = pl.ds(blk * BLK, BLK)
        h_ref[rows, :] = _rmsnorm(x_ref[blk], g_pre_ref[...]).astype(BF16)

    tpos = lax.broadcasted_iota(jnp.int32, (n_seq_tile, seg, FF_BLK), 1)

    def conv(cols):
        up = _dot(h_ref[...], w_up_ref[:, cols])
        prev0 = buf_in[:, 0:1, cols]
        prev1 = buf_in[:, 1:2, cols]
        up3 = up.reshape(n_seq_tile, seg, FF_BLK)
        sh1 = pltpu.roll(up, 1, axis=0).reshape(n_seq_tile, seg, FF_BLK)
        sh2 = pltpu.roll(up, 2, axis=0).reshape(n_seq_tile, seg, FF_BLK)
        sh1 = jnp.where(tpos == 0, prev1, sh1)
        sh2 = jnp.where(tpos == 0, prev0, jnp.where(tpos == 1, prev1, sh2))
        buf_out[:, 0:1, cols] = up3[:, seg - 2:seg - 1, :]
        buf_out[:, 1:2, cols] = up3[:, seg - 1:seg, :]
        w = conv_w_ref[:, cols]
        out = sh2 * w[0:1, :] + sh1 * w[1:2, :] + up3 * w[2:3, :] + conv_b_ref[:, cols]
        return out.reshape(rows_tile, FF_BLK)

    def activation(gate, val):
        c0 = math.sqrt(2.0 / math.pi)
        c1 = 0.044715 * c0
        t = jnp.tanh(gate * (gate * gate * c1 + c0))
        half = 0.5 * gate
        return ((half + half * t) * val).astype(BF16)

    for j in range(D_FF // FF_BLK):
        cols = slice(j * FF_BLK, (j + 1) * FF_BLK)
        gate = conv(cols)
        val = conv(slice(D_FF + j * FF_BLK, D_FF + (j + 1) * FF_BLK))
        act_ref[:, cols] = activation(gate, val)

    ffn_ref[...] = _dot(act_ref[...], w_down_ref[...])
    for blk in range(n_blk):
        rows = pl.ds(blk * BLK, BLK)
        y_ref[blk] = x_ref[blk] + _rmsnorm(ffn_ref[rows, :], g_post_ref[...])


def _const_spec(shape):
    zeros = (0,) * len(shape)
    return pl.BlockSpec(shape, lambda *_: zeros, pipeline_mode=pl.Buffered(1))


def _compiler_params(n_grid_dims):
    return pltpu.CompilerParams(
        dimension_semantics=("arbitrary",) * n_grid_dims,
        vmem_limit_bytes=VMEM_LIMIT_BYTES)


def _mixer_scratch(n_blk, fresh_state):
    rows = n_blk * BLK
    carried = [pltpu.VMEM((n_blk, N_RET_HEADS, HEAD_DIM, HEAD_DIM), F32)] if fresh_state else []
    return [
        pltpu.VMEM((rows, D_MODEL), BF16),
        pltpu.VMEM((2, rows, 4 * GROUP_W), F32),
        pltpu.VMEM((rows, RET_W + ML_W), BF16),
        pltpu.VMEM((rows, D_MODEL), F32),
        pltpu.VMEM((rows, BLK), F32),
        pltpu.VMEM((rows, BLK), F32),
        pltpu.VMEM((n_blk, BLK, BLK), F32),
        pltpu.VMEM((n_blk, BLK, BLK), F32),
    ] + carried


def _mixer_weight_specs():
    return [
        _const_spec((1, D_MODEL)),
        _const_spec((D_MODEL, 4 * RET_W + 4 * ML_W)),
        _const_spec((D_MODEL, BLK)),
        _const_spec((1, BLK)),
        _const_spec((1, RET_W)),
        _const_spec((1, ML_W)),
        _const_spec((RET_W + ML_W, D_MODEL)),
        _const_spec((1, D_MODEL)),
    ]


def _state_shapes(n_seqs, n_blocks):
    return [
        jax.ShapeDtypeStruct((n_seqs, N_RET_HEADS, HEAD_DIM, HEAD_DIM), F32),
        jax.ShapeDtypeStruct((n_seqs, N_ML_HEADS, HEAD_DIM, HEAD_DIM), F32),
        jax.ShapeDtypeStruct((n_seqs, N_ML_HEADS, HEAD_DIM), F32),
        jax.ShapeDtypeStruct((n_blocks, N_ML_HEADS, BLK), F32),
    ]


def _state_specs(n_seq_tile, n_blk, index):
    return [
        pl.BlockSpec((n_seq_tile, N_RET_HEADS, HEAD_DIM, HEAD_DIM), lambda *g: (index(*g), 0, 0, 0)),
        pl.BlockSpec((n_seq_tile, N_ML_HEADS, HEAD_DIM, HEAD_DIM), lambda *g: (index(*g), 0, 0, 0)),
        pl.BlockSpec((n_seq_tile, N_ML_HEADS, HEAD_DIM), lambda *g: (index(*g), 0, 0)),
        pl.BlockSpec((n_blk, N_ML_HEADS, BLK), lambda *g: (index(*g), 0, 0)),
    ]


def _prompt_mixer(x, cos, sin, weights, n_blk):
    batch, seq_len, _ = x.shape
    grid = (batch // n_blk, seq_len // BLK)
    x_spec = pl.BlockSpec((n_blk, BLK, D_MODEL), lambda g, c: (g, c, 0))
    rope_spec = pl.BlockSpec((BLK, HEAD_DIM), lambda g, c: (c, 0))
    return pl.pallas_call(
        functools.partial(_mixer_kernel, seg=BLK, n_blk=n_blk, fresh_state=True),
        grid=grid,
        in_specs=[x_spec, rope_spec, rope_spec] + _mixer_weight_specs(),
        out_specs=[x_spec] + _state_specs(n_blk, n_blk, lambda g, c: g),
        out_shape=[jax.ShapeDtypeStruct(x.shape, F32)] + _state_shapes(batch, batch),
        scratch_shapes=_mixer_scratch(n_blk, fresh_state=True),
        compiler_params=_compiler_params(2),
        name="prompt_mixer",
    )(x, cos, sin, *weights)


def _sample_mixer(x, cos, sin, weights, states, seg, n_blk):
    n_blocks = x.shape[0]
    n_seq_tile = n_blk * BLK // seg
    grid = (n_blocks // n_blk,)
    x_spec = pl.BlockSpec((n_blk, BLK, D_MODEL), lambda g: (g, 0, 0))
    rope_spec = _const_spec((BLK, HEAD_DIM))
    state_specs = _state_specs(n_seq_tile, n_blk, lambda g: g)
    return pl.pallas_call(
        functools.partial(_mixer_kernel, seg=seg, n_blk=n_blk, fresh_state=False),
        grid=grid,
        in_specs=[x_spec, rope_spec, rope_spec] + _mixer_weight_specs() + state_specs,
        out_specs=[x_spec] + state_specs,
        out_shape=[jax.ShapeDtypeStruct(x.shape, F32)]
        + _state_shapes(n_blocks * BLK // seg, n_blocks),
        scratch_shapes=_mixer_scratch(n_blk, fresh_state=False),
        compiler_params=_compiler_params(1),
        name="sample_mixer",
    )(x, cos, sin, *weights, *states)


def _ffn_scratch(n_blk):
    rows = n_blk * BLK
    return [
        pltpu.VMEM((rows, D_MODEL), BF16),
        pltpu.VMEM((rows, D_FF), BF16),
        pltpu.VMEM((rows, D_MODEL), F32),
    ]


def _ffn_weight_specs():
    return [
        _const_spec((1, D_MODEL)),
        _const_spec((D_MODEL, 2 * D_FF)),
        _const_spec((CONV_W, 2 * D_FF)),
        _const_spec((1, 2 * D_FF)),
        _const_spec((D_FF, D_MODEL)),
        _const_spec((1, D_MODEL)),
    ]


def _prompt_ffn(x, weights, n_blk):
    batch, seq_len, _ = x.shape
    grid = (batch // n_blk, seq_len // BLK)
    x_spec = pl.BlockSpec((n_blk, BLK, D_MODEL), lambda g, c: (g, c, 0))
    buf_spec = pl.BlockSpec((n_blk, CONV_W - 1, 2 * D_FF), lambda g, c: (g, 0, 0))
    return pl.pallas_call(
        functools.partial(_ffn_kernel, seg=BLK, n_blk=n_blk, fresh_state=True),
        grid=grid,
        in_specs=[x_spec] + _ffn_weight_specs(),
        out_specs=[x_spec, buf_spec],
        out_shape=[jax.ShapeDtypeStruct(x.shape, F32),
                   jax.ShapeDtypeStruct((batch, CONV_W - 1, 2 * D_FF), F32)],
        scratch_shapes=_ffn_scratch(n_blk),
        compiler_params=_compiler_params(2),
        name="prompt_ffn",
    )(x, *weights)


def _sample_ffn(x, weights, conv_buf, seg, n_blk):
    n_blocks = x.shape[0]
    n_seq_tile = n_blk * BLK // seg
    grid = (n_blocks // n_blk,)
    x_spec = pl.BlockSpec((n_blk, BLK, D_MODEL), lambda g: (g, 0, 0))
    buf_spec = pl.BlockSpec((n_seq_tile, CONV_W - 1, 2 * D_FF), lambda g: (g, 0, 0))
    return pl.pallas_call(
        functools.partial(_ffn_kernel, seg=seg, n_blk=n_blk, fresh_state=False),
        grid=grid,
        in_specs=[x_spec] + _ffn_weight_specs() + [buf_spec],
        out_specs=[x_spec, buf_spec],
        out_shape=[jax.ShapeDtypeStruct(x.shape, F32),
                   jax.ShapeDtypeStruct(conv_buf.shape, F32)],
        scratch_shapes=_ffn_scratch(n_blk),
        compiler_params=_compiler_params(1),
        name="sample_ffn",
    )(x, *weights, conv_buf)


def _rope_tables(pos):
    freqs = ROPE_BASE ** (-jnp.arange(0, HEAD_DIM, 2, dtype=F32) / HEAD_DIM)
    ang = pos.astype(F32)[:, None] * freqs[None, :]
    cos, sin = jnp.cos(ang), jnp.sin(ang)
    return jnp.concatenate([cos, cos], axis=-1), jnp.concatenate([-sin, sin], axis=-1)


def kernel(x_prompt, x_sample, state_ret, state_mlstm_C, state_mlstm_n, state_mlstm_m, cache_ffn_conv, pre_mix_gain, w_in, b_gates, ret_head_gain, mlstm_head_gain, w_out, post_mix_gain, pre_ffn_gain, w_up, conv_w, conv_b, w_down, post_ffn_gain):
    depth = w_in.shape[0]
    assert depth == 1
    batch, seq_len, _ = x_prompt.shape
    dec_batch, dec_seq, _ = x_sample.shape
    assert seq_len % BLK == 0 and BLK % dec_seq == 0 and (dec_batch * dec_seq) % BLK == 0
    seqs_per_blk = BLK // dec_seq
    n_sample_blocks = dec_batch * dec_seq // BLK
    layer = 0

    n_gate = 2 * N_ML_HEADS
    w_gate = jnp.pad(w_in[layer][:, -n_gate:], ((0, 0), (0, BLK - n_gate))).astype(BF16)
    b_gate = jnp.pad(b_gates[layer], (0, BLK - n_gate)).reshape(1, BLK)
    mixer_weights = (
        pre_mix_gain[layer].reshape(1, D_MODEL), w_in[layer][:, :-n_gate].astype(BF16), w_gate, b_gate,
        ret_head_gain[layer].reshape(1, RET_W), mlstm_head_gain[layer].reshape(1, ML_W),
        w_out[layer].astype(BF16), post_mix_gain[layer].reshape(1, D_MODEL))
    ffn_weights = (
        pre_ffn_gain[layer].reshape(1, D_MODEL), w_up[layer].astype(BF16), conv_w[layer],
        conv_b[layer].reshape(1, 2 * D_FF), w_down[layer].astype(BF16),
        post_ffn_gain[layer].reshape(1, D_MODEL))

    cos_p, sin_p = _rope_tables(jnp.arange(seq_len, dtype=jnp.int32))
    x1_p, s_p, c_p, n_p, m_p = _prompt_mixer(x_prompt, cos_p, sin_p, mixer_weights, n_blk=8)
    y_p, buf_p = _prompt_ffn(x1_p, ffn_weights, n_blk=8)
    m_p = m_p[:, :, 0]

    pos_s = PAST_LEN + jnp.arange(dec_seq, dtype=jnp.int32)
    cos_s, sin_s = _rope_tables(jnp.tile(pos_s, seqs_per_blk))
    m_blocks = jnp.transpose(state_mlstm_m[layer].reshape(n_sample_blocks, seqs_per_blk, N_ML_HEADS), (0, 2, 1))
    m_blocks = jnp.pad(m_blocks, ((0, 0), (0, 0), (0, BLK - seqs_per_blk)))
    xs = x_sample.reshape(n_sample_blocks, BLK, D_MODEL)
    x1_s, s_s, c_s, n_s, m_s = _sample_mixer(
        xs, cos_s, sin_s, mixer_weights,
        (state_ret[layer], state_mlstm_C[layer], state_mlstm_n[layer], m_blocks),
        seg=dec_seq, n_blk=1)
    y_s, buf_s = _sample_ffn(x1_s, ffn_weights, cache_ffn_conv[layer], seg=dec_seq, n_blk=1)
    y_s = y_s.reshape(x_sample.shape)
    m_s = jnp.transpose(m_s[:, :, :seqs_per_blk], (0, 2, 1)).reshape(dec_batch, N_ML_HEADS)

    return (y_p, y_s, s_p[None], s_s[None], c_p[None], c_s[None], n_p[None], n_s[None],
            m_p[None], m_s[None], buf_p[None], buf_s[None])
```

```python
import functools
import math

import jax
import jax.numpy as jnp
from jax import lax
from jax.experimental import pallas as pl
from jax.experimental.pallas import tpu as pltpu

D_MODEL = 1024
HEAD_DIM = 128
N_RET_HEADS = 4
N_ML_HEADS = 4
RET_W = N_RET_HEADS * HEAD_DIM
ML_W = N_ML_HEADS * HEAD_DIM
D_FF = 2816
CONV_W = 3
PAST_LEN = 16384
ROPE_BASE = 10000.0
EPS = 1e-6
M_INIT = -1e30

BLK = 128
HEADS_PER_GROUP = 2
GROUP_W = HEADS_PER_GROUP * HEAD_DIM
FF_BLK = 256
SUBLANES = 8
VMEM_LIMIT_BYTES = 56 * 1024 * 1024

F32 = jnp.float32
BF16 = jnp.bfloat16


def _dot(a, b, precision=None):
    return jnp.dot(a, b, preferred_element_type=F32, precision=precision)


def _dot_nt(a, b):
    return lax.dot_general(a, b, (((1,), (1,)), ((), ())), preferred_element_type=F32)


def _rmsnorm(x, g):
    return x * lax.rsqrt(jnp.mean(x * x, axis=-1, keepdims=True) + EPS) * g


def _groupnorm(h, g):
    mu = jnp.mean(h, axis=-1, keepdims=True)
    d = h - mu
    var = jnp.mean(d * d, axis=-1, keepdims=True)
    return d * lax.rsqrt(var + EPS) * g


def _ret_log_gamma(h):
    return math.log(1.0 - 2.0 ** (-5.0 - h))


class _Masks:
    def __init__(self, seg):
        shift = seg.bit_length() - 1
        r = lax.broadcasted_iota(jnp.int32, (BLK, BLK), 0)
        c = lax.broadcasted_iota(jnp.int32, (BLK, BLK), 1)
        rseq = r >> shift
        self.causal = (rseq == (c >> shift)) & (r >= c)
        self.tpos = (r & (seg - 1)).astype(F32)
        self.diff = (r - c).astype(F32)
        self.col_is_last_of_rowseq = c == (rseq * seg + (seg - 1))
        self.col_is_rowseq = c == rseq
        self.row_is_last_of_colseq = r == (c * seg + (seg - 1))


def _rows_from_seq(mask_col_is_rowseq, seq_row):
    return jnp.sum(jnp.where(mask_col_is_rowseq, seq_row, 0.0), axis=1, keepdims=True)


def _seq_from_last_row(mask_row_is_last_of_colseq, col):
    return jnp.sum(jnp.where(mask_row_is_last_of_colseq, col, 0.0), axis=0, keepdims=True)


def _mixer_kernel(*refs, seg, n_blk, fresh_state):
    n_seq = BLK // seg
    assert fresh_state == (n_seq == 1)
    if fresh_state:
        (x_ref, cos_ref, sin_ref, g_pre_ref, w_in_ref, w_gate_ref, b_gate_ref, g_ret_ref, g_ml_ref,
         w_out_ref, g_post_ref,
         y_ref, s_out, c_out, n_out, m_out,
         h_ref, proj_ref, mix_ref, mixed_ref, gate_ref, bcum_ref, gate_t_ref, bcum_t_ref, st_ref) = refs
        c_in, n_in, m_in = c_out, n_out, m_out

        @pl.when(pl.program_id(1) == 0)
        def _():
            st_ref[...] = jnp.zeros(st_ref.shape, F32)
            c_out[...] = jnp.zeros(c_out.shape, F32)
            n_out[...] = jnp.zeros(n_out.shape, F32)
            m_out[...] = jnp.full(m_out.shape, M_INIT, F32)
    else:
        (x_ref, cos_ref, sin_ref, g_pre_ref, w_in_ref, w_gate_ref, b_gate_ref, g_ret_ref, g_ml_ref,
         w_out_ref, g_post_ref, s_in, c_in, n_in, m_in,
         y_ref, s_out, c_out, n_out, m_out,
         h_ref, proj_ref, mix_ref, mixed_ref, gate_ref, bcum_ref, gate_t_ref, bcum_t_ref) = refs

    masks = _Masks(seg)
    scale = HEAD_DIM ** -0.5
    tri = jnp.where(masks.causal, 1.0, 0.0).astype(F32)
    cos = cos_ref[...]
    sin = sin_ref[...]

    def prologue(blocks, rows_all):
        for blk in blocks:
            rows = pl.ds(blk * BLK, BLK)
            h_ref[rows, :] = _rmsnorm(x_ref[blk], g_pre_ref[...]).astype(BF16)
        gate_ref[rows_all, :] = _dot(h_ref[rows_all, :], w_gate_ref[...]) + b_gate_ref[...]
        for blk in blocks:
            rows = pl.ds(blk * BLK, BLK)
            gates = gate_ref[rows, :]
            bcum = _dot(tri, jax.nn.log_sigmoid(gates), precision=lax.Precision.HIGHEST)
            bcum_ref[rows, :] = bcum
            gate_t_ref[blk] = gates.T
            bcum_t_ref[blk] = bcum.T

    def rotary(t):
        return t * cos + pltpu.roll(t, HEAD_DIM // 2, axis=1) * sin

    def seq_rows(j):
        return slice(j * seg, (j + 1) * seg)

    def lane_mask(j):
        c = lax.broadcasted_iota(jnp.int32, (BLK, BLK), 1)
        return (c >= j * seg) & (c < (j + 1) * seg)

    def ret_head(proj, blk, head, col):
        rows = pl.ds(blk * BLK, BLK)
        lg = _ret_log_gamma(head)
        q = rotary(proj[rows,col:col + HEAD_DIM])
        k = rotary(proj[rows,GROUP_W + col:GROUP_W + col + HEAD_DIM]) * scale
        v = proj[rows,2 * GROUP_W + col:2 * GROUP_W + col + HEAD_DIM]
        g = proj[rows,3 * GROUP_W + col:3 * GROUP_W + col + HEAD_DIM]
        qb, kb, vb = q.astype(BF16), k.astype(BF16), v.astype(BF16)
        dmat = jnp.where(masks.causal, jnp.exp(jnp.where(masks.causal, masks.diff * lg, 0.0)), 0.0)
        xi = jnp.exp((masks.tpos + 1.0) * lg)
        zeta = jnp.exp((float(seg - 1) - masks.tpos) * lg)
        chunk_decay = math.exp(seg * lg)
        scores = _dot_nt(qb, kb)
        kz_t = (k * zeta).T
        inter = []
        for j in range(n_seq):
            seq = blk * n_seq + j
            s_prev = s_in[seq, head]
            inter.append(_dot(q[seq_rows(j)].astype(BF16), s_prev.astype(BF16)))
            kz_j = jnp.where(lane_mask(j), kz_t, 0.0)
            s_out[seq, head] = s_prev * chunk_decay + _dot(kz_j.astype(BF16), vb)
        inter = jnp.concatenate(inter, axis=0)
        yield
        out = _dot((scores * dmat).astype(BF16), vb)
        yield
        out = out + inter * xi
        y = _groupnorm(out, g_ret_ref[:, head * HEAD_DIM:(head + 1) * HEAD_DIM]) * (g * jax.nn.sigmoid(g))
        mix_ref[rows, head * HEAD_DIM:(head + 1) * HEAD_DIM] = y.astype(BF16)

    def ml_head(proj, blk, head, col):
        rows = pl.ds(blk * BLK, BLK)
        q = proj[rows,col:col + HEAD_DIM]
        k = proj[rows,GROUP_W + col:GROUP_W + col + HEAD_DIM] * scale
        v = proj[rows,2 * GROUP_W + col:2 * GROUP_W + col + HEAD_DIM]
        o = proj[rows,3 * GROUP_W + col:3 * GROUP_W + col + HEAD_DIM]
        qb, kb, vb = q.astype(BF16), k.astype(BF16), v.astype(BF16)
        i_col = gate_ref[rows, head:head + 1]
        b_col = bcum_ref[rows, N_ML_HEADS + head:N_ML_HEADS + head + 1]
        i_row = gate_t_ref[blk, head:head + 1, :]
        b_row = bcum_t_ref[blk, N_ML_HEADS + head:N_ML_HEADS + head + 1, :]
        m_prev_seq = m_in[blk, head:head + 1, :]
        m_prev = _rows_from_seq(masks.col_is_rowseq, m_prev_seq)
        logw = jnp.where(masks.causal, b_col - b_row + i_row, -jnp.inf)
        m_t = jnp.maximum(b_col + m_prev, jnp.max(logw, axis=1, keepdims=True))
        w = jnp.exp(logw - m_t)
        inter_w = jnp.exp(b_col + m_prev - m_t)
        s = _dot_nt(qb, kb)
        m_new_seq = _seq_from_last_row(masks.row_is_last_of_colseq, m_t)
        b_last_seq = _seq_from_last_row(masks.row_is_last_of_colseq, b_col)
        decay_seq = jnp.exp(b_last_seq + m_prev_seq - m_new_seq)
        m_new = _rows_from_seq(masks.col_is_rowseq, m_new_seq)
        b_last = jnp.sum(jnp.where(masks.col_is_last_of_rowseq, b_row, 0.0), axis=1, keepdims=True)
        wl = jnp.exp(b_last - b_col + i_col - m_new)
        vw_t = (v * wl).T
        kw = k * wl
        qc, qn = [], []
        for j in range(n_seq):
            seq = blk * n_seq + j
            c_prev = c_in[seq, head]
            n_prev = n_in[seq, head:head + 1, :]
            qc.append(_dot_nt(q[seq_rows(j)].astype(BF16), c_prev.astype(BF16)))
            qn.append(jnp.sum(q[seq_rows(j)] * n_prev, axis=1, keepdims=True))
            decay = decay_seq[:, j:j + 1]
            n_out[seq, head:head + 1, :] = decay * n_prev + jnp.sum(kw[seq_rows(j)], axis=0, keepdims=True)
        qc = jnp.concatenate(qc, axis=0)
        qn = jnp.concatenate(qn, axis=0)
        m_out[blk, head:head + 1, :] = m_new_seq
        yield
        s = s * w
        num = _dot(s.astype(BF16), vb)
        den = jnp.sum(s, axis=1, keepdims=True)
        for j in range(n_seq):
            seq = blk * n_seq + j
            vw_j = jnp.where(lane_mask(j), vw_t, 0.0)
            c_out[seq, head] = decay_seq[:, j:j + 1] * c_in[seq, head] + _dot(vw_j.astype(BF16), kb)
        yield
        num = num + inter_w * qc
        den = jnp.maximum(jnp.abs(den + inter_w * qn), jnp.exp(-m_t))
        hout = num / den
        y = _groupnorm(hout, g_ml_ref[:, head * HEAD_DIM:(head + 1) * HEAD_DIM]) * jax.nn.sigmoid(o)
        mix_ref[rows, RET_W + head * HEAD_DIM:RET_W + (head + 1) * HEAD_DIM] = y.astype(BF16)

    r_idx = lax.broadcasted_iota(jnp.int32, (BLK, BLK), 0)
    c_idx = lax.broadcasted_iota(jnp.int32, (BLK, BLK), 1)
    causal_t = r_idx <= c_idx
    lane_pos = lax.broadcasted_iota(jnp.int32, (1, BLK), 1).astype(F32)
    head_consts = {}

    def ret_consts(head):
        if head not in head_consts:
            lg = _ret_log_gamma(head)
            dmat_t = jnp.where(causal_t, jnp.exp(jnp.where(causal_t, (c_idx - r_idx).astype(F32) * lg, 0.0)), 0.0)
            xi_row = jnp.exp((lane_pos + 1.0) * lg)
            zeta = jnp.exp((float(BLK - 1) - r_idx.astype(F32)) * lg)
            head_consts[head] = (dmat_t, xi_row, zeta, math.exp(BLK * lg))
        return head_consts[head]

    def channel_norm_t(h_t):
        mu = jnp.mean(h_t, axis=0, keepdims=True)
        d = h_t - mu
        var = jnp.mean(d * d, axis=0, keepdims=True)
        return d * lax.rsqrt(var + EPS)

    def ret_head_single(proj, blk, head, col):
        rows = pl.ds(blk * BLK, BLK)
        dmat_t, xi_row, zeta, chunk_decay = ret_consts(head)
        q = rotary(proj[rows, col:col + HEAD_DIM])
        k = rotary(proj[rows, GROUP_W + col:GROUP_W + col + HEAD_DIM]) * scale
        v_t = proj[rows, 2 * GROUP_W + col:2 * GROUP_W + col + HEAD_DIM].T
        qb, kb, v_tb = q.astype(BF16), k.astype(BF16), v_t.astype(BF16)
        scores_t = _dot_nt(kb, qb)
        st_prev = st_ref[blk, head]
        inter_t = _dot_nt(st_prev.astype(BF16), qb)
        st_ref[blk, head] = st_prev * chunk_decay + _dot(v_tb, (k * zeta).astype(BF16))
        yield
        out_t = _dot(v_tb, (scores_t * dmat_t).astype(BF16))
        yield
        out_t = out_t + inter_t * xi_row
        g = proj[rows, 3 * GROUP_W + col:3 * GROUP_W + col + HEAD_DIM]
        y = channel_norm_t(out_t).T * g_ret_ref[:, head * HEAD_DIM:(head + 1) * HEAD_DIM] * (g * jax.nn.sigmoid(g))
        mix_ref[rows, head * HEAD_DIM:(head + 1) * HEAD_DIM] = y.astype(BF16)

    def ml_head_single(proj, blk, head, col):
        rows = pl.ds(blk * BLK, BLK)
        q = proj[rows, col:col + HEAD_DIM]
        k = proj[rows, GROUP_W + col:GROUP_W + col + HEAD_DIM] * scale
        v_t = proj[rows, 2 * GROUP_W + col:2 * GROUP_W + col + HEAD_DIM].T
        qb, kb = q.astype(BF16), k.astype(BF16)
        s_t = _dot_nt(kb, qb)
        c_prev = c_in[blk, head]
        n_prev = n_in[blk, head:head + 1, :]
        qc_t = _dot_nt(c_prev.astype(BF16), qb)
        qn = _dot_nt(jnp.broadcast_to(n_prev, (2 * SUBLANES, HEAD_DIM)).astype(BF16), qb)[0:1, :]
        i_row = gate_t_ref[blk, head:head + 1, :]
        b_row = bcum_t_ref[blk, N_ML_HEADS + head:N_ML_HEADS + head + 1, :]
        key_term = gate_ref[rows, head:head + 1] - bcum_ref[rows, N_ML_HEADS + head:N_ML_HEADS + head + 1]
        m_prev = m_in[blk, head:head + 1, 0:1]
        logw_t = jnp.where(causal_t, b_row + key_term, -jnp.inf)
        m_t = jnp.maximum(b_row + m_prev, jnp.max(logw_t, axis=0, keepdims=True))
        w_t = jnp.exp(logw_t - m_t)
        inter_w = jnp.exp(b_row + m_prev - m_t)
        m_new = m_t[:, BLK - 1:BLK]
        b_last = b_row[:, BLK - 1:BLK]
        decay = jnp.exp(b_last + m_prev - m_new)
        wl = jnp.exp(b_last - b_row + i_row - m_new)
        m_out[blk, head:head + 1, :] = jnp.broadcast_to(m_new, (1, BLK))
        wk = _dot(jnp.broadcast_to(wl, (2 * SUBLANES, BLK)).astype(BF16), kb)[0:1, :]
        n_out[blk, head:head + 1, :] = decay * n_prev + wk
        vw_tb = (v_t * wl).astype(BF16)
        v_tb = v_t.astype(BF16)
        yield
        s_t = s_t * w_t
        num_t = _dot(v_tb, s_t.astype(BF16))
        den = jnp.sum(s_t, axis=0, keepdims=True)
        c_out[blk, head] = decay * c_in[blk, head] + _dot(vw_tb, kb)
        yield
        num_t = num_t + inter_w * qc_t
        den = jnp.maximum(jnp.abs(den + inter_w * qn), jnp.exp(-m_t))
        o = proj[rows, 3 * GROUP_W + col:3 * GROUP_W + col + HEAD_DIM]
        y = (channel_norm_t(num_t / den).T * g_ml_ref[:, head * HEAD_DIM:(head + 1) * HEAD_DIM]
             * jax.nn.sigmoid(o))
        mix_ref[rows, RET_W + head * HEAD_DIM:RET_W + (head + 1) * HEAD_DIM] = y.astype(BF16)

    n_ret_groups = N_RET_HEADS // HEADS_PER_GROUP
    n_groups = n_ret_groups + N_ML_HEADS // HEADS_PER_GROUP

    def first_head_of(grp):
        return (grp if grp < n_ret_groups else grp - n_ret_groups) * HEADS_PER_GROUP

    def project(grp, kind, rows_all):
        is_ret = grp < n_ret_groups
        c0 = (0 if is_ret else 4 * RET_W) + kind * (RET_W if is_ret else ML_W) + first_head_of(grp) * HEAD_DIM
        proj_ref[grp % 2, rows_all, kind * GROUP_W:(kind + 1) * GROUP_W] = _dot(
            h_ref[rows_all, :], w_in_ref[:, c0:c0 + GROUP_W])

    def out_project(grp, rows_all):
        cols = slice(grp * GROUP_W, (grp + 1) * GROUP_W)
        part = _dot(mix_ref[rows_all, cols], w_out_ref[cols, :])
        if grp == 0:
            mixed_ref[rows_all, :] = part
        else:
            mixed_ref[rows_all, :] += part

    def row_pipeline(blocks):
        rows_all = pl.ds(blocks[0] * BLK, len(blocks) * BLK)
        prologue(blocks, rows_all)
        for kind in range(4):
            project(0, kind, rows_all)
        yield
        for grp in range(n_groups):
            if n_seq == 1:
                head_fn = ret_head_single if grp < n_ret_groups else ml_head_single
            else:
                head_fn = ret_head if grp < n_ret_groups else ml_head
            heads = [head_fn(proj_ref.at[grp % 2], blk, first_head_of(grp) + hh, hh * HEAD_DIM)
                     for blk in blocks for hh in range(HEADS_PER_GROUP)]
            fillers = {0: [functools.partial(project, grp + 1, kind, rows_all) for kind in range(4)]
                       if grp + 1 < n_groups else [],
                       1: [functools.partial(out_project, grp - 1, rows_all)] if grp >= 1 else [],
                       2: []}
            for phase in range(3):
                todo = list(fillers[phase])
                every = max(1, len(heads) // max(1, len(todo)))
                for i, head in enumerate(heads):
                    next(head, None)
                    if todo and (i + 1) % every == 0:
                        todo.pop(0)()
                for filler in todo:
                    filler()
            yield
        last_cols = slice((n_groups - 1) * GROUP_W, n_groups * GROUP_W)
        last = _dot(mix_ref[rows_all, last_cols], w_out_ref[last_cols, :])
        for i, blk in enumerate(blocks):
            rows = pl.ds(blk * BLK, BLK)
            mixed = mixed_ref[rows, :] + last[i * BLK:(i + 1) * BLK]
            y_ref[blk] = x_ref[blk] + _rmsnorm(mixed, g_post_ref[...])

    n_runs = 2 if n_blk % 2 == 0 else 1
    per_run = n_blk // n_runs
    pipelines = [row_pipeline(list(range(i * per_run, (i + 1) * per_run))) for i in range(n_runs)]
    n_stages = n_groups + 2
    for step in range(n_stages + n_runs - 1):
        for lag, pipeline in enumerate(pipelines):
            if 0 <= step - lag < n_stages:
                next(pipeline, None)

    if fresh_state:
        @pl.when(pl.program_id(1) == pl.num_programs(1) - 1)
        def _():
            for blk in range(n_blk):
                for head in range(N_RET_HEADS):
                    s_out[blk, head] = st_ref[blk, head].T


def _ffn_kernel(*refs, seg, n_blk, fresh_state):
    n_seq_tile = n_blk * BLK // seg
    rows_tile = n_blk * BLK
    if fresh_state:
        (x_ref, g_pre_ref, w_up_ref, conv_w_ref, conv_b_ref, w_down_ref, g_post_ref,
         y_ref, buf_out, h_ref, act_ref, ffn_ref) = refs
        buf_in = buf_out

        @pl.when(pl.program_id(1) == 0)
        def _():
            buf_out[...] = jnp.zeros(buf_out.shape, F32)
    else:
        (x_ref, g_pre_ref, w_up_ref, conv_w_ref, conv_b_ref, w_down_ref, g_post_ref, buf_in,
         y_ref, buf_out, h_ref, act_ref, ffn_ref) = refs

    for blk in range(n_blk):
        rows = pl.ds(blk * BLK, BLK)
        h_ref[rows, :] = _rmsnorm(x_ref[blk], g_pre_ref[...]).astype(BF16)

    tpos = lax.broadcasted_iota(jnp.int32, (n_seq_tile, seg, FF_BLK), 1)

    def conv(cols):
        up = _dot(h_ref[...], w_up_ref[:, cols])
        prev0 = buf_in[:, 0:1, cols]
        prev1 = buf_in[:, 1:2, cols]
        up3 = up.reshape(n_seq_tile, seg, FF_BLK)
        sh1 = pltpu.roll(up, 1, axis=0).reshape(n_seq_tile, seg, FF_BLK)
        sh2 = pltpu.roll(up, 2, axis=0).reshape(n_seq_tile, seg, FF_BLK)
        sh1 = jnp.where(tpos == 0, prev1, sh1)
        sh2 = jnp.where(tpos == 0, prev0, jnp.where(tpos == 1, prev1, sh2))
        buf_out[:, 0:1, cols] = up3[:, seg - 2:seg - 1, :]
        buf_out[:, 1:2, cols] = up3[:, seg - 1:seg, :]
        w = conv_w_ref[:, cols]
        out = sh2 * w[0:1, :] + sh1 * w[1:2, :] + up3 * w[2:3, :] + conv_b_ref[:, cols]
        return out.reshape(rows_tile, FF_BLK)

    def activation(gate, val):
        c0 = math.sqrt(2.0 / math.pi)
        c1 = 0.044715 * c0
        t = jnp.tanh(gate * (gate * gate * c1 + c0))
        half = 0.5 * gate
        return ((half + half * t) * val).astype(BF16)

    for j in range(D_FF // FF_BLK):
        cols = slice(j * FF_BLK, (j + 1) * FF_BLK)
        gate = conv(cols)
        val = conv(slice(D_FF + j * FF_BLK, D_FF + (j + 1) * FF_BLK))
        act_ref[:, cols] = activation(gate, val)

    ffn_ref[...] = _dot(act_ref[...], w_down_ref[...])
    for blk in range(n_blk):
        rows = pl.ds(blk * BLK, BLK)
        y_ref[blk] = x_ref[blk] + _rmsnorm(ffn_ref[rows, :], g_post_ref[...])


def _const_spec(shape):
    zeros = (0,) * len(shape)
    return pl.BlockSpec(shape, lambda *_: zeros, pipeline_mode=pl.Buffered(1))


def _compiler_params(n_grid_dims):
    return pltpu.CompilerParams(
        dimension_semantics=("arbitrary",) * n_grid_dims,
        vmem_limit_bytes=VMEM_LIMIT_BYTES)


def _mixer_scratch(n_blk, fresh_state):
    rows = n_blk * BLK
    carried = [pltpu.VMEM((n_blk, N_RET_HEADS, HEAD_DIM, HEAD_DIM), F32)] if fresh_state else []
    return [
        pltpu.VMEM((rows, D_MODEL), BF16),
        pltpu.VMEM((2, rows, 4 * GROUP_W), F32),
        pltpu.VMEM((rows, RET_W + ML_W), BF16),
        pltpu.VMEM((rows, D_MODEL), F32),
        pltpu.VMEM((rows, BLK), F32),
        pltpu.VMEM((rows, BLK), F32),
        pltpu.VMEM((n_blk, BLK, BLK), F32),
        pltpu.VMEM((n_blk, BLK, BLK), F32),
    ] + carried


def _mixer_weight_specs():
    return [
        _const_spec((1, D_MODEL)),
        _const_spec((D_MODEL, 4 * RET_W + 4 * ML_W)),
        _const_spec((D_MODEL, BLK)),
        _const_spec((1, BLK)),
        _const_spec((1, RET_W)),
        _const_spec((1, ML_W)),
        _const_spec((RET_W + ML_W, D_MODEL)),
        _const_spec((1, D_MODEL)),
    ]


def _state_shapes(n_seqs, n_blocks):
    return [
        jax.ShapeDtypeStruct((n_seqs, N_RET_HEADS, HEAD_DIM, HEAD_DIM), F32),
        jax.ShapeDtypeStruct((n_seqs, N_ML_HEADS, HEAD_DIM, HEAD_DIM), F32),
        jax.ShapeDtypeStruct((n_seqs, N_ML_HEADS, HEAD_DIM), F32),
        jax.ShapeDtypeStruct((n_blocks, N_ML_HEADS, BLK), F32),
    ]


def _state_specs(n_seq_tile, n_blk, index):
    return [
        pl.BlockSpec((n_seq_tile, N_RET_HEADS, HEAD_DIM, HEAD_DIM), lambda *g: (index(*g), 0, 0, 0)),
        pl.BlockSpec((n_seq_tile, N_ML_HEADS, HEAD_DIM, HEAD_DIM), lambda *g: (index(*g), 0, 0, 0)),
        pl.BlockSpec((n_seq_tile, N_ML_HEADS, HEAD_DIM), lambda *g: (index(*g), 0, 0)),
        pl.BlockSpec((n_blk, N_ML_HEADS, BLK), lambda *g: (index(*g), 0, 0)),
    ]


def _prompt_mixer(x, cos, sin, weights, n_blk):
    batch, seq_len, _ = x.shape
    grid = (batch // n_blk, seq_len // BLK)
    x_spec = pl.BlockSpec((n_blk, BLK, D_MODEL), lambda g, c: (g, c, 0))
    rope_spec = pl.BlockSpec((BLK, HEAD_DIM), lambda g, c: (c, 0))
    return pl.pallas_call(
        functools.partial(_mixer_kernel, seg=BLK, n_blk=n_blk, fresh_state=True),
        grid=grid,
        in_specs=[x_spec, rope_spec, rope_spec] + _mixer_weight_specs(),
        out_specs=[x_spec] + _state_specs(n_blk, n_blk, lambda g, c: g),
        out_shape=[jax.ShapeDtypeStruct(x.shape, F32)] + _state_shapes(batch, batch),
        scratch_shapes=_mixer_scratch(n_blk, fresh_state=True),
        compiler_params=_compiler_params(2),
        name="prompt_mixer",
    )(x, cos, sin, *weights)


def _sample_mixer(x, cos, sin, weights, states, seg, n_blk):
    n_blocks = x.shape[0]
    n_seq_tile = n_blk * BLK // seg
    grid = (n_blocks // n_blk,)
    x_spec = pl.BlockSpec((n_blk, BLK, D_MODEL), lambda g: (g, 0, 0))
    rope_spec = _const_spec((BLK, HEAD_DIM))
    state_specs = _state_specs(n_seq_tile, n_blk, lambda g: g)
    return pl.pallas_call(
        functools.partial(_mixer_kernel, seg=seg, n_blk=n_blk, fresh_state=False),
        grid=grid,
        in_specs=[x_spec, rope_spec, rope_spec] + _mixer_weight_specs() + state_specs,
        out_specs=[x_spec] + state_specs,
        out_shape=[jax.ShapeDtypeStruct(x.shape, F32)]
        + _state_shapes(n_blocks * BLK // seg, n_blocks),
        scratch_shapes=_mixer_scratch(n_blk, fresh_state=False),
        compiler_params=_compiler_params(1),
        name="sample_mixer",
    )(x, cos, sin, *weights, *states)


def _ffn_scratch(n_blk):
    rows = n_blk * BLK
    return [
        pltpu.VMEM((rows, D_MODEL), BF16),
        pltpu.VMEM((rows, D_FF), BF16),
        pltpu.VMEM((rows, D_MODEL), F32),
    ]


def _ffn_weight_specs():
    return [
        _const_spec((1, D_MODEL)),
        _const_spec((D_MODEL, 2 * D_FF)),
        _const_spec((CONV_W, 2 * D_FF)),
        _const_spec((1, 2 * D_FF)),
        _const_spec((D_FF, D_MODEL)),
        _const_spec((1, D_MODEL)),
    ]


def _prompt_ffn(x, weights, n_blk):
    batch, seq_len, _ = x.shape
    grid = (batch // n_blk, seq_len // BLK)
    x_spec = pl.BlockSpec((n_blk, BLK, D_MODEL), lambda g, c: (g, c, 0))
    buf_spec = pl.BlockSpec((n_blk, CONV_W - 1, 2 * D_FF), lambda g, c: (g, 0, 0))
    return pl.pallas_call(
        functools.partial(_ffn_kernel, seg=BLK, n_blk=n_blk, fresh_state=True),
        grid=grid,
        in_specs=[x_spec] + _ffn_weight_specs(),
        out_specs=[x_spec, buf_spec],
        out_shape=[jax.ShapeDtypeStruct(x.shape, F32),
                   jax.ShapeDtypeStruct((batch, CONV_W - 1, 2 * D_FF), F32)],
        scratch_shapes=_ffn_scratch(n_blk),
        compiler_params=_compiler_params(2),
        name="prompt_ffn",
    )(x, *weights)


def _sample_ffn(x, weights, conv_buf, seg, n_blk):
    n_blocks = x.shape[0]
    n_seq_tile = n_blk * BLK // seg
    grid = (n_blocks // n_blk,)
    x_spec = pl.BlockSpec((n_blk, BLK, D_MODEL), lambda g: (g, 0, 0))
    buf_spec = pl.BlockSpec((n_seq_tile, CONV_W - 1, 2 * D_FF), lambda g: (g, 0, 0))
    return pl.pallas_call(
        functools.partial(_ffn_kernel, seg=seg, n_blk=n_blk, fresh_state=False),
        grid=grid,
        in_specs=[x_spec] + _ffn_weight_specs() + [buf_spec],
        out_specs=[x_spec, buf_spec],
        out_shape=[jax.ShapeDtypeStruct(x.shape, F32),
                   jax.ShapeDtypeStruct(conv_buf.shape, F32)],
        scratch_shapes=_ffn_scratch(n_blk),
        compiler_params=_compiler_params(1),
        name="sample_ffn",
    )(x, *weights, conv_buf)


def _rope_tables(pos):
    freqs = ROPE_BASE ** (-jnp.arange(0, HEAD_DIM, 2, dtype=F32) / HEAD_DIM)
    ang = pos.astype(F32)[:, None] * freqs[None, :]
    cos, sin = jnp.cos(ang), jnp.sin(ang)
    return jnp.concatenate([cos, cos], axis=-1), jnp.concatenate([-sin, sin], axis=-1)


def kernel(x_prompt, x_sample, state_ret, state_mlstm_C, state_mlstm_n, state_mlstm_m, cache_ffn_conv, pre_mix_gain, w_in, b_gates, ret_head_gain, mlstm_head_gain, w_out, post_mix_gain, pre_ffn_gain, w_up, conv_w, conv_b, w_down, post_ffn_gain):
    depth = w_in.shape[0]
    assert depth == 1
    batch, seq_len, _ = x_prompt.shape
    dec_batch, dec_seq, _ = x_sample.shape
    assert seq_len % BLK == 0 and BLK % dec_seq == 0 and (dec_batch * dec_seq) % BLK == 0
    seqs_per_blk = BLK // dec_seq
    n_sample_blocks = dec_batch * dec_seq // BLK
    layer = 0

    n_gate = 2 * N_ML_HEADS
    w_gate = jnp.pad(w_in[layer][:, -n_gate:], ((0, 0), (0, BLK - n_gate))).astype(BF16)
    b_gate = jnp.pad(b_gates[layer], (0, BLK - n_gate)).reshape(1, BLK)
    mixer_weights = (
        pre_mix_gain[layer].reshape(1, D_MODEL), w_in[layer][:, :-n_gate].astype(BF16), w_gate, b_gate,
        ret_head_gain[layer].reshape(1, RET_W), mlstm_head_gain[layer].reshape(1, ML_W),
        w_out[layer].astype(BF16), post_mix_gain[layer].reshape(1, D_MODEL))
    ffn_weights = (
        pre_ffn_gain[layer].reshape(1, D_MODEL), w_up[layer].astype(BF16), conv_w[layer],
        conv_b[layer].reshape(1, 2 * D_FF), w_down[layer].astype(BF16),
        post_ffn_gain[layer].reshape(1, D_MODEL))

    cos_p, sin_p = _rope_tables(jnp.arange(seq_len, dtype=jnp.int32))
    x1_p, s_p, c_p, n_p, m_p = _prompt_mixer(x_prompt, cos_p, sin_p, mixer_weights, n_blk=8)
    y_p, buf_p = _prompt_ffn(x1_p, ffn_weights, n_blk=8)
    m_p = m_p[:, :, 0]

    pos_s = PAST_LEN + jnp.arange(dec_seq, dtype=jnp.int32)
    cos_s, sin_s = _rope_tables(jnp.tile(pos_s, seqs_per_blk))
    m_blocks = jnp.transpose(state_mlstm_m[layer].reshape(n_sample_blocks, seqs_per_blk, N_ML_HEADS), (0, 2, 1))
    m_blocks = jnp.pad(m_blocks, ((0, 0), (0, 0), (0, BLK - seqs_per_blk)))
    xs = x_sample.reshape(n_sample_blocks, BLK, D_MODEL)
    x1_s, s_s, c_s, n_s, m_s = _sample_mixer(
        xs, cos_s, sin_s, mixer_weights,
        (state_ret[layer], state_mlstm_C[layer], state_mlstm_n[layer], m_blocks),
        seg=dec_seq, n_blk=1)
    y_s, buf_s = _sample_ffn(x1_s, ffn_weights, cache_ffn_conv[layer], seg=dec_seq, n_blk=2)
    y_s = y_s.reshape(x_sample.shape)
    m_s = jnp.transpose(m_s[:, :, :seqs_per_blk], (0, 2, 1)).reshape(dec_batch, N_ML_HEADS)

    return (y_p, y_s, s_p[None], s_s[None], c_p[None], c_s[None], n_p[None], n_s[None],
            m_p[None], m_s[None], buf_p[None], buf_s[None])
```

```python
import functools
import math

import jax
import jax.numpy as jnp
from jax import lax
from jax.experimental import pallas as pl
from jax.experimental.pallas import tpu as pltpu

D_MODEL = 1024
HEAD_DIM = 128
N_RET_HEADS = 4
N_ML_HEADS = 4
RET_W = N_RET_HEADS * HEAD_DIM
ML_W = N_ML_HEADS * HEAD_DIM
D_FF = 2816
CONV_W = 3
PAST_LEN = 16384
ROPE_BASE = 10000.0
EPS = 1e-6
M_INIT = -1e30

BLK = 128
HEADS_PER_GROUP = 2
GROUP_W = HEADS_PER_GROUP * HEAD_DIM
FF_BLK = 256
SUBLANES = 8
VMEM_LIMIT_BYTES = 56 * 1024 * 1024

F32 = jnp.float32
BF16 = jnp.bfloat16


def _dot(a, b, precision=None):
    return jnp.dot(a, b, preferred_element_type=F32, precision=precision)


def _dot_nt(a, b):
    return lax.dot_general(a, b, (((1,), (1,)), ((), ())), preferred_element_type=F32)


def _rmsnorm(x, g):
    return x * lax.rsqrt(jnp.mean(x * x, axis=-1, keepdims=True) + EPS) * g


def _groupnorm(h, g):
    mu = jnp.mean(h, axis=-1, keepdims=True)
    d = h - mu
    var = jnp.mean(d * d, axis=-1, keepdims=True)
    return d * lax.rsqrt(var + EPS) * g


def _ret_log_gamma(h):
    return math.log(1.0 - 2.0 ** (-5.0 - h))


class _Masks:
    def __init__(self, seg):
        shift = seg.bit_length() - 1
        r = lax.broadcasted_iota(jnp.int32, (BLK, BLK), 0)
        c = lax.broadcasted_iota(jnp.int32, (BLK, BLK), 1)
        rseq = r >> shift
        self.causal = (rseq == (c >> shift)) & (r >= c)
        self.tpos = (r & (seg - 1)).astype(F32)
        self.diff = (r - c).astype(F32)
        self.col_is_last_of_rowseq = c == (rseq * seg + (seg - 1))
        self.col_is_rowseq = c == rseq
        self.row_is_last_of_colseq = r == (c * seg + (seg - 1))


def _rows_from_seq(mask_col_is_rowseq, seq_row):
    return jnp.sum(jnp.where(mask_col_is_rowseq, seq_row, 0.0), axis=1, keepdims=True)


def _seq_from_last_row(mask_row_is_last_of_colseq, col):
    return jnp.sum(jnp.where(mask_row_is_last_of_colseq, col, 0.0), axis=0, keepdims=True)


def _mixer_kernel(*refs, seg, n_blk, fresh_state):
    n_seq = BLK // seg
    assert fresh_state == (n_seq == 1)
    if fresh_state:
        (x_ref, cos_ref, sin_ref, g_pre_ref, w_in_ref, w_gate_ref, b_gate_ref, g_ret_ref, g_ml_ref,
         w_out_ref, g_post_ref,
         y_ref, s_out, c_out, n_out, m_out,
         h_ref, proj_ref, mix_ref, mixed_ref, gate_ref, bcum_ref, gate_t_ref, bcum_t_ref, st_ref) = refs
        c_in, n_in, m_in = c_out, n_out, m_out

        @pl.when(pl.program_id(1) == 0)
        def _():
            st_ref[...] = jnp.zeros(st_ref.shape, F32)
            c_out[...] = jnp.zeros(c_out.shape, F32)
            n_out[...] = jnp.zeros(n_out.shape, F32)
            m_out[...] = jnp.full(m_out.shape, M_INIT, F32)
    else:
        (x_ref, cos_ref, sin_ref, g_pre_ref, w_in_ref, w_gate_ref, b_gate_ref, g_ret_ref, g_ml_ref,
         w_out_ref, g_post_ref, s_in, c_in, n_in, m_in,
         y_ref, s_out, c_out, n_out, m_out,
         h_ref, proj_ref, mix_ref, mixed_ref, gate_ref, bcum_ref, gate_t_ref, bcum_t_ref) = refs

    masks = _Masks(seg)
    scale = HEAD_DIM ** -0.5
    tri = jnp.where(masks.causal, 1.0, 0.0).astype(F32)
    cos = cos_ref[...]
    sin = sin_ref[...]

    def prologue(blocks, rows_all):
        for blk in blocks:
            rows = pl.ds(blk * BLK, BLK)
            h_ref[rows, :] = _rmsnorm(x_ref[blk], g_pre_ref[...]).astype(BF16)
        gate_ref[rows_all, :] = _dot(h_ref[rows_all, :], w_gate_ref[...]) + b_gate_ref[...]
        for blk in blocks:
            rows = pl.ds(blk * BLK, BLK)
            gates = gate_ref[rows, :]
            bcum = _dot(tri, jax.nn.log_sigmoid(gates), precision=lax.Precision.HIGHEST)
            bcum_ref[rows, :] = bcum
            gate_t_ref[blk] = gates.T
            bcum_t_ref[blk] = bcum.T

    def rotary(t):
        return t * cos + pltpu.roll(t, HEAD_DIM // 2, axis=1) * sin

    def seq_rows(j):
        return slice(j * seg, (j + 1) * seg)

    def lane_mask(j):
        c = lax.broadcasted_iota(jnp.int32, (BLK, BLK), 1)
        return (c >= j * seg) & (c < (j + 1) * seg)

    def ret_head(proj, blk, head, col):
        rows = pl.ds(blk * BLK, BLK)
        lg = _ret_log_gamma(head)
        q = rotary(proj[rows,col:col + HEAD_DIM])
        k = rotary(proj[rows,GROUP_W + col:GROUP_W + col + HEAD_DIM]) * scale
        v = proj[rows,2 * GROUP_W + col:2 * GROUP_W + col + HEAD_DIM]
        g = proj[rows,3 * GROUP_W + col:3 * GROUP_W + col + HEAD_DIM]
        qb, kb, vb = q.astype(BF16), k.astype(BF16), v.astype(BF16)
        dmat = jnp.where(masks.causal, jnp.exp(jnp.where(masks.causal, masks.diff * lg, 0.0)), 0.0)
        xi = jnp.exp((masks.tpos + 1.0) * lg)
        zeta = jnp.exp((float(seg - 1) - masks.tpos) * lg)
        chunk_decay = math.exp(seg * lg)
        scores = _dot_nt(qb, kb)
        kz_t = (k * zeta).T
        inter = []
        kz_tb = kz_t.astype(BF16)
        updates = _dot(jnp.concatenate(
            [jnp.where(lane_mask(j), kz_tb, jnp.zeros_like(kz_tb)) for j in range(n_seq)], axis=0), vb)
        for j in range(n_seq):
            seq = blk * n_seq + j
            s_prev = s_in[seq, head]
            inter.append(_dot(q[seq_rows(j)].astype(BF16), s_prev.astype(BF16)))
            s_out[seq, head] = s_prev * chunk_decay + updates[j * HEAD_DIM:(j + 1) * HEAD_DIM]
        inter = jnp.concatenate(inter, axis=0)
        yield
        out = _dot((scores * dmat).astype(BF16), vb)
        yield
        out = out + inter * xi
        y = _groupnorm(out, g_ret_ref[:, head * HEAD_DIM:(head + 1) * HEAD_DIM]) * (g * jax.nn.sigmoid(g))
        mix_ref[rows, head * HEAD_DIM:(head + 1) * HEAD_DIM] = y.astype(BF16)

    def ml_head(proj, blk, head, col):
        rows = pl.ds(blk * BLK, BLK)
        q = proj[rows,col:col + HEAD_DIM]
        k = proj[rows,GROUP_W + col:GROUP_W + col + HEAD_DIM] * scale
        v = proj[rows,2 * GROUP_W + col:2 * GROUP_W + col + HEAD_DIM]
        o = proj[rows,3 * GROUP_W + col:3 * GROUP_W + col + HEAD_DIM]
        qb, kb, vb = q.astype(BF16), k.astype(BF16), v.astype(BF16)
        i_col = gate_ref[rows, head:head + 1]
        b_col = bcum_ref[rows, N_ML_HEADS + head:N_ML_HEADS + head + 1]
        i_row = gate_t_ref[blk, head:head + 1, :]
        b_row = bcum_t_ref[blk, N_ML_HEADS + head:N_ML_HEADS + head + 1, :]
        m_prev_seq = m_in[blk, head:head + 1, :]
        m_prev = _rows_from_seq(masks.col_is_rowseq, m_prev_seq)
        logw = jnp.where(masks.causal, b_col - b_row + i_row, -jnp.inf)
        m_t = jnp.maximum(b_col + m_prev, jnp.max(logw, axis=1, keepdims=True))
        w = jnp.exp(logw - m_t)
        inter_w = jnp.exp(b_col + m_prev - m_t)
        s = _dot_nt(qb, kb)
        m_new_seq = _seq_from_last_row(masks.row_is_last_of_colseq, m_t)
        b_last_seq = _seq_from_last_row(masks.row_is_last_of_colseq, b_col)
        decay_seq = jnp.exp(b_last_seq + m_prev_seq - m_new_seq)
        m_new = _rows_from_seq(masks.col_is_rowseq, m_new_seq)
        b_last = jnp.sum(jnp.where(masks.col_is_last_of_rowseq, b_row, 0.0), axis=1, keepdims=True)
        wl = jnp.exp(b_last - b_col + i_col - m_new)
        vw_t = (v * wl).T
        kw = k * wl
        qc, qn = [], []
        for j in range(n_seq):
            seq = blk * n_seq + j
            c_prev = c_in[seq, head]
            n_prev = n_in[seq, head:head + 1, :]
            qc.append(_dot_nt(q[seq_rows(j)].astype(BF16), c_prev.astype(BF16)))
            qn.append(jnp.sum(q[seq_rows(j)] * n_prev, axis=1, keepdims=True))
            decay = decay_seq[:, j:j + 1]
            n_out[seq, head:head + 1, :] = decay * n_prev + jnp.sum(kw[seq_rows(j)], axis=0, keepdims=True)
        qc = jnp.concatenate(qc, axis=0)
        qn = jnp.concatenate(qn, axis=0)
        m_out[blk, head:head + 1, :] = m_new_seq
        yield
        s = s * w
        num = _dot(s.astype(BF16), vb)
        den = jnp.sum(s, axis=1, keepdims=True)
        vw_tb = vw_t.astype(BF16)
        updates = _dot(jnp.concatenate(
            [jnp.where(lane_mask(j), vw_tb, jnp.zeros_like(vw_tb)) for j in range(n_seq)], axis=0), kb)
        for j in range(n_seq):
            seq = blk * n_seq + j
            c_out[seq, head] = (decay_seq[:, j:j + 1] * c_in[seq, head]
                                + updates[j * HEAD_DIM:(j + 1) * HEAD_DIM])
        yield
        num = num + inter_w * qc
        den = jnp.maximum(jnp.abs(den + inter_w * qn), jnp.exp(-m_t))
        hout = num / den
        y = _groupnorm(hout, g_ml_ref[:, head * HEAD_DIM:(head + 1) * HEAD_DIM]) * jax.nn.sigmoid(o)
        mix_ref[rows, RET_W + head * HEAD_DIM:RET_W + (head + 1) * HEAD_DIM] = y.astype(BF16)

    r_idx = lax.broadcasted_iota(jnp.int32, (BLK, BLK), 0)
    c_idx = lax.broadcasted_iota(jnp.int32, (BLK, BLK), 1)
    causal_t = r_idx <= c_idx
    lane_pos = lax.broadcasted_iota(jnp.int32, (1, BLK), 1).astype(F32)
    head_consts = {}

    def ret_consts(head):
        if head not in head_consts:
            lg = _ret_log_gamma(head)
            dmat_t = jnp.where(causal_t, jnp.exp(jnp.where(causal_t, (c_idx - r_idx).astype(F32) * lg, 0.0)), 0.0)
            xi_row = jnp.exp((lane_pos + 1.0) * lg)
            zeta = jnp.exp((float(BLK - 1) - r_idx.astype(F32)) * lg)
            head_consts[head] = (dmat_t, xi_row, zeta, math.exp(BLK * lg))
        return head_consts[head]

    def channel_norm_t(h_t):
        mu = jnp.mean(h_t, axis=0, keepdims=True)
        d = h_t - mu
        var = jnp.mean(d * d, axis=0, keepdims=True)
        return d * lax.rsqrt(var + EPS)

    def ret_head_single(proj, blk, head, col):
        rows = pl.ds(blk * BLK, BLK)
        dmat_t, xi_row, zeta, chunk_decay = ret_consts(head)
        q = rotary(proj[rows, col:col + HEAD_DIM])
        k = rotary(proj[rows, GROUP_W + col:GROUP_W + col + HEAD_DIM]) * scale
        v_t = proj[rows, 2 * GROUP_W + col:2 * GROUP_W + col + HEAD_DIM].T
        qb, kb, v_tb = q.astype(BF16), k.astype(BF16), v_t.astype(BF16)
        scores_t = _dot_nt(kb, qb)
        st_prev = st_ref[blk, head]
        inter_t = _dot_nt(st_prev.astype(BF16), qb)
        st_ref[blk, head] = st_prev * chunk_decay + _dot(v_tb, (k * zeta).astype(BF16))
        yield
        out_t = _dot(v_tb, (scores_t * dmat_t).astype(BF16))
        yield
        out_t = out_t + inter_t * xi_row
        g = proj[rows, 3 * GROUP_W + col:3 * GROUP_W + col + HEAD_DIM]
        y = channel_norm_t(out_t).T * g_ret_ref[:, head * HEAD_DIM:(head + 1) * HEAD_DIM] * (g * jax.nn.sigmoid(g))
        mix_ref[rows, head * HEAD_DIM:(head + 1) * HEAD_DIM] = y.astype(BF16)

    def ml_head_single(proj, blk, head, col):
        rows = pl.ds(blk * BLK, BLK)
        q = proj[rows, col:col + HEAD_DIM]
        k = proj[rows, GROUP_W + col:GROUP_W + col + HEAD_DIM] * scale
        v_t = proj[rows, 2 * GROUP_W + col:2 * GROUP_W + col + HEAD_DIM].T
        qb, kb = q.astype(BF16), k.astype(BF16)
        s_t = _dot_nt(kb, qb)
        c_prev = c_in[blk, head]
        n_prev = n_in[blk, head:head + 1, :]
        qc_t = _dot_nt(c_prev.astype(BF16), qb)
        qn = _dot_nt(jnp.broadcast_to(n_prev, (2 * SUBLANES, HEAD_DIM)).astype(BF16), qb)[0:1, :]
        i_row = gate_t_ref[blk, head:head + 1, :]
        b_row = bcum_t_ref[blk, N_ML_HEADS + head:N_ML_HEADS + head + 1, :]
        key_term = gate_ref[rows, head:head + 1] - bcum_ref[rows, N_ML_HEADS + head:N_ML_HEADS + head + 1]
        m_prev = m_in[blk, head:head + 1, 0:1]
        logw_t = jnp.where(causal_t, b_row + key_term, -jnp.inf)
        m_t = jnp.maximum(b_row + m_prev, jnp.max(logw_t, axis=0, keepdims=True))
        w_t = jnp.exp(logw_t - m_t)
        inter_w = jnp.exp(b_row + m_prev - m_t)
        m_new = m_t[:, BLK - 1:BLK]
        b_last = b_row[:, BLK - 1:BLK]
        decay = jnp.exp(b_last + m_prev - m_new)
        wl = jnp.exp(b_last - b_row + i_row - m_new)
        m_out[blk, head:head + 1, :] = jnp.broadcast_to(m_new, (1, BLK))
        wk = _dot(jnp.broadcast_to(wl, (2 * SUBLANES, BLK)).astype(BF16), kb)[0:1, :]
        n_out[blk, head:head + 1, :] = decay * n_prev + wk
        vw_tb = (v_t * wl).astype(BF16)
        v_tb = v_t.astype(BF16)
        yield
        s_t = s_t * w_t
        num_t = _dot(v_tb, s_t.astype(BF16))
        den = jnp.sum(s_t, axis=0, keepdims=True)
        c_out[blk, head] = decay * c_in[blk, head] + _dot(vw_tb, kb)
        yield
        num_t = num_t + inter_w * qc_t
        den = jnp.maximum(jnp.abs(den + inter_w * qn), jnp.exp(-m_t))
        o = proj[rows, 3 * GROUP_W + col:3 * GROUP_W + col + HEAD_DIM]
        y = (channel_norm_t(num_t / den).T * g_ml_ref[:, head * HEAD_DIM:(head + 1) * HEAD_DIM]
             * jax.nn.sigmoid(o))
        mix_ref[rows, RET_W + head * HEAD_DIM:RET_W + (head + 1) * HEAD_DIM] = y.astype(BF16)

    n_ret_groups = N_RET_HEADS // HEADS_PER_GROUP
    n_groups = n_ret_groups + N_ML_HEADS // HEADS_PER_GROUP

    def first_head_of(grp):
        return (grp if grp < n_ret_groups else grp - n_ret_groups) * HEADS_PER_GROUP

    def project(grp, kind, rows_all):
        is_ret = grp < n_ret_groups
        c0 = (0 if is_ret else 4 * RET_W) + kind * (RET_W if is_ret else ML_W) + first_head_of(grp) * HEAD_DIM
        proj_ref[grp % 2, rows_all, kind * GROUP_W:(kind + 1) * GROUP_W] = _dot(
            h_ref[rows_all, :], w_in_ref[:, c0:c0 + GROUP_W])

    def out_project(grp, rows_all):
        cols = slice(grp * GROUP_W, (grp + 1) * GROUP_W)
        part = _dot(mix_ref[rows_all, cols], w_out_ref[cols, :])
        if grp == 0:
            mixed_ref[rows_all, :] = part
        else:
            mixed_ref[rows_all, :] += part

    def row_pipeline(blocks):
        rows_all = pl.ds(blocks[0] * BLK, len(blocks) * BLK)
        prologue(blocks, rows_all)
        for kind in range(4):
            project(0, kind, rows_all)
        yield
        for grp in range(n_groups):
            if n_seq == 1:
                head_fn = ret_head_single if grp < n_ret_groups else ml_head_single
            else:
                head_fn = ret_head if grp < n_ret_groups else ml_head
            heads = [head_fn(proj_ref.at[grp % 2], blk, first_head_of(grp) + hh, hh * HEAD_DIM)
                     for blk in blocks for hh in range(HEADS_PER_GROUP)]
            fillers = {0: [functools.partial(project, grp + 1, kind, rows_all) for kind in range(4)]
                       if grp + 1 < n_groups else [],
                       1: [functools.partial(out_project, grp - 1, rows_all)] if grp >= 1 else [],
                       2: []}
            for phase in range(3):
                todo = list(fillers[phase])
                every = max(1, len(heads) // max(1, len(todo)))
                for i, head in enumerate(heads):
                    next(head, None)
                    if todo and (i + 1) % every == 0:
                        todo.pop(0)()
                for filler in todo:
                    filler()
            yield
        last_cols = slice((n_groups - 1) * GROUP_W, n_groups * GROUP_W)
        last = _dot(mix_ref[rows_all, last_cols], w_out_ref[last_cols, :])
        for i, blk in enumerate(blocks):
            rows = pl.ds(blk * BLK, BLK)
            mixed = mixed_ref[rows, :] + last[i * BLK:(i + 1) * BLK]
            y_ref[blk] = x_ref[blk] + _rmsnorm(mixed, g_post_ref[...])

    n_runs = 2 if n_blk % 2 == 0 else 1
    per_run = n_blk // n_runs
    pipelines = [row_pipeline(list(range(i * per_run, (i + 1) * per_run))) for i in range(n_runs)]
    n_stages = n_groups + 2
    for step in range(n_stages + n_runs - 1):
        for lag, pipeline in enumerate(pipelines):
            if 0 <= step - lag < n_stages:
                next(pipeline, None)

    if fresh_state:
        @pl.when(pl.program_id(1) == pl.num_programs(1) - 1)
        def _():
            for blk in range(n_blk):
                for head in range(N_RET_HEADS):
                    s_out[blk, head] = st_ref[blk, head].T


def _ffn_kernel(*refs, seg, n_blk, fresh_state):
    n_seq_tile = n_blk * BLK // seg
    rows_tile = n_blk * BLK
    if fresh_state:
        (x_ref, g_pre_ref, w_up_ref, conv_w_ref, conv_b_ref, w_down_ref, g_post_ref,
         y_ref, buf_out, h_ref, act_ref, ffn_ref) = refs
        buf_in = buf_out

        @pl.when(pl.program_id(1) == 0)
        def _():
            buf_out[...] = jnp.zeros(buf_out.shape, F32)
    else:
        (x_ref, g_pre_ref, w_up_ref, conv_w_ref, conv_b_ref, w_down_ref, g_post_ref, buf_in,
         y_ref, buf_out, h_ref, act_ref, ffn_ref) = refs

    for blk in range(n_blk):
        rows = pl.ds(blk * BLK, BLK)
        h_ref[rows, :] = _rmsnorm(x_ref[blk], g_pre_ref[...]).astype(BF16)

    tpos = lax.broadcasted_iota(jnp.int32, (n_seq_tile, seg, FF_BLK), 1)

    def conv(cols):
        up = _dot(h_ref[...], w_up_ref[:, cols])
        prev0 = buf_in[:, 0:1, cols]
        prev1 = buf_in[:, 1:2, cols]
        up3 = up.reshape(n_seq_tile, seg, FF_BLK)
        sh1 = pltpu.roll(up, 1, axis=0).reshape(n_seq_tile, seg, FF_BLK)
        sh2 = pltpu.roll(up, 2, axis=0).reshape(n_seq_tile, seg, FF_BLK)
        sh1 = jnp.where(tpos == 0, prev1, sh1)
        sh2 = jnp.where(tpos == 0, prev0, jnp.where(tpos == 1, prev1, sh2))
        buf_out[:, 0:1, cols] = up3[:, seg - 2:seg - 1, :]
        buf_out[:, 1:2, cols] = up3[:, seg - 1:seg, :]
        w = conv_w_ref[:, cols]
        out = sh2 * w[0:1, :] + sh1 * w[1:2, :] + up3 * w[2:3, :] + conv_b_ref[:, cols]
        return out.reshape(rows_tile, FF_BLK)

    def activation(gate, val):
        c0 = math.sqrt(2.0 / math.pi)
        c1 = 0.044715 * c0
        t = jnp.tanh(gate * (gate * gate * c1 + c0))
        half = 0.5 * gate
        return ((half + half * t) * val).astype(BF16)

    for j in range(D_FF // FF_BLK):
        cols = slice(j * FF_BLK, (j + 1) * FF_BLK)
        gate = conv(cols)
        val = conv(slice(D_FF + j * FF_BLK, D_FF + (j + 1) * FF_BLK))
        act_ref[:, cols] = activation(gate, val)

    ffn_ref[...] = _dot(act_ref[...], w_down_ref[...])
    for blk in range(n_blk):
        rows = pl.ds(blk * BLK, BLK)
        y_ref[blk] = x_ref[blk] + _rmsnorm(ffn_ref[rows, :], g_post_ref[...])


def _const_spec(shape):
    zeros = (0,) * len(shape)
    return pl.BlockSpec(shape, lambda *_: zeros, pipeline_mode=pl.Buffered(1))


def _compiler_params(n_grid_dims):
    return pltpu.CompilerParams(
        dimension_semantics=("arbitrary",) * n_grid_dims,
        vmem_limit_bytes=VMEM_LIMIT_BYTES)


def _mixer_scratch(n_blk, fresh_state):
    rows = n_blk * BLK
    carried = [pltpu.VMEM((n_blk, N_RET_HEADS, HEAD_DIM, HEAD_DIM), F32)] if fresh_state else []
    return [
        pltpu.VMEM((rows, D_MODEL), BF16),
        pltpu.VMEM((2, rows, 4 * GROUP_W), F32),
        pltpu.VMEM((rows, RET_W + ML_W), BF16),
        pltpu.VMEM((rows, D_MODEL), F32),
        pltpu.VMEM((rows, BLK), F32),
        pltpu.VMEM((rows, BLK), F32),
        pltpu.VMEM((n_blk, BLK, BLK), F32),
        pltpu.VMEM((n_blk, BLK, BLK), F32),
    ] + carried


def _mixer_weight_specs():
    return [
        _const_spec((1, D_MODEL)),
        _const_spec((D_MODEL, 4 * RET_W + 4 * ML_W + 2 * N_ML_HEADS)),
        _const_spec((D_MODEL, BLK)),
        _const_spec((1, BLK)),
        _const_spec((1, RET_W)),
        _const_spec((1, ML_W)),
        _const_spec((RET_W + ML_W, D_MODEL)),
        _const_spec((1, D_MODEL)),
    ]


def _state_shapes(n_seqs, n_blocks):
    return [
        jax.ShapeDtypeStruct((n_seqs, N_RET_HEADS, HEAD_DIM, HEAD_DIM), F32),
        jax.ShapeDtypeStruct((n_seqs, N_ML_HEADS, HEAD_DIM, HEAD_DIM), F32),
        jax.ShapeDtypeStruct((n_seqs, N_ML_HEADS, HEAD_DIM), F32),
        jax.ShapeDtypeStruct((n_blocks, N_ML_HEADS, BLK), F32),
    ]


def _state_specs(n_seq_tile, n_blk, index):
    return [
        pl.BlockSpec((n_seq_tile, N_RET_HEADS, HEAD_DIM, HEAD_DIM), lambda *g: (index(*g), 0, 0, 0)),
        pl.BlockSpec((n_seq_tile, N_ML_HEADS, HEAD_DIM, HEAD_DIM), lambda *g: (index(*g), 0, 0, 0)),
        pl.BlockSpec((n_seq_tile, N_ML_HEADS, HEAD_DIM), lambda *g: (index(*g), 0, 0)),
        pl.BlockSpec((n_blk, N_ML_HEADS, BLK), lambda *g: (index(*g), 0, 0)),
    ]


def _prompt_mixer(x, cos, sin, weights, n_blk):
    batch, seq_len, _ = x.shape
    grid = (batch // n_blk, seq_len // BLK)
    x_spec = pl.BlockSpec((n_blk, BLK, D_MODEL), lambda g, c: (g, c, 0))
    rope_spec = pl.BlockSpec((BLK, HEAD_DIM), lambda g, c: (c, 0))
    return pl.pallas_call(
        functools.partial(_mixer_kernel, seg=BLK, n_blk=n_blk, fresh_state=True),
        grid=grid,
        in_specs=[x_spec, rope_spec, rope_spec] + _mixer_weight_specs(),
        out_specs=[x_spec] + _state_specs(n_blk, n_blk, lambda g, c: g),
        out_shape=[jax.ShapeDtypeStruct(x.shape, F32)] + _state_shapes(batch, batch),
        scratch_shapes=_mixer_scratch(n_blk, fresh_state=True),
        compiler_params=_compiler_params(2),
        name="prompt_mixer",
    )(x, cos, sin, *weights)


def _sample_mixer(x, cos, sin, weights, states, seg, n_blk):
    n_blocks = x.shape[0]
    n_seq_tile = n_blk * BLK // seg
    grid = (n_blocks // n_blk,)
    x_spec = pl.BlockSpec((n_blk, BLK, D_MODEL), lambda g: (g, 0, 0))
    rope_spec = _const_spec((BLK, HEAD_DIM))
    state_specs = _state_specs(n_seq_tile, n_blk, lambda g: g)
    return pl.pallas_call(
        functools.partial(_mixer_kernel, seg=seg, n_blk=n_blk, fresh_state=False),
        grid=grid,
        in_specs=[x_spec, rope_spec, rope_spec] + _mixer_weight_specs() + state_specs,
        out_specs=[x_spec] + state_specs,
        out_shape=[jax.ShapeDtypeStruct(x.shape, F32)]
        + _state_shapes(n_blocks * BLK // seg, n_blocks),
        scratch_shapes=_mixer_scratch(n_blk, fresh_state=False),
        compiler_params=_compiler_params(1),
        name="sample_mixer",
    )(x, cos, sin, *weights, *states)


def _ffn_scratch(n_blk):
    rows = n_blk * BLK
    return [
        pltpu.VMEM((rows, D_MODEL), BF16),
        pltpu.VMEM((rows, D_FF), BF16),
        pltpu.VMEM((rows, D_MODEL), F32),
    ]


def _ffn_weight_specs():
    return [
        _const_spec((1, D_MODEL)),
        _const_spec((D_MODEL, 2 * D_FF)),
        _const_spec((CONV_W, 2 * D_FF)),
        _const_spec((1, 2 * D_FF)),
        _const_spec((D_FF, D_MODEL)),
        _const_spec((1, D_MODEL)),
    ]


def _prompt_ffn(x, weights, n_blk):
    batch, seq_len, _ = x.shape
    grid = (batch // n_blk, seq_len // BLK)
    x_spec = pl.BlockSpec((n_blk, BLK, D_MODEL), lambda g, c: (g, c, 0))
    buf_spec = pl.BlockSpec((n_blk, CONV_W - 1, 2 * D_FF), lambda g, c: (g, 0, 0))
    return pl.pallas_call(
        functools.partial(_ffn_kernel, seg=BLK, n_blk=n_blk, fresh_state=True),
        grid=grid,
        in_specs=[x_spec] + _ffn_weight_specs(),
        out_specs=[x_spec, buf_spec],
        out_shape=[jax.ShapeDtypeStruct(x.shape, F32),
                   jax.ShapeDtypeStruct((batch, CONV_W - 1, 2 * D_FF), F32)],
        scratch_shapes=_ffn_scratch(n_blk),
        compiler_params=_compiler_params(2),
        name="prompt_ffn",
    )(x, *weights)


def _sample_ffn(x, weights, conv_buf, seg, n_blk):
    n_blocks = x.shape[0]
    n_seq_tile = n_blk * BLK // seg
    grid = (n_blocks // n_blk,)
    x_spec = pl.BlockSpec((n_blk, BLK, D_MODEL), lambda g: (g, 0, 0))
    buf_spec = pl.BlockSpec((n_seq_tile, CONV_W - 1, 2 * D_FF), lambda g: (g, 0, 0))
    return pl.pallas_call(
        functools.partial(_ffn_kernel, seg=seg, n_blk=n_blk, fresh_state=False),
        grid=grid,
        in_specs=[x_spec] + _ffn_weight_specs() + [buf_spec],
        out_specs=[x_spec, buf_spec],
        out_shape=[jax.ShapeDtypeStruct(x.shape, F32),
                   jax.ShapeDtypeStruct(conv_buf.shape, F32)],
        scratch_shapes=_ffn_scratch(n_blk),
        compiler_params=_compiler_params(1),
        name="sample_ffn",
    )(x, *weights, conv_buf)


def _rope_tables(pos):
    freqs = ROPE_BASE ** (-jnp.arange(0, HEAD_DIM, 2, dtype=F32) / HEAD_DIM)
    ang = pos.astype(F32)[:, None] * freqs[None, :]
    cos, sin = jnp.cos(ang), jnp.sin(ang)
    return jnp.concatenate([cos, cos], axis=-1), jnp.concatenate([-sin, sin], axis=-1)


def kernel(x_prompt, x_sample, state_ret, state_mlstm_C, state_mlstm_n, state_mlstm_m, cache_ffn_conv, pre_mix_gain, w_in, b_gates, ret_head_gain, mlstm_head_gain, w_out, post_mix_gain, pre_ffn_gain, w_up, conv_w, conv_b, w_down, post_ffn_gain):
    depth = w_in.shape[0]
    assert depth == 1
    batch, seq_len, _ = x_prompt.shape
    dec_batch, dec_seq, _ = x_sample.shape
    assert seq_len % BLK == 0 and BLK % dec_seq == 0 and (dec_batch * dec_seq) % BLK == 0
    seqs_per_blk = BLK // dec_seq
    n_sample_blocks = dec_batch * dec_seq // BLK
    layer = 0

    n_gate = 2 * N_ML_HEADS
    w_gate = jnp.pad(w_in[layer][:, -n_gate:], ((0, 0), (0, BLK - n_gate))).astype(BF16)
    b_gate = jnp.pad(b_gates[layer], (0, BLK - n_gate)).reshape(1, BLK)
    mixer_weights = (
        pre_mix_gain[layer].reshape(1, D_MODEL), w_in[layer].astype(BF16), w_gate, b_gate,
        ret_head_gain[layer].reshape(1, RET_W), mlstm_head_gain[layer].reshape(1, ML_W),
        w_out[layer].astype(BF16), post_mix_gain[layer].reshape(1, D_MODEL))
    ffn_weights = (
        pre_ffn_gain[layer].reshape(1, D_MODEL), w_up[layer].astype(BF16), conv_w[layer],
        conv_b[layer].reshape(1, 2 * D_FF), w_down[layer].astype(BF16),
        post_ffn_gain[layer].reshape(1, D_MODEL))

    cos_p, sin_p = _rope_tables(jnp.arange(seq_len, dtype=jnp.int32))
    x1_p, s_p, c_p, n_p, m_p = _prompt_mixer(x_prompt, cos_p, sin_p, mixer_weights, n_blk=8)
    y_p, buf_p = _prompt_ffn(x1_p, ffn_weights, n_blk=8)
    m_p = m_p[:, :, 0]

    pos_s = PAST_LEN + jnp.arange(dec_seq, dtype=jnp.int32)
    cos_s, sin_s = _rope_tables(jnp.tile(pos_s, seqs_per_blk))
    m_blocks = jnp.transpose(state_mlstm_m[layer].reshape(n_sample_blocks, seqs_per_blk, N_ML_HEADS), (0, 2, 1))
    m_blocks = jnp.pad(m_blocks, ((0, 0), (0, 0), (0, BLK - seqs_per_blk)))
    xs = x_sample.reshape(n_sample_blocks, BLK, D_MODEL)
    x1_s, s_s, c_s, n_s, m_s = _sample_mixer(
        xs, cos_s, sin_s, mixer_weights,
        (state_ret[layer], state_mlstm_C[layer], state_mlstm_n[layer], m_blocks),
        seg=dec_seq, n_blk=1)
    y_s, buf_s = _sample_ffn(x1_s, ffn_weights, cache_ffn_conv[layer], seg=dec_seq, n_blk=2)
    y_s = y_s.reshape(x_sample.shape)
    m_s = jnp.transpose(m_s[:, :, :seqs_per_blk], (0, 2, 1)).reshape(dec_batch, N_ML_HEADS)

    return (y_p, y_s, s_p[None], s_s[None], c_p[None], c_s[None], n_p[None], n_s[None],
            m_p[None], m_s[None], buf_p[None], buf_s[None])
```

```python
import functools
import math

import jax
import jax.numpy as jnp
from jax import lax
from jax.experimental import pallas as pl
from jax.experimental.pallas import tpu as pltpu

D_MODEL = 1024
HEAD_DIM = 128
N_RET_HEADS = 4
N_ML_HEADS = 4
RET_W = N_RET_HEADS * HEAD_DIM
ML_W = N_ML_HEADS * HEAD_DIM
D_FF = 2816
CONV_W = 3
PAST_LEN = 16384
ROPE_BASE = 10000.0
EPS = 1e-6
M_INIT = -1e30

BLK = 128
HEADS_PER_GROUP = 2
GROUP_W = HEADS_PER_GROUP * HEAD_DIM
FF_BLK = 256
SUBLANES = 8
VMEM_LIMIT_BYTES = 56 * 1024 * 1024

F32 = jnp.float32
BF16 = jnp.bfloat16


def _dot(a, b, precision=None):
    return jnp.dot(a, b, preferred_element_type=F32, precision=precision)


def _dot_nt(a, b):
    return lax.dot_general(a, b, (((1,), (1,)), ((), ())), preferred_element_type=F32)


def _rmsnorm(x, g):
    return x * lax.rsqrt(jnp.mean(x * x, axis=-1, keepdims=True) + EPS) * g


def _groupnorm(h, g):
    mu = jnp.mean(h, axis=-1, keepdims=True)
    d = h - mu
    var = jnp.mean(d * d, axis=-1, keepdims=True)
    return d * lax.rsqrt(var + EPS) * g


def _ret_log_gamma(h):
    return math.log(1.0 - 2.0 ** (-5.0 - h))


class _Masks:
    def __init__(self, seg):
        shift = seg.bit_length() - 1
        r = lax.broadcasted_iota(jnp.int32, (BLK, BLK), 0)
        c = lax.broadcasted_iota(jnp.int32, (BLK, BLK), 1)
        rseq = r >> shift
        self.causal = (rseq == (c >> shift)) & (r >= c)
        self.tpos = (r & (seg - 1)).astype(F32)
        self.diff = (r - c).astype(F32)
        self.col_is_last_of_rowseq = c == (rseq * seg + (seg - 1))
        self.col_is_rowseq = c == rseq
        self.row_is_last_of_colseq = r == (c * seg + (seg - 1))


def _rows_from_seq(mask_col_is_rowseq, seq_row):
    return jnp.sum(jnp.where(mask_col_is_rowseq, seq_row, 0.0), axis=1, keepdims=True)


def _seq_from_last_row(mask_row_is_last_of_colseq, col):
    return jnp.sum(jnp.where(mask_row_is_last_of_colseq, col, 0.0), axis=0, keepdims=True)


def _mixer_kernel(*refs, seg, n_blk, fresh_state):
    n_seq = BLK // seg
    assert fresh_state == (n_seq == 1)
    if fresh_state:
        (x_ref, cos_ref, sin_ref, g_pre_ref, w_in_ref, w_gate_ref, b_gate_ref, g_ret_ref, g_ml_ref,
         w_out_ref, g_post_ref,
         y_ref, s_out, c_out, n_out, m_out,
         h_ref, proj_ref, mix_ref, mixed_ref, gate_ref, bcum_ref, gate_t_ref, bcum_t_ref, st_ref) = refs
        c_in, n_in, m_in = c_out, n_out, m_out

        @pl.when(pl.program_id(1) == 0)
        def _():
            st_ref[...] = jnp.zeros(st_ref.shape, F32)
            c_out[...] = jnp.zeros(c_out.shape, F32)
            n_out[...] = jnp.zeros(n_out.shape, F32)
            m_out[...] = jnp.full(m_out.shape, M_INIT, F32)
    else:
        (x_ref, cos_ref, sin_ref, g_pre_ref, w_in_ref, w_gate_ref, b_gate_ref, g_ret_ref, g_ml_ref,
         w_out_ref, g_post_ref, s_in, c_in, n_in, m_in,
         y_ref, s_out, c_out, n_out, m_out,
         h_ref, proj_ref, mix_ref, mixed_ref, gate_ref, bcum_ref, gate_t_ref, bcum_t_ref) = refs

    masks = _Masks(seg)
    scale = HEAD_DIM ** -0.5
    tri = jnp.where(masks.causal, 1.0, 0.0).astype(F32)
    cos = cos_ref[...]
    sin = sin_ref[...]

    def prologue(blocks, rows_all):
        for blk in blocks:
            rows = pl.ds(blk * BLK, BLK)
            h_ref[rows, :] = _rmsnorm(x_ref[blk], g_pre_ref[...]).astype(BF16)
        gate_ref[rows_all, :] = _dot(h_ref[rows_all, :], w_gate_ref[...]) + b_gate_ref[...]
        for blk in blocks:
            rows = pl.ds(blk * BLK, BLK)
            gates = gate_ref[rows, :]
            bcum = _dot(tri, jax.nn.log_sigmoid(gates), precision=lax.Precision.HIGHEST)
            bcum_ref[rows, :] = bcum
            gate_t_ref[blk] = gates.T
            bcum_t_ref[blk] = bcum.T

    def rotary(t):
        return t * cos + pltpu.roll(t, HEAD_DIM // 2, axis=1) * sin

    def seq_rows(j):
        return slice(j * seg, (j + 1) * seg)

    def lane_mask(j):
        c = lax.broadcasted_iota(jnp.int32, (BLK, BLK), 1)
        return (c >= j * seg) & (c < (j + 1) * seg)

    def ret_head(proj, blk, head, col):
        rows = pl.ds(blk * BLK, BLK)
        lg = _ret_log_gamma(head)
        q = rotary(proj[rows,col:col + HEAD_DIM])
        k = rotary(proj[rows,GROUP_W + col:GROUP_W + col + HEAD_DIM]) * scale
        v = proj[rows,2 * GROUP_W + col:2 * GROUP_W + col + HEAD_DIM]
        g = proj[rows,3 * GROUP_W + col:3 * GROUP_W + col + HEAD_DIM]
        qb, kb, vb = q.astype(BF16), k.astype(BF16), v.astype(BF16)
        dmat = jnp.where(masks.causal, jnp.exp(jnp.where(masks.causal, masks.diff * lg, 0.0)), 0.0)
        xi = jnp.exp((masks.tpos + 1.0) * lg)
        zeta = jnp.exp((float(seg - 1) - masks.tpos) * lg)
        chunk_decay = math.exp(seg * lg)
        scores = _dot_nt(qb, kb)
        kz_t = (k * zeta).T
        inter = []
        kz_tb = kz_t.astype(BF16)
        updates = _dot(jnp.concatenate(
            [jnp.where(lane_mask(j), kz_tb, jnp.zeros_like(kz_tb)) for j in range(n_seq)], axis=0), vb)
        for j in range(n_seq):
            seq = blk * n_seq + j
            s_prev = s_in[seq, head]
            inter.append(_dot(q[seq_rows(j)].astype(BF16), s_prev.astype(BF16)))
            s_out[seq, head] = s_prev * chunk_decay + updates[j * HEAD_DIM:(j + 1) * HEAD_DIM]
        inter = jnp.concatenate(inter, axis=0)
        yield
        out = _dot((scores * dmat).astype(BF16), vb)
        yield
        out = out + inter * xi
        y = _groupnorm(out, g_ret_ref[:, head * HEAD_DIM:(head + 1) * HEAD_DIM]) * (g * jax.nn.sigmoid(g))
        mix_ref[rows, head * HEAD_DIM:(head + 1) * HEAD_DIM] = y.astype(BF16)

    def ml_head(proj, blk, head, col):
        rows = pl.ds(blk * BLK, BLK)
        q = proj[rows,col:col + HEAD_DIM]
        k = proj[rows,GROUP_W + col:GROUP_W + col + HEAD_DIM] * scale
        v = proj[rows,2 * GROUP_W + col:2 * GROUP_W + col + HEAD_DIM]
        o = proj[rows,3 * GROUP_W + col:3 * GROUP_W + col + HEAD_DIM]
        qb, kb, vb = q.astype(BF16), k.astype(BF16), v.astype(BF16)
        i_col = gate_ref[rows, head:head + 1]
        b_col = bcum_ref[rows, N_ML_HEADS + head:N_ML_HEADS + head + 1]
        i_row = gate_t_ref[blk, head:head + 1, :]
        b_row = bcum_t_ref[blk, N_ML_HEADS + head:N_ML_HEADS + head + 1, :]
        m_prev_seq = m_in[blk, head:head + 1, :]
        m_prev = _rows_from_seq(masks.col_is_rowseq, m_prev_seq)
        logw = jnp.where(masks.causal, b_col - b_row + i_row, -jnp.inf)
        m_t = jnp.maximum(b_col + m_prev, jnp.max(logw, axis=1, keepdims=True))
        w = jnp.exp(logw - m_t)
        inter_w = jnp.exp(b_col + m_prev - m_t)
        s = _dot_nt(qb, kb)
        m_new_seq = _seq_from_last_row(masks.row_is_last_of_colseq, m_t)
        b_last_seq = _seq_from_last_row(masks.row_is_last_of_colseq, b_col)
        decay_seq = jnp.exp(b_last_seq + m_prev_seq - m_new_seq)
        m_new = _rows_from_seq(masks.col_is_rowseq, m_new_seq)
        b_last = jnp.sum(jnp.where(masks.col_is_last_of_rowseq, b_row, 0.0), axis=1, keepdims=True)
        wl = jnp.exp(b_last - b_col + i_col - m_new)
        vw_t = (v * wl).T
        kw = k * wl
        qc, qn = [], []
        for j in range(n_seq):
            seq = blk * n_seq + j
            c_prev = c_in[seq, head]
            n_prev = n_in[seq, head:head + 1, :]
            qc.append(_dot_nt(q[seq_rows(j)].astype(BF16), c_prev.astype(BF16)))
            qn.append(jnp.sum(q[seq_rows(j)] * n_prev, axis=1, keepdims=True))
            decay = decay_seq[:, j:j + 1]
            n_out[seq, head:head + 1, :] = decay * n_prev + jnp.sum(kw[seq_rows(j)], axis=0, keepdims=True)
        qc = jnp.concatenate(qc, axis=0)
        qn = jnp.concatenate(qn, axis=0)
        m_out[blk, head:head + 1, :] = m_new_seq
        yield
        s = s * w
        num = _dot(s.astype(BF16), vb)
        den = jnp.sum(s, axis=1, keepdims=True)
        vw_tb = vw_t.astype(BF16)
        updates = _dot(jnp.concatenate(
            [jnp.where(lane_mask(j), vw_tb, jnp.zeros_like(vw_tb)) for j in range(n_seq)], axis=0), kb)
        for j in range(n_seq):
            seq = blk * n_seq + j
            c_out[seq, head] = (decay_seq[:, j:j + 1] * c_in[seq, head]
                                + updates[j * HEAD_DIM:(j + 1) * HEAD_DIM])
        yield
        num = num + inter_w * qc
        den = jnp.maximum(jnp.abs(den + inter_w * qn), jnp.exp(-m_t))
        hout = num / den
        y = _groupnorm(hout, g_ml_ref[:, head * HEAD_DIM:(head + 1) * HEAD_DIM]) * jax.nn.sigmoid(o)
        mix_ref[rows, RET_W + head * HEAD_DIM:RET_W + (head + 1) * HEAD_DIM] = y.astype(BF16)

    r_idx = lax.broadcasted_iota(jnp.int32, (BLK, BLK), 0)
    c_idx = lax.broadcasted_iota(jnp.int32, (BLK, BLK), 1)
    causal_t = r_idx <= c_idx
    lane_pos = lax.broadcasted_iota(jnp.int32, (1, BLK), 1).astype(F32)
    head_consts = {}

    def ret_consts(head):
        if head not in head_consts:
            lg = _ret_log_gamma(head)
            dmat_t = jnp.where(causal_t, jnp.exp(jnp.where(causal_t, (c_idx - r_idx).astype(F32) * lg, 0.0)), 0.0)
            xi_row = jnp.exp((lane_pos + 1.0) * lg)
            zeta = jnp.exp((float(BLK - 1) - r_idx.astype(F32)) * lg)
            head_consts[head] = (dmat_t, xi_row, zeta, math.exp(BLK * lg))
        return head_consts[head]

    def channel_norm_t(h_t):
        mu = jnp.mean(h_t, axis=0, keepdims=True)
        d = h_t - mu
        var = jnp.mean(d * d, axis=0, keepdims=True)
        return d * lax.rsqrt(var + EPS)

    def ret_head_single(proj, blk, head, col):
        rows = pl.ds(blk * BLK, BLK)
        dmat_t, xi_row, zeta, chunk_decay = ret_consts(head)
        q = rotary(proj[rows, col:col + HEAD_DIM])
        k = rotary(proj[rows, GROUP_W + col:GROUP_W + col + HEAD_DIM]) * scale
        v_t = proj[rows, 2 * GROUP_W + col:2 * GROUP_W + col + HEAD_DIM].T
        qb, kb, v_tb = q.astype(BF16), k.astype(BF16), v_t.astype(BF16)
        scores_t = _dot_nt(kb, qb)
        inter_t = _dot_nt(st_ref[blk, head].astype(BF16), qb)
        kz_b = (k * zeta).astype(BF16)
        yield
        both = _dot(v_tb, jnp.concatenate([kz_b, (scores_t * dmat_t).astype(BF16)], axis=1))
        st_ref[blk, head] = st_ref[blk, head] * chunk_decay + both[:, :HEAD_DIM]
        out_t = both[:, HEAD_DIM:]
        yield
        out_t = out_t + inter_t * xi_row
        g = proj[rows, 3 * GROUP_W + col:3 * GROUP_W + col + HEAD_DIM]
        y = channel_norm_t(out_t).T * g_ret_ref[:, head * HEAD_DIM:(head + 1) * HEAD_DIM] * (g * jax.nn.sigmoid(g))
        mix_ref[rows, head * HEAD_DIM:(head + 1) * HEAD_DIM] = y.astype(BF16)

    def ml_head_single(proj, blk, head, col):
        rows = pl.ds(blk * BLK, BLK)
        q = proj[rows, col:col + HEAD_DIM]
        k = proj[rows, GROUP_W + col:GROUP_W + col + HEAD_DIM] * scale
        v_t = proj[rows, 2 * GROUP_W + col:2 * GROUP_W + col + HEAD_DIM].T
        qb, kb = q.astype(BF16), k.astype(BF16)
        s_t = _dot_nt(kb, qb)
        c_prev = c_in[blk, head]
        n_prev = n_in[blk, head:head + 1, :]
        qc_t = _dot_nt(c_prev.astype(BF16), qb)
        qn = _dot_nt(jnp.broadcast_to(n_prev, (2 * SUBLANES, HEAD_DIM)).astype(BF16), qb)[0:1, :]
        i_row = gate_t_ref[blk, head:head + 1, :]
        b_row = bcum_t_ref[blk, N_ML_HEADS + head:N_ML_HEADS + head + 1, :]
        key_term = gate_ref[rows, head:head + 1] - bcum_ref[rows, N_ML_HEADS + head:N_ML_HEADS + head + 1]
        m_prev = m_in[blk, head:head + 1, 0:1]
        logw_t = jnp.where(causal_t, b_row + key_term, -jnp.inf)
        m_t = jnp.maximum(b_row + m_prev, jnp.max(logw_t, axis=0, keepdims=True))
        w_t = jnp.exp(logw_t - m_t)
        inter_w = jnp.exp(b_row + m_prev - m_t)
        m_new = m_t[:, BLK - 1:BLK]
        b_last = b_row[:, BLK - 1:BLK]
        decay = jnp.exp(b_last + m_prev - m_new)
        wl = jnp.exp(b_last - b_row + i_row - m_new)
        m_out[blk, head:head + 1, :] = jnp.broadcast_to(m_new, (1, BLK))
        wk = _dot(jnp.broadcast_to(wl, (2 * SUBLANES, BLK)).astype(BF16), kb)[0:1, :]
        n_out[blk, head:head + 1, :] = decay * n_prev + wk
        kw_b = (k * jnp.exp(key_term + (b_last - m_new))).astype(BF16)
        v_tb = v_t.astype(BF16)
        yield
        s_t = s_t * w_t
        den = jnp.sum(s_t, axis=0, keepdims=True)
        both = _dot(v_tb, jnp.concatenate([kw_b, s_t.astype(BF16)], axis=1))
        c_out[blk, head] = decay * c_in[blk, head] + both[:, :HEAD_DIM]
        num_t = both[:, HEAD_DIM:]
        yield
        num_t = num_t + inter_w * qc_t
        den = jnp.maximum(jnp.abs(den + inter_w * qn), jnp.exp(-m_t))
        o = proj[rows, 3 * GROUP_W + col:3 * GROUP_W + col + HEAD_DIM]
        y = (channel_norm_t(num_t / den).T * g_ml_ref[:, head * HEAD_DIM:(head + 1) * HEAD_DIM]
             * jax.nn.sigmoid(o))
        mix_ref[rows, RET_W + head * HEAD_DIM:RET_W + (head + 1) * HEAD_DIM] = y.astype(BF16)

    n_ret_groups = N_RET_HEADS // HEADS_PER_GROUP
    n_groups = n_ret_groups + N_ML_HEADS // HEADS_PER_GROUP

    def first_head_of(grp):
        return (grp if grp < n_ret_groups else grp - n_ret_groups) * HEADS_PER_GROUP

    def project(grp, kind, rows_all):
        is_ret = grp < n_ret_groups
        c0 = (0 if is_ret else 4 * RET_W) + kind * (RET_W if is_ret else ML_W) + first_head_of(grp) * HEAD_DIM
        proj_ref[grp % 2, rows_all, kind * GROUP_W:(kind + 1) * GROUP_W] = _dot(
            h_ref[rows_all, :], w_in_ref[:, c0:c0 + GROUP_W])

    def out_project(grp, rows_all):
        cols = slice(grp * GROUP_W, (grp + 1) * GROUP_W)
        part = _dot(mix_ref[rows_all, cols], w_out_ref[cols, :])
        if grp == 0:
            mixed_ref[rows_all, :] = part
        else:
            mixed_ref[rows_all, :] += part

    def row_pipeline(blocks):
        rows_all = pl.ds(blocks[0] * BLK, len(blocks) * BLK)
        prologue(blocks, rows_all)
        for kind in range(4):
            project(0, kind, rows_all)
        yield
        for grp in range(n_groups):
            if n_seq == 1:
                head_fn = ret_head_single if grp < n_ret_groups else ml_head_single
            else:
                head_fn = ret_head if grp < n_ret_groups else ml_head
            heads = [head_fn(proj_ref.at[grp % 2], blk, first_head_of(grp) + hh, hh * HEAD_DIM)
                     for blk in blocks for hh in range(HEADS_PER_GROUP)]
            fillers = {0: [functools.partial(project, grp + 1, kind, rows_all) for kind in range(4)]
                       if grp + 1 < n_groups else [],
                       1: [functools.partial(out_project, grp - 1, rows_all)] if grp >= 1 else [],
                       2: []}
            for phase in range(3):
                todo = list(fillers[phase])
                every = max(1, len(heads) // max(1, len(todo)))
                for i, head in enumerate(heads):
                    next(head, None)
                    if todo and (i + 1) % every == 0:
                        todo.pop(0)()
                for filler in todo:
                    filler()
            yield
        last_cols = slice((n_groups - 1) * GROUP_W, n_groups * GROUP_W)
        last = _dot(mix_ref[rows_all, last_cols], w_out_ref[last_cols, :])
        for i, blk in enumerate(blocks):
            rows = pl.ds(blk * BLK, BLK)
            mixed = mixed_ref[rows, :] + last[i * BLK:(i + 1) * BLK]
            y_ref[blk] = x_ref[blk] + _rmsnorm(mixed, g_post_ref[...])

    n_runs = 2 if n_blk % 2 == 0 else 1
    per_run = n_blk // n_runs
    pipelines = [row_pipeline(list(range(i * per_run, (i + 1) * per_run))) for i in range(n_runs)]
    n_stages = n_groups + 2
    for step in range(n_stages + n_runs - 1):
        for lag, pipeline in enumerate(pipelines):
            if 0 <= step - lag < n_stages:
                next(pipeline, None)

    if fresh_state:
        @pl.when(pl.program_id(1) == pl.num_programs(1) - 1)
        def _():
            for blk in range(n_blk):
                for head in range(N_RET_HEADS):
                    s_out[blk, head] = st_ref[blk, head].T


def _ffn_kernel(*refs, seg, n_blk, fresh_state):
    n_seq_tile = n_blk * BLK // seg
    rows_tile = n_blk * BLK
    if fresh_state:
        (x_ref, g_pre_ref, w_up_ref, conv_w_ref, conv_b_ref, w_down_ref, g_post_ref,
         y_ref, buf_out, h_ref, act_ref, ffn_ref) = refs
        buf_in = buf_out

        @pl.when(pl.program_id(1) == 0)
        def _():
            buf_out[...] = jnp.zeros(buf_out.shape, F32)
    else:
        (x_ref, g_pre_ref, w_up_ref, conv_w_ref, conv_b_ref, w_down_ref, g_post_ref, buf_in,
         y_ref, buf_out, h_ref, act_ref, ffn_ref) = refs

    for blk in range(n_blk):
        rows = pl.ds(blk * BLK, BLK)
        h_ref[rows, :] = _rmsnorm(x_ref[blk], g_pre_ref[...]).astype(BF16)

    tpos = lax.broadcasted_iota(jnp.int32, (n_seq_tile, seg, FF_BLK), 1)

    def conv(cols):
        up = _dot(h_ref[...], w_up_ref[:, cols])
        prev0 = buf_in[:, 0:1, cols]
        prev1 = buf_in[:, 1:2, cols]
        up3 = up.reshape(n_seq_tile, seg, FF_BLK)
        sh1 = pltpu.roll(up, 1, axis=0).reshape(n_seq_tile, seg, FF_BLK)
        sh2 = pltpu.roll(up, 2, axis=0).reshape(n_seq_tile, seg, FF_BLK)
        sh1 = jnp.where(tpos == 0, prev1, sh1)
        sh2 = jnp.where(tpos == 0, prev0, jnp.where(tpos == 1, prev1, sh2))
        buf_out[:, 0:1, cols] = up3[:, seg - 2:seg - 1, :]
        buf_out[:, 1:2, cols] = up3[:, seg - 1:seg, :]
        w = conv_w_ref[:, cols]
        out = sh2 * w[0:1, :] + sh1 * w[1:2, :] + up3 * w[2:3, :] + conv_b_ref[:, cols]
        return out.reshape(rows_tile, FF_BLK)

    def activation(gate, val):
        c0 = -2.0 * math.log2(math.e) * math.sqrt(2.0 / math.pi)
        c1 = 0.044715 * c0
        return ((gate * val) / (1.0 + jnp.exp2(gate * (gate * gate * c1 + c0)))).astype(BF16)

    for j in range(D_FF // FF_BLK):
        cols = slice(j * FF_BLK, (j + 1) * FF_BLK)
        gate = conv(cols)
        val = conv(slice(D_FF + j * FF_BLK, D_FF + (j + 1) * FF_BLK))
        act_ref[:, cols] = activation(gate, val)

    ffn_ref[...] = _dot(act_ref[...], w_down_ref[...])
    for blk in range(n_blk):
        rows = pl.ds(blk * BLK, BLK)
        y_ref[blk] = x_ref[blk] + _rmsnorm(ffn_ref[rows, :], g_post_ref[...])


def _const_spec(shape):
    zeros = (0,) * len(shape)
    return pl.BlockSpec(shape, lambda *_: zeros, pipeline_mode=pl.Buffered(1))


def _compiler_params(n_grid_dims):
    return pltpu.CompilerParams(
        dimension_semantics=("arbitrary",) * n_grid_dims,
        vmem_limit_bytes=VMEM_LIMIT_BYTES)


def _mixer_scratch(n_blk, fresh_state):
    rows = n_blk * BLK
    carried = [pltpu.VMEM((n_blk, N_RET_HEADS, HEAD_DIM, HEAD_DIM), F32)] if fresh_state else []
    return [
        pltpu.VMEM((rows, D_MODEL), BF16),
        pltpu.VMEM((2, rows, 4 * GROUP_W), F32),
        pltpu.VMEM((rows, RET_W + ML_W), BF16),
        pltpu.VMEM((rows, D_MODEL), F32),
        pltpu.VMEM((rows, BLK), F32),
        pltpu.VMEM((rows, BLK), F32),
        pltpu.VMEM((n_blk, BLK, BLK), F32),
        pltpu.VMEM((n_blk, BLK, BLK), F32),
    ] + carried


def _mixer_weight_specs():
    return [
        _const_spec((1, D_MODEL)),
        _const_spec((D_MODEL, 4 * RET_W + 4 * ML_W + 2 * N_ML_HEADS)),
        _const_spec((D_MODEL, BLK)),
        _const_spec((1, BLK)),
        _const_spec((1, RET_W)),
        _const_spec((1, ML_W)),
        _const_spec((RET_W + ML_W, D_MODEL)),
        _const_spec((1, D_MODEL)),
    ]


def _state_shapes(n_seqs, n_blocks):
    return [
        jax.ShapeDtypeStruct((n_seqs, N_RET_HEADS, HEAD_DIM, HEAD_DIM), F32),
        jax.ShapeDtypeStruct((n_seqs, N_ML_HEADS, HEAD_DIM, HEAD_DIM), F32),
        jax.ShapeDtypeStruct((n_seqs, N_ML_HEADS, HEAD_DIM), F32),
        jax.ShapeDtypeStruct((n_blocks, N_ML_HEADS, BLK), F32),
    ]


def _state_specs(n_seq_tile, n_blk, index):
    return [
        pl.BlockSpec((n_seq_tile, N_RET_HEADS, HEAD_DIM, HEAD_DIM), lambda *g: (index(*g), 0, 0, 0)),
        pl.BlockSpec((n_seq_tile, N_ML_HEADS, HEAD_DIM, HEAD_DIM), lambda *g: (index(*g), 0, 0, 0)),
        pl.BlockSpec((n_seq_tile, N_ML_HEADS, HEAD_DIM), lambda *g: (index(*g), 0, 0)),
        pl.BlockSpec((n_blk, N_ML_HEADS, BLK), lambda *g: (index(*g), 0, 0)),
    ]


def _prompt_mixer(x, cos, sin, weights, n_blk):
    batch, seq_len, _ = x.shape
    grid = (batch // n_blk, seq_len // BLK)
    x_spec = pl.BlockSpec((n_blk, BLK, D_MODEL), lambda g, c: (g, c, 0))
    rope_spec = pl.BlockSpec((BLK, HEAD_DIM), lambda g, c: (c, 0))
    return pl.pallas_call(
        functools.partial(_mixer_kernel, seg=BLK, n_blk=n_blk, fresh_state=True),
        grid=grid,
        in_specs=[x_spec, rope_spec, rope_spec] + _mixer_weight_specs(),
        out_specs=[x_spec] + _state_specs(n_blk, n_blk, lambda g, c: g),
        out_shape=[jax.ShapeDtypeStruct(x.shape, F32)] + _state_shapes(batch, batch),
        scratch_shapes=_mixer_scratch(n_blk, fresh_state=True),
        compiler_params=_compiler_params(2),
        name="prompt_mixer",
    )(x, cos, sin, *weights)


def _sample_mixer(x, cos, sin, weights, states, seg, n_blk):
    n_blocks = x.shape[0]
    n_seq_tile = n_blk * BLK // seg
    grid = (n_blocks // n_blk,)
    x_spec = pl.BlockSpec((n_blk, BLK, D_MODEL), lambda g: (g, 0, 0))
    rope_spec = _const_spec((BLK, HEAD_DIM))
    state_specs = _state_specs(n_seq_tile, n_blk, lambda g: g)
    return pl.pallas_call(
        functools.partial(_mixer_kernel, seg=seg, n_blk=n_blk, fresh_state=False),
        grid=grid,
        in_specs=[x_spec, rope_spec, rope_spec] + _mixer_weight_specs() + state_specs,
        out_specs=[x_spec] + state_specs,
        out_shape=[jax.ShapeDtypeStruct(x.shape, F32)]
        + _state_shapes(n_blocks * BLK // seg, n_blocks),
        scratch_shapes=_mixer_scratch(n_blk, fresh_state=False),
        compiler_params=_compiler_params(1),
        name="sample_mixer",
    )(x, cos, sin, *weights, *states)


def _ffn_scratch(n_blk):
    rows = n_blk * BLK
    return [
        pltpu.VMEM((rows, D_MODEL), BF16),
        pltpu.VMEM((rows, D_FF), BF16),
        pltpu.VMEM((rows, D_MODEL), F32),
    ]


def _ffn_weight_specs():
    return [
        _const_spec((1, D_MODEL)),
        _const_spec((D_MODEL, 2 * D_FF)),
        _const_spec((CONV_W, 2 * D_FF)),
        _const_spec((1, 2 * D_FF)),
        _const_spec((D_FF, D_MODEL)),
        _const_spec((1, D_MODEL)),
    ]


def _prompt_ffn(x, weights, n_blk):
    batch, seq_len, _ = x.shape
    grid = (batch // n_blk, seq_len // BLK)
    x_spec = pl.BlockSpec((n_blk, BLK, D_MODEL), lambda g, c: (g, c, 0))
    buf_spec = pl.BlockSpec((n_blk, CONV_W - 1, 2 * D_FF), lambda g, c: (g, 0, 0))
    return pl.pallas_call(
        functools.partial(_ffn_kernel, seg=BLK, n_blk=n_blk, fresh_state=True),
        grid=grid,
        in_specs=[x_spec] + _ffn_weight_specs(),
        out_specs=[x_spec, buf_spec],
        out_shape=[jax.ShapeDtypeStruct(x.shape, F32),
                   jax.ShapeDtypeStruct((batch, CONV_W - 1, 2 * D_FF), F32)],
        scratch_shapes=_ffn_scratch(n_blk),
        compiler_params=_compiler_params(2),
        name="prompt_ffn",
    )(x, *weights)


def _sample_ffn(x, weights, conv_buf, seg, n_blk):
    n_blocks = x.shape[0]
    n_seq_tile = n_blk * BLK // seg
    grid = (n_blocks // n_blk,)
    x_spec = pl.BlockSpec((n_blk, BLK, D_MODEL), lambda g: (g, 0, 0))
    buf_spec = pl.BlockSpec((n_seq_tile, CONV_W - 1, 2 * D_FF), lambda g: (g, 0, 0))
    return pl.pallas_call(
        functools.partial(_ffn_kernel, seg=seg, n_blk=n_blk, fresh_state=False),
        grid=grid,
        in_specs=[x_spec] + _ffn_weight_specs() + [buf_spec],
        out_specs=[x_spec, buf_spec],
        out_shape=[jax.ShapeDtypeStruct(x.shape, F32),
                   jax.ShapeDtypeStruct(conv_buf.shape, F32)],
        scratch_shapes=_ffn_scratch(n_blk),
        compiler_params=_compiler_params(1),
        name="sample_ffn",
    )(x, *weights, conv_buf)


def _rope_tables(pos):
    freqs = ROPE_BASE ** (-jnp.arange(0, HEAD_DIM, 2, dtype=F32) / HEAD_DIM)
    ang = pos.astype(F32)[:, None] * freqs[None, :]
    cos, sin = jnp.cos(ang), jnp.sin(ang)
    return jnp.concatenate([cos, cos], axis=-1), jnp.concatenate([-sin, sin], axis=-1)


def kernel(x_prompt, x_sample, state_ret, state_mlstm_C, state_mlstm_n, state_mlstm_m, cache_ffn_conv, pre_mix_gain, w_in, b_gates, ret_head_gain, mlstm_head_gain, w_out, post_mix_gain, pre_ffn_gain, w_up, conv_w, conv_b, w_down, post_ffn_gain):
    depth = w_in.shape[0]
    assert depth == 1
    batch, seq_len, _ = x_prompt.shape
    dec_batch, dec_seq, _ = x_sample.shape
    assert seq_len % BLK == 0 and BLK % dec_seq == 0 and (dec_batch * dec_seq) % BLK == 0
    seqs_per_blk = BLK // dec_seq
    n_sample_blocks = dec_batch * dec_seq // BLK
    layer = 0

    n_gate = 2 * N_ML_HEADS
    w_gate = jnp.pad(w_in[layer][:, -n_gate:], ((0, 0), (0, BLK - n_gate))).astype(BF16)
    b_gate = jnp.pad(b_gates[layer], (0, BLK - n_gate)).reshape(1, BLK)
    mixer_weights = (
        pre_mix_gain[layer].reshape(1, D_MODEL), w_in[layer].astype(BF16), w_gate, b_gate,
        ret_head_gain[layer].reshape(1, RET_W), mlstm_head_gain[layer].reshape(1, ML_W),
        w_out[layer].astype(BF16), post_mix_gain[layer].reshape(1, D_MODEL))
    ffn_weights = (
        pre_ffn_gain[layer].reshape(1, D_MODEL), w_up[layer].astype(BF16), conv_w[layer],
        conv_b[layer].reshape(1, 2 * D_FF), w_down[layer].astype(BF16),
        post_ffn_gain[layer].reshape(1, D_MODEL))

    cos_p, sin_p = _rope_tables(jnp.arange(seq_len, dtype=jnp.int32))
    x1_p, s_p, c_p, n_p, m_p = _prompt_mixer(x_prompt, cos_p, sin_p, mixer_weights, n_blk=8)
    y_p, buf_p = _prompt_ffn(x1_p, ffn_weights, n_blk=8)
    m_p = m_p[:, :, 0]

    pos_s = PAST_LEN + jnp.arange(dec_seq, dtype=jnp.int32)
    cos_s, sin_s = _rope_tables(jnp.tile(pos_s, seqs_per_blk))
    m_blocks = jnp.transpose(state_mlstm_m[layer].reshape(n_sample_blocks, seqs_per_blk, N_ML_HEADS), (0, 2, 1))
    m_blocks = jnp.pad(m_blocks, ((0, 0), (0, 0), (0, BLK - seqs_per_blk)))
    xs = x_sample.reshape(n_sample_blocks, BLK, D_MODEL)
    x1_s, s_s, c_s, n_s, m_s = _sample_mixer(
        xs, cos_s, sin_s, mixer_weights,
        (state_ret[layer], state_mlstm_C[layer], state_mlstm_n[layer], m_blocks),
        seg=dec_seq, n_blk=1)
    y_s, buf_s = _sample_ffn(x1_s, ffn_weights, cache_ffn_conv[layer], seg=dec_seq, n_blk=2)
    y_s = y_s.reshape(x_sample.shape)
    m_s = jnp.transpose(m_s[:, :, :seqs_per_blk], (0, 2, 1)).reshape(dec_batch, N_ML_HEADS)

    return (y_p, y_s, s_p[None], s_s[None], c_p[None], c_s[None], n_p[None], n_s[None],
            m_p[None], m_s[None], buf_p[None], buf_s[None])
```

```python
import functools
import math

import jax
import jax.numpy as jnp
from jax import lax
from jax.experimental import pallas as pl
from jax.experimental.pallas import tpu as pltpu

D_MODEL = 1024
HEAD_DIM = 128
N_RET_HEADS = 4
N_ML_HEADS = 4
RET_W = N_RET_HEADS * HEAD_DIM
ML_W = N_ML_HEADS * HEAD_DIM
D_FF = 2816
CONV_W = 3
PAST_LEN = 16384
ROPE_BASE = 10000.0
EPS = 1e-6
M_INIT = -1e30

BLK = 128
HEADS_PER_GROUP = 2
GROUP_W = HEADS_PER_GROUP * HEAD_DIM
FF_BLK = 256
SUBLANES = 8
VMEM_LIMIT_BYTES = 56 * 1024 * 1024

F32 = jnp.float32
BF16 = jnp.bfloat16


def _dot(a, b, precision=None):
    return jnp.dot(a, b, preferred_element_type=F32, precision=precision)


def _dot_nt(a, b):
    return lax.dot_general(a, b, (((1,), (1,)), ((), ())), preferred_element_type=F32)


def _rmsnorm(x, g):
    return x * lax.rsqrt(jnp.mean(x * x, axis=-1, keepdims=True) + EPS) * g


def _groupnorm(h, g):
    mu = jnp.mean(h, axis=-1, keepdims=True)
    d = h - mu
    var = jnp.mean(d * d, axis=-1, keepdims=True)
    return d * lax.rsqrt(var + EPS) * g


def _ret_log_gamma(h):
    return math.log(1.0 - 2.0 ** (-5.0 - h))


class _Masks:
    def __init__(self, seg):
        shift = seg.bit_length() - 1
        r = lax.broadcasted_iota(jnp.int32, (BLK, BLK), 0)
        c = lax.broadcasted_iota(jnp.int32, (BLK, BLK), 1)
        rseq = r >> shift
        self.causal = (rseq == (c >> shift)) & (r >= c)
        self.tpos = (r & (seg - 1)).astype(F32)
        self.diff = (r - c).astype(F32)
        self.col_is_last_of_rowseq = c == (rseq * seg + (seg - 1))
        self.col_is_rowseq = c == rseq
        self.row_is_last_of_colseq = r == (c * seg + (seg - 1))


def _rows_from_seq(mask_col_is_rowseq, seq_row):
    return jnp.sum(jnp.where(mask_col_is_rowseq, seq_row, 0.0), axis=1, keepdims=True)


def _seq_from_last_row(mask_row_is_last_of_colseq, col):
    return jnp.sum(jnp.where(mask_row_is_last_of_colseq, col, 0.0), axis=0, keepdims=True)


def _mixer_kernel(*refs, seg, n_blk, fresh_state):
    n_seq = BLK // seg
    assert fresh_state == (n_seq == 1)
    if fresh_state:
        (x_ref, cos_ref, sin_ref, g_pre_ref, w_in_ref, w_gate_ref, b_gate_ref, g_ret_ref, g_ml_ref,
         w_out_ref, g_post_ref,
         y_ref, s_out, c_out, n_out, m_out,
         h_ref, proj_ref, mix_ref, mixed_ref, gate_ref, bcum_ref, gate_t_ref, bcum_t_ref, st_ref) = refs
        c_in, n_in, m_in = c_out, n_out, m_out

        @pl.when(pl.program_id(1) == 0)
        def _():
            st_ref[...] = jnp.zeros(st_ref.shape, F32)
            c_out[...] = jnp.zeros(c_out.shape, F32)
            n_out[...] = jnp.zeros(n_out.shape, F32)
            m_out[...] = jnp.full(m_out.shape, M_INIT, F32)
    else:
        (x_ref, cos_ref, sin_ref, g_pre_ref, w_in_ref, w_gate_ref, b_gate_ref, g_ret_ref, g_ml_ref,
         w_out_ref, g_post_ref, s_in, c_in, n_in, m_in,
         y_ref, s_out, c_out, n_out, m_out,
         h_ref, proj_ref, mix_ref, mixed_ref, gate_ref, bcum_ref, gate_t_ref, bcum_t_ref) = refs

    masks = _Masks(seg)
    scale = HEAD_DIM ** -0.5
    tri = jnp.where(masks.causal, 1.0, 0.0).astype(F32)
    cos = cos_ref[...]
    sin = sin_ref[...]

    def prologue(blocks, rows_all):
        for blk in blocks:
            rows = pl.ds(blk * BLK, BLK)
            h_ref[rows, :] = _rmsnorm(x_ref[blk], g_pre_ref[...]).astype(BF16)
        gate_ref[rows_all, :] = _dot(h_ref[rows_all, :], w_gate_ref[...]) + b_gate_ref[...]
        for blk in blocks:
            rows = pl.ds(blk * BLK, BLK)
            gates = gate_ref[rows, :]
            bcum = _dot(tri, jax.nn.log_sigmoid(gates), precision=lax.Precision.HIGHEST)
            bcum_ref[rows, :] = bcum
            gate_t_ref[blk] = gates.T
            bcum_t_ref[blk] = bcum.T

    def rotary(t):
        return t * cos + pltpu.roll(t, HEAD_DIM // 2, axis=1) * sin

    def seq_rows(j):
        return slice(j * seg, (j + 1) * seg)

    def lane_mask(j):
        c = lax.broadcasted_iota(jnp.int32, (BLK, BLK), 1)
        return (c >= j * seg) & (c < (j + 1) * seg)

    def ret_head(proj, blk, head, col):
        rows = pl.ds(blk * BLK, BLK)
        lg = _ret_log_gamma(head)
        q = rotary(proj[rows,col:col + HEAD_DIM])
        k = rotary(proj[rows,GROUP_W + col:GROUP_W + col + HEAD_DIM]) * scale
        v = proj[rows,2 * GROUP_W + col:2 * GROUP_W + col + HEAD_DIM]
        g = proj[rows,3 * GROUP_W + col:3 * GROUP_W + col + HEAD_DIM]
        qb, kb, vb = q.astype(BF16), k.astype(BF16), v.astype(BF16)
        dmat = jnp.where(masks.causal, jnp.exp(jnp.where(masks.causal, masks.diff * lg, 0.0)), 0.0)
        xi = jnp.exp((masks.tpos + 1.0) * lg)
        zeta = jnp.exp((float(seg - 1) - masks.tpos) * lg)
        chunk_decay = math.exp(seg * lg)
        scores = _dot_nt(qb, kb)
        kz_t = (k * zeta).T
        inter = []
        kz_tb = kz_t.astype(BF16)
        updates = _dot(jnp.concatenate(
            [jnp.where(lane_mask(j), kz_tb, jnp.zeros_like(kz_tb)) for j in range(n_seq)], axis=0), vb)
        for j in range(n_seq):
            seq = blk * n_seq + j
            s_prev = s_in[seq, head]
            inter.append(_dot(q[seq_rows(j)].astype(BF16), s_prev.astype(BF16)))
            s_out[seq, head] = s_prev * chunk_decay + updates[j * HEAD_DIM:(j + 1) * HEAD_DIM]
        inter = jnp.concatenate(inter, axis=0)
        yield
        out = _dot((scores * dmat).astype(BF16), vb)
        yield
        out = out + inter * xi
        y = _groupnorm(out, g_ret_ref[:, head * HEAD_DIM:(head + 1) * HEAD_DIM]) * (g * jax.nn.sigmoid(g))
        mix_ref[rows, head * HEAD_DIM:(head + 1) * HEAD_DIM] = y.astype(BF16)

    def ml_head(proj, blk, head, col):
        rows = pl.ds(blk * BLK, BLK)
        q = proj[rows,col:col + HEAD_DIM]
        k = proj[rows,GROUP_W + col:GROUP_W + col + HEAD_DIM] * scale
        v = proj[rows,2 * GROUP_W + col:2 * GROUP_W + col + HEAD_DIM]
        o = proj[rows,3 * GROUP_W + col:3 * GROUP_W + col + HEAD_DIM]
        qb, kb, vb = q.astype(BF16), k.astype(BF16), v.astype(BF16)
        i_col = gate_ref[rows, head:head + 1]
        b_col = bcum_ref[rows, N_ML_HEADS + head:N_ML_HEADS + head + 1]
        i_row = gate_t_ref[blk, head:head + 1, :]
        b_row = bcum_t_ref[blk, N_ML_HEADS + head:N_ML_HEADS + head + 1, :]
        m_prev_seq = m_in[blk, head:head + 1, :]
        m_prev = _rows_from_seq(masks.col_is_rowseq, m_prev_seq)
        logw = jnp.where(masks.causal, b_col - b_row + i_row, -jnp.inf)
        m_t = jnp.maximum(b_col + m_prev, jnp.max(logw, axis=1, keepdims=True))
        w = jnp.exp(logw - m_t)
        inter_w = jnp.exp(b_col + m_prev - m_t)
        s = _dot_nt(qb, kb)
        m_new_seq = _seq_from_last_row(masks.row_is_last_of_colseq, m_t)
        b_last_seq = _seq_from_last_row(masks.row_is_last_of_colseq, b_col)
        decay_seq = jnp.exp(b_last_seq + m_prev_seq - m_new_seq)
        m_new = _rows_from_seq(masks.col_is_rowseq, m_new_seq)
        b_last = jnp.sum(jnp.where(masks.col_is_last_of_rowseq, b_row, 0.0), axis=1, keepdims=True)
        wl = jnp.exp(b_last - b_col + i_col - m_new)
        vw_t = (v * wl).T
        kw = k * wl
        qc, qn = [], []
        for j in range(n_seq):
            seq = blk * n_seq + j
            c_prev = c_in[seq, head]
            n_prev = n_in[seq, head:head + 1, :]
            qc.append(_dot_nt(q[seq_rows(j)].astype(BF16), c_prev.astype(BF16)))
            qn.append(jnp.sum(q[seq_rows(j)] * n_prev, axis=1, keepdims=True))
            decay = decay_seq[:, j:j + 1]
            n_out[seq, head:head + 1, :] = decay * n_prev + jnp.sum(kw[seq_rows(j)], axis=0, keepdims=True)
        qc = jnp.concatenate(qc, axis=0)
        qn = jnp.concatenate(qn, axis=0)
        m_out[blk, head:head + 1, :] = m_new_seq
        yield
        s = s * w
        num = _dot(s.astype(BF16), vb)
        den = jnp.sum(s, axis=1, keepdims=True)
        vw_tb = vw_t.astype(BF16)
        updates = _dot(jnp.concatenate(
            [jnp.where(lane_mask(j), vw_tb, jnp.zeros_like(vw_tb)) for j in range(n_seq)], axis=0), kb)
        for j in range(n_seq):
            seq = blk * n_seq + j
            c_out[seq, head] = (decay_seq[:, j:j + 1] * c_in[seq, head]
                                + updates[j * HEAD_DIM:(j + 1) * HEAD_DIM])
        yield
        num = num + inter_w * qc
        den = jnp.maximum(jnp.abs(den + inter_w * qn), jnp.exp(-m_t))
        hout = num / den
        y = _groupnorm(hout, g_ml_ref[:, head * HEAD_DIM:(head + 1) * HEAD_DIM]) * jax.nn.sigmoid(o)
        mix_ref[rows, RET_W + head * HEAD_DIM:RET_W + (head + 1) * HEAD_DIM] = y.astype(BF16)

    r_idx = lax.broadcasted_iota(jnp.int32, (BLK, BLK), 0)
    c_idx = lax.broadcasted_iota(jnp.int32, (BLK, BLK), 1)
    causal_t = r_idx <= c_idx
    lane_pos = lax.broadcasted_iota(jnp.int32, (1, BLK), 1).astype(F32)
    head_consts = {}

    def ret_consts(head):
        if head not in head_consts:
            lg = _ret_log_gamma(head)
            dmat_t = jnp.where(causal_t, jnp.exp(jnp.where(causal_t, (c_idx - r_idx).astype(F32) * lg, 0.0)), 0.0)
            xi_row = jnp.exp((lane_pos + 1.0) * lg)
            zeta = jnp.exp((float(BLK - 1) - r_idx.astype(F32)) * lg)
            head_consts[head] = (dmat_t, xi_row, zeta, math.exp(BLK * lg))
        return head_consts[head]

    def channel_norm_t(h_t):
        mu = jnp.mean(h_t, axis=0, keepdims=True)
        d = h_t - mu
        var = jnp.mean(d * d, axis=0, keepdims=True)
        return d * lax.rsqrt(var + EPS)

    def queries_of_pair(q_pair):
        zero = jnp.zeros_like(q_pair[0])
        return jnp.concatenate([jnp.concatenate([q_pair[0], zero], axis=1),
                                jnp.concatenate([zero, q_pair[1]], axis=1)], axis=0)

    def pair_cols(i):
        return slice(i * HEAD_DIM, (i + 1) * HEAD_DIM)

    def ret_pair_single(proj, blk, first_head):
        rows = pl.ds(blk * BLK, BLK)
        pair = range(HEADS_PER_GROUP)
        consts = [ret_consts(first_head + i) for i in pair]
        q = [rotary(proj[rows, pair_cols(i)]) for i in pair]
        k = [rotary(proj[rows, GROUP_W + i * HEAD_DIM:GROUP_W + (i + 1) * HEAD_DIM]) * scale for i in pair]
        v_tb = [proj[rows, 2 * GROUP_W + i * HEAD_DIM:2 * GROUP_W + (i + 1) * HEAD_DIM].T.astype(BF16)
                for i in pair]
        lhs = jnp.concatenate([
            jnp.concatenate([k[i].astype(BF16) for i in pair], axis=1),
            jnp.concatenate([st_ref[blk, first_head + i].astype(BF16) for i in pair], axis=1)], axis=0)
        first = _dot_nt(lhs, queries_of_pair([q[i].astype(BF16) for i in pair]))
        kz_b = [(k[i] * consts[i][2]).astype(BF16) for i in pair]
        yield
        out_t = []
        for i in pair:
            dmat_t, _, _, chunk_decay = consts[i]
            scores_t = first[:BLK, pair_cols(i)]
            both = _dot(v_tb[i], jnp.concatenate([kz_b[i], (scores_t * dmat_t).astype(BF16)], axis=1))
            st_ref[blk, first_head + i] = st_ref[blk, first_head + i] * chunk_decay + both[:, :HEAD_DIM]
            out_t.append(both[:, HEAD_DIM:])
        yield
        for i in pair:
            head = first_head + i
            out = out_t[i] + first[BLK:, pair_cols(i)] * consts[i][1]
            g = proj[rows, 3 * GROUP_W + i * HEAD_DIM:3 * GROUP_W + (i + 1) * HEAD_DIM]
            y = channel_norm_t(out).T * g_ret_ref[:, head * HEAD_DIM:(head + 1) * HEAD_DIM] * (g * jax.nn.sigmoid(g))
            mix_ref[rows, head * HEAD_DIM:(head + 1) * HEAD_DIM] = y.astype(BF16)

    def ml_pair_single(proj, blk, first_head):
        rows = pl.ds(blk * BLK, BLK)
        pair = range(HEADS_PER_GROUP)
        q = [proj[rows, pair_cols(i)] for i in pair]
        k = [proj[rows, GROUP_W + i * HEAD_DIM:GROUP_W + (i + 1) * HEAD_DIM] * scale for i in pair]
        kb = [k[i].astype(BF16) for i in pair]
        n_prev = [n_in[blk, first_head + i:first_head + i + 1, :] for i in pair]
        lhs = jnp.concatenate([
            jnp.concatenate(kb, axis=1),
            jnp.concatenate([c_in[blk, first_head + i].astype(BF16) for i in pair], axis=1),
            jnp.concatenate([jnp.broadcast_to(n_prev[i], (2 * SUBLANES, HEAD_DIM)).astype(BF16) for i in pair],
                            axis=1)], axis=0)
        first = _dot_nt(lhs, queries_of_pair([q[i].astype(BF16) for i in pair]))
        heads = [ml_head_single(proj, blk, first_head + i, i * HEAD_DIM, k[i], kb[i], n_prev[i],
                                first[:BLK, pair_cols(i)], first[BLK:2 * BLK, pair_cols(i)],
                                first[2 * BLK:2 * BLK + 1, pair_cols(i)]) for i in pair]
        for _ in range(3):
            for head in heads:
                next(head, None)
            yield

    def ml_head_single(proj, blk, head, col, k, kb, n_prev, s_t, qc_t, qn):
        rows = pl.ds(blk * BLK, BLK)
        v_t = proj[rows, 2 * GROUP_W + col:2 * GROUP_W + col + HEAD_DIM].T
        i_row = gate_t_ref[blk, head:head + 1, :]
        b_row = bcum_t_ref[blk, N_ML_HEADS + head:N_ML_HEADS + head + 1, :]
        key_term = gate_ref[rows, head:head + 1] - bcum_ref[rows, N_ML_HEADS + head:N_ML_HEADS + head + 1]
        m_prev = m_in[blk, head:head + 1, 0:1]
        logw_t = jnp.where(causal_t, b_row + key_term, -jnp.inf)
        m_t = jnp.maximum(b_row + m_prev, jnp.max(logw_t, axis=0, keepdims=True))
        w_t = jnp.exp(logw_t - m_t)
        inter_w = jnp.exp(b_row + m_prev - m_t)
        m_new = m_t[:, BLK - 1:BLK]
        b_last = b_row[:, BLK - 1:BLK]
        decay = jnp.exp(b_last + m_prev - m_new)
        wl = jnp.exp(b_last - b_row + i_row - m_new)
        m_out[blk, head:head + 1, :] = jnp.broadcast_to(m_new, (1, BLK))
        wk = _dot(jnp.broadcast_to(wl, (2 * SUBLANES, BLK)).astype(BF16), kb)[0:1, :]
        n_out[blk, head:head + 1, :] = decay * n_prev + wk
        kw_b = (k * jnp.exp(key_term + (b_last - m_new))).astype(BF16)
        v_tb = v_t.astype(BF16)
        yield
        s_t = s_t * w_t
        den = jnp.sum(s_t, axis=0, keepdims=True)
        both = _dot(v_tb, jnp.concatenate([kw_b, s_t.astype(BF16)], axis=1))
        c_out[blk, head] = decay * c_in[blk, head] + both[:, :HEAD_DIM]
        num_t = both[:, HEAD_DIM:]
        yield
        num_t = num_t + inter_w * qc_t
        den = jnp.maximum(jnp.abs(den + inter_w * qn), jnp.exp(-m_t))
        o = proj[rows, 3 * GROUP_W + col:3 * GROUP_W + col + HEAD_DIM]
        y = (channel_norm_t(num_t / den).T * g_ml_ref[:, head * HEAD_DIM:(head + 1) * HEAD_DIM]
             * jax.nn.sigmoid(o))
        mix_ref[rows, RET_W + head * HEAD_DIM:RET_W + (head + 1) * HEAD_DIM] = y.astype(BF16)

    n_ret_groups = N_RET_HEADS // HEADS_PER_GROUP
    n_groups = n_ret_groups + N_ML_HEADS // HEADS_PER_GROUP

    def first_head_of(grp):
        return (grp if grp < n_ret_groups else grp - n_ret_groups) * HEADS_PER_GROUP

    def project(grp, kind, rows_all):
        is_ret = grp < n_ret_groups
        c0 = (0 if is_ret else 4 * RET_W) + kind * (RET_W if is_ret else ML_W) + first_head_of(grp) * HEAD_DIM
        proj_ref[grp % 2, rows_all, kind * GROUP_W:(kind + 1) * GROUP_W] = _dot(
            h_ref[rows_all, :], w_in_ref[:, c0:c0 + GROUP_W])

    def out_project(grp, rows_all):
        cols = slice(grp * GROUP_W, (grp + 1) * GROUP_W)
        part = _dot(mix_ref[rows_all, cols], w_out_ref[cols, :])
        if grp == 0:
            mixed_ref[rows_all, :] = part
        else:
            mixed_ref[rows_all, :] += part

    def row_pipeline(blocks):
        rows_all = pl.ds(blocks[0] * BLK, len(blocks) * BLK)
        prologue(blocks, rows_all)
        for kind in range(4):
            project(0, kind, rows_all)
        yield
        for grp in range(n_groups):
            if n_seq == 1:
                pair_fn = ret_pair_single if grp < n_ret_groups else ml_pair_single
                heads = [pair_fn(proj_ref.at[grp % 2], blk, first_head_of(grp)) for blk in blocks]
            else:
                head_fn = ret_head if grp < n_ret_groups else ml_head
                heads = [head_fn(proj_ref.at[grp % 2], blk, first_head_of(grp) + hh, hh * HEAD_DIM)
                         for blk in blocks for hh in range(HEADS_PER_GROUP)]
            fillers = {0: [functools.partial(project, grp + 1, kind, rows_all) for kind in range(4)]
                       if grp + 1 < n_groups else [],
                       1: [functools.partial(out_project, grp - 1, rows_all)] if grp >= 1 else [],
                       2: []}
            for phase in range(3):
                todo = list(fillers[phase])
                every = max(1, len(heads) // max(1, len(todo)))
                for i, head in enumerate(heads):
                    next(head, None)
                    if todo and (i + 1) % every == 0:
                        todo.pop(0)()
                for filler in todo:
                    filler()
            yield
        last_cols = slice((n_groups - 1) * GROUP_W, n_groups * GROUP_W)
        last = _dot(mix_ref[rows_all, last_cols], w_out_ref[last_cols, :])
        for i, blk in enumerate(blocks):
            rows = pl.ds(blk * BLK, BLK)
            mixed = mixed_ref[rows, :] + last[i * BLK:(i + 1) * BLK]
            y_ref[blk] = x_ref[blk] + _rmsnorm(mixed, g_post_ref[...])

    n_runs = 2 if n_blk % 2 == 0 else 1
    per_run = n_blk // n_runs
    pipelines = [row_pipeline(list(range(i * per_run, (i + 1) * per_run))) for i in range(n_runs)]
    n_stages = n_groups + 2
    for step in range(n_stages + n_runs - 1):
        for lag, pipeline in enumerate(pipelines):
            if 0 <= step - lag < n_stages:
                next(pipeline, None)

    if fresh_state:
        @pl.when(pl.program_id(1) == pl.num_programs(1) - 1)
        def _():
            for blk in range(n_blk):
                for head in range(N_RET_HEADS):
                    s_out[blk, head] = st_ref[blk, head].T


def _ffn_kernel(*refs, seg, n_blk, fresh_state):
    n_seq_tile = n_blk * BLK // seg
    rows_tile = n_blk * BLK
    if fresh_state:
        (x_ref, g_pre_ref, w_up_ref, conv_w_ref, conv_b_ref, w_down_ref, g_post_ref,
         y_ref, buf_out, h_ref, act_ref, ffn_ref) = refs
        buf_in = buf_out

        @pl.when(pl.program_id(1) == 0)
        def _():
            buf_out[...] = jnp.zeros(buf_out.shape, F32)
    else:
        (x_ref, g_pre_ref, w_up_ref, conv_w_ref, conv_b_ref, w_down_ref, g_post_ref, buf_in,
         y_ref, buf_out, h_ref, act_ref, ffn_ref) = refs

    for blk in range(n_blk):
        rows = pl.ds(blk * BLK, BLK)
        h_ref[rows, :] = _rmsnorm(x_ref[blk], g_pre_ref[...]).astype(BF16)

    tpos = lax.broadcasted_iota(jnp.int32, (n_seq_tile, seg, FF_BLK), 1)

    def conv(cols):
        up = _dot(h_ref[...], w_up_ref[:, cols])
        prev0 = buf_in[:, 0:1, cols]
        prev1 = buf_in[:, 1:2, cols]
        up3 = up.reshape(n_seq_tile, seg, FF_BLK)
        sh1 = pltpu.roll(up, 1, axis=0).reshape(n_seq_tile, seg, FF_BLK)
        sh2 = pltpu.roll(up, 2, axis=0).reshape(n_seq_tile, seg, FF_BLK)
        sh1 = jnp.where(tpos == 0, prev1, sh1)
        sh2 = jnp.where(tpos == 0, prev0, jnp.where(tpos == 1, prev1, sh2))
        buf_out[:, 0:1, cols] = up3[:, seg - 2:seg - 1, :]
        buf_out[:, 1:2, cols] = up3[:, seg - 1:seg, :]
        w = conv_w_ref[:, cols]
        out = sh2 * w[0:1, :] + sh1 * w[1:2, :] + up3 * w[2:3, :] + conv_b_ref[:, cols]
        return out.reshape(rows_tile, FF_BLK)

    def activation(gate, val):
        c0 = -2.0 * math.log2(math.e) * math.sqrt(2.0 / math.pi)
        c1 = 0.044715 * c0
        return ((gate * val) / (1.0 + jnp.exp2(gate * (gate * gate * c1 + c0)))).astype(BF16)

    for j in range(D_FF // FF_BLK):
        cols = slice(j * FF_BLK, (j + 1) * FF_BLK)
        gate = conv(cols)
        val = conv(slice(D_FF + j * FF_BLK, D_FF + (j + 1) * FF_BLK))
        act_ref[:, cols] = activation(gate, val)

    ffn_ref[...] = _dot(act_ref[...], w_down_ref[...])
    for blk in range(n_blk):
        rows = pl.ds(blk * BLK, BLK)
        y_ref[blk] = x_ref[blk] + _rmsnorm(ffn_ref[rows, :], g_post_ref[...])


def _const_spec(shape):
    zeros = (0,) * len(shape)
    return pl.BlockSpec(shape, lambda *_: zeros, pipeline_mode=pl.Buffered(1))


def _compiler_params(n_grid_dims):
    return pltpu.CompilerParams(
        dimension_semantics=("arbitrary",) * n_grid_dims,
        vmem_limit_bytes=VMEM_LIMIT_BYTES)


def _mixer_scratch(n_blk, fresh_state):
    rows = n_blk * BLK
    carried = [pltpu.VMEM((n_blk, N_RET_HEADS, HEAD_DIM, HEAD_DIM), F32)] if fresh_state else []
    return [
        pltpu.VMEM((rows, D_MODEL), BF16),
        pltpu.VMEM((2, rows, 4 * GROUP_W), F32),
        pltpu.VMEM((rows, RET_W + ML_W), BF16),
        pltpu.VMEM((rows, D_MODEL), F32),
        pltpu.VMEM((rows, BLK), F32),
        pltpu.VMEM((rows, BLK), F32),
        pltpu.VMEM((n_blk, BLK, BLK), F32),
        pltpu.VMEM((n_blk, BLK, BLK), F32),
    ] + carried


def _mixer_weight_specs():
    return [
        _const_spec((1, D_MODEL)),
        _const_spec((D_MODEL, 4 * RET_W + 4 * ML_W + 2 * N_ML_HEADS)),
        _const_spec((D_MODEL, BLK)),
        _const_spec((1, BLK)),
        _const_spec((1, RET_W)),
        _const_spec((1, ML_W)),
        _const_spec((RET_W + ML_W, D_MODEL)),
        _const_spec((1, D_MODEL)),
    ]


def _state_shapes(n_seqs, n_blocks):
    return [
        jax.ShapeDtypeStruct((n_seqs, N_RET_HEADS, HEAD_DIM, HEAD_DIM), F32),
        jax.ShapeDtypeStruct((n_seqs, N_ML_HEADS, HEAD_DIM, HEAD_DIM), F32),
        jax.ShapeDtypeStruct((n_seqs, N_ML_HEADS, HEAD_DIM), F32),
        jax.ShapeDtypeStruct((n_blocks, N_ML_HEADS, BLK), F32),
    ]


def _state_specs(n_seq_tile, n_blk, index):
    return [
        pl.BlockSpec((n_seq_tile, N_RET_HEADS, HEAD_DIM, HEAD_DIM), lambda *g: (index(*g), 0, 0, 0)),
        pl.BlockSpec((n_seq_tile, N_ML_HEADS, HEAD_DIM, HEAD_DIM), lambda *g: (index(*g), 0, 0, 0)),
        pl.BlockSpec((n_seq_tile, N_ML_HEADS, HEAD_DIM), lambda *g: (index(*g), 0, 0)),
        pl.BlockSpec((n_blk, N_ML_HEADS, BLK), lambda *g: (index(*g), 0, 0)),
    ]


def _prompt_mixer(x, cos, sin, weights, n_blk):
    batch, seq_len, _ = x.shape
    grid = (batch // n_blk, seq_len // BLK)
    x_spec = pl.BlockSpec((n_blk, BLK, D_MODEL), lambda g, c: (g, c, 0))
    rope_spec = pl.BlockSpec((BLK, HEAD_DIM), lambda g, c: (c, 0))
    return pl.pallas_call(
        functools.partial(_mixer_kernel, seg=BLK, n_blk=n_blk, fresh_state=True),
        grid=grid,
        in_specs=[x_spec, rope_spec, rope_spec] + _mixer_weight_specs(),
        out_specs=[x_spec] + _state_specs(n_blk, n_blk, lambda g, c: g),
        out_shape=[jax.ShapeDtypeStruct(x.shape, F32)] + _state_shapes(batch, batch),
        scratch_shapes=_mixer_scratch(n_blk, fresh_state=True),
        compiler_params=_compiler_params(2),
        name="prompt_mixer",
    )(x, cos, sin, *weights)


def _sample_mixer(x, cos, sin, weights, states, seg, n_blk):
    n_blocks = x.shape[0]
    n_seq_tile = n_blk * BLK // seg
    grid = (n_blocks // n_blk,)
    x_spec = pl.BlockSpec((n_blk, BLK, D_MODEL), lambda g: (g, 0, 0))
    rope_spec = _const_spec((BLK, HEAD_DIM))
    state_specs = _state_specs(n_seq_tile, n_blk, lambda g: g)
    return pl.pallas_call(
        functools.partial(_mixer_kernel, seg=seg, n_blk=n_blk, fresh_state=False),
        grid=grid,
        in_specs=[x_spec, rope_spec, rope_spec] + _mixer_weight_specs() + state_specs,
        out_specs=[x_spec] + state_specs,
        out_shape=[jax.ShapeDtypeStruct(x.shape, F32)]
        + _state_shapes(n_blocks * BLK // seg, n_blocks),
        scratch_shapes=_mixer_scratch(n_blk, fresh_state=False),
        compiler_params=_compiler_params(1),
        name="sample_mixer",
    )(x, cos, sin, *weights, *states)


def _ffn_scratch(n_blk):
    rows = n_blk * BLK
    return [
        pltpu.VMEM((rows, D_MODEL), BF16),
        pltpu.VMEM((rows, D_FF), BF16),
        pltpu.VMEM((rows, D_MODEL), F32),
    ]


def _ffn_weight_specs():
    return [
        _const_spec((1, D_MODEL)),
        _const_spec((D_MODEL, 2 * D_FF)),
        _const_spec((CONV_W, 2 * D_FF)),
        _const_spec((1, 2 * D_FF)),
        _const_spec((D_FF, D_MODEL)),
        _const_spec((1, D_MODEL)),
    ]


def _prompt_ffn(x, weights, n_blk):
    batch, seq_len, _ = x.shape
    grid = (batch // n_blk, seq_len // BLK)
    x_spec = pl.BlockSpec((n_blk, BLK, D_MODEL), lambda g, c: (g, c, 0))
    buf_spec = pl.BlockSpec((n_blk, CONV_W - 1, 2 * D_FF), lambda g, c: (g, 0, 0))
    return pl.pallas_call(
        functools.partial(_ffn_kernel, seg=BLK, n_blk=n_blk, fresh_state=True),
        grid=grid,
        in_specs=[x_spec] + _ffn_weight_specs(),
        out_specs=[x_spec, buf_spec],
        out_shape=[jax.ShapeDtypeStruct(x.shape, F32),
                   jax.ShapeDtypeStruct((batch, CONV_W - 1, 2 * D_FF), F32)],
        scratch_shapes=_ffn_scratch(n_blk),
        compiler_params=_compiler_params(2),
        name="prompt_ffn",
    )(x, *weights)


def _sample_ffn(x, weights, conv_buf, seg, n_blk):
    n_blocks = x.shape[0]
    n_seq_tile = n_blk * BLK // seg
    grid = (n_blocks // n_blk,)
    x_spec = pl.BlockSpec((n_blk, BLK, D_MODEL), lambda g: (g, 0, 0))
    buf_spec = pl.BlockSpec((n_seq_tile, CONV_W - 1, 2 * D_FF), lambda g: (g, 0, 0))
    return pl.pallas_call(
        functools.partial(_ffn_kernel, seg=seg, n_blk=n_blk, fresh_state=False),
        grid=grid,
        in_specs=[x_spec] + _ffn_weight_specs() + [buf_spec],
        out_specs=[x_spec, buf_spec],
        out_shape=[jax.ShapeDtypeStruct(x.shape, F32),
                   jax.ShapeDtypeStruct(conv_buf.shape, F32)],
        scratch_shapes=_ffn_scratch(n_blk),
        compiler_params=_compiler_params(1),
        name="sample_ffn",
    )(x, *weights, conv_buf)


def _rope_tables(pos):
    freqs = ROPE_BASE ** (-jnp.arange(0, HEAD_DIM, 2, dtype=F32) / HEAD_DIM)
    ang = pos.astype(F32)[:, None] * freqs[None, :]
    cos, sin = jnp.cos(ang), jnp.sin(ang)
    return jnp.concatenate([cos, cos], axis=-1), jnp.concatenate([-sin, sin], axis=-1)


def kernel(x_prompt, x_sample, state_ret, state_mlstm_C, state_mlstm_n, state_mlstm_m, cache_ffn_conv, pre_mix_gain, w_in, b_gates, ret_head_gain, mlstm_head_gain, w_out, post_mix_gain, pre_ffn_gain, w_up, conv_w, conv_b, w_down, post_ffn_gain):
    depth = w_in.shape[0]
    assert depth == 1
    batch, seq_len, _ = x_prompt.shape
    dec_batch, dec_seq, _ = x_sample.shape
    assert seq_len % BLK == 0 and BLK % dec_seq == 0 and (dec_batch * dec_seq) % BLK == 0
    seqs_per_blk = BLK // dec_seq
    n_sample_blocks = dec_batch * dec_seq // BLK
    layer = 0

    n_gate = 2 * N_ML_HEADS
    w_gate = jnp.pad(w_in[layer][:, -n_gate:], ((0, 0), (0, BLK - n_gate))).astype(BF16)
    b_gate = jnp.pad(b_gates[layer], (0, BLK - n_gate)).reshape(1, BLK)
    mixer_weights = (
        pre_mix_gain[layer].reshape(1, D_MODEL), w_in[layer].astype(BF16), w_gate, b_gate,
        ret_head_gain[layer].reshape(1, RET_W), mlstm_head_gain[layer].reshape(1, ML_W),
        w_out[layer].astype(BF16), post_mix_gain[layer].reshape(1, D_MODEL))
    ffn_weights = (
        pre_ffn_gain[layer].reshape(1, D_MODEL), w_up[layer].astype(BF16), conv_w[layer],
        conv_b[layer].reshape(1, 2 * D_FF), w_down[layer].astype(BF16),
        post_ffn_gain[layer].reshape(1, D_MODEL))

    cos_p, sin_p = _rope_tables(jnp.arange(seq_len, dtype=jnp.int32))
    x1_p, s_p, c_p, n_p, m_p = _prompt_mixer(x_prompt, cos_p, sin_p, mixer_weights, n_blk=8)
    y_p, buf_p = _prompt_ffn(x1_p, ffn_weights, n_blk=8)
    m_p = m_p[:, :, 0]

    pos_s = PAST_LEN + jnp.arange(dec_seq, dtype=jnp.int32)
    cos_s, sin_s = _rope_tables(jnp.tile(pos_s, seqs_per_blk))
    m_blocks = jnp.transpose(state_mlstm_m[layer].reshape(n_sample_blocks, seqs_per_blk, N_ML_HEADS), (0, 2, 1))
    m_blocks = jnp.pad(m_blocks, ((0, 0), (0, 0), (0, BLK - seqs_per_blk)))
    xs = x_sample.reshape(n_sample_blocks, BLK, D_MODEL)
    x1_s, s_s, c_s, n_s, m_s = _sample_mixer(
        xs, cos_s, sin_s, mixer_weights,
        (state_ret[layer], state_mlstm_C[layer], state_mlstm_n[layer], m_blocks),
        seg=dec_seq, n_blk=1)
    y_s, buf_s = _sample_ffn(x1_s, ffn_weights, cache_ffn_conv[layer], seg=dec_seq, n_blk=2)
    y_s = y_s.reshape(x_sample.shape)
    m_s = jnp.transpose(m_s[:, :, :seqs_per_blk], (0, 2, 1)).reshape(dec_batch, N_ML_HEADS)

    return (y_p, y_s, s_p[None], s_s[None], c_p[None], c_s[None], n_p[None], n_s[None],
            m_p[None], m_s[None], buf_p[None], buf_s[None])
```

```python
import functools
import math

import jax
import jax.numpy as jnp
from jax import lax
from jax.experimental import pallas as pl
from jax.experimental.pallas import tpu as pltpu

D_MODEL = 1024
HEAD_DIM = 128
N_RET_HEADS = 4
N_ML_HEADS = 4
RET_W = N_RET_HEADS * HEAD_DIM
ML_W = N_ML_HEADS * HEAD_DIM
D_FF = 2816
CONV_W = 3
PAST_LEN = 16384
ROPE_BASE = 10000.0
EPS = 1e-6
M_INIT = -1e30

MXU_WIDTH = 256
SUBLANES = 8
V7X_VMEM_BYTES = 64 * 1024 * 1024
VMEM_LIMIT_BYTES = V7X_VMEM_BYTES - 8 * 1024 * 1024

BLK = 128
HEADS_PER_GROUP = MXU_WIDTH // HEAD_DIM
GROUP_W = HEADS_PER_GROUP * HEAD_DIM
FF_BLK = MXU_WIDTH

PROMPT_BLOCKS_PER_STEP = 8
SAMPLE_MIXER_BLOCKS_PER_STEP = 1
SAMPLE_FFN_BLOCKS_PER_STEP = 2

F32 = jnp.float32
BF16 = jnp.bfloat16


def _dot(a, b, precision=None):
    return jnp.dot(a, b, preferred_element_type=F32, precision=precision)


def _dot_nt(a, b):
    return lax.dot_general(a, b, (((1,), (1,)), ((), ())), preferred_element_type=F32)


def _rmsnorm(x, g):
    return x * lax.rsqrt(jnp.mean(x * x, axis=-1, keepdims=True) + EPS) * g


def _groupnorm(h, g):
    mu = jnp.mean(h, axis=-1, keepdims=True)
    d = h - mu
    var = jnp.mean(d * d, axis=-1, keepdims=True)
    return d * lax.rsqrt(var + EPS) * g


def _ret_log_gamma(h):
    return math.log(1.0 - 2.0 ** (-5.0 - h))


class _Masks:
    def __init__(self, seg):
        shift = seg.bit_length() - 1
        r = lax.broadcasted_iota(jnp.int32, (BLK, BLK), 0)
        c = lax.broadcasted_iota(jnp.int32, (BLK, BLK), 1)
        rseq = r >> shift
        self.causal = (rseq == (c >> shift)) & (r >= c)
        self.tpos = (r & (seg - 1)).astype(F32)
        self.diff = (r - c).astype(F32)
        self.col_is_last_of_rowseq = c == (rseq * seg + (seg - 1))
        self.col_is_rowseq = c == rseq
        self.row_is_last_of_colseq = r == (c * seg + (seg - 1))


def _rows_from_seq(mask_col_is_rowseq, seq_row):
    return jnp.sum(jnp.where(mask_col_is_rowseq, seq_row, 0.0), axis=1, keepdims=True)


def _seq_from_last_row(mask_row_is_last_of_colseq, col):
    return jnp.sum(jnp.where(mask_row_is_last_of_colseq, col, 0.0), axis=0, keepdims=True)


def _mixer_kernel(*refs, seg, n_blk, fresh_state):
    n_seq = BLK // seg
    assert fresh_state == (n_seq == 1)
    if fresh_state:
        (x_ref, cos_ref, sin_ref, g_pre_ref, w_in_ref, w_gate_ref, b_gate_ref, g_ret_ref, g_ml_ref,
         w_out_ref, g_post_ref,
         y_ref, s_out, c_out, n_out, m_out,
         h_ref, proj_ref, mix_ref, mixed_ref, gate_ref, bcum_ref, gate_t_ref, bcum_t_ref, st_ref) = refs
        c_in, n_in, m_in = c_out, n_out, m_out

        @pl.when(pl.program_id(1) == 0)
        def _():
            st_ref[...] = jnp.zeros(st_ref.shape, F32)
            c_out[...] = jnp.zeros(c_out.shape, F32)
            n_out[...] = jnp.zeros(n_out.shape, F32)
            m_out[...] = jnp.full(m_out.shape, M_INIT, F32)
    else:
        (x_ref, cos_ref, sin_ref, g_pre_ref, w_in_ref, w_gate_ref, b_gate_ref, g_ret_ref, g_ml_ref,
         w_out_ref, g_post_ref, s_in, c_in, n_in, m_in,
         y_ref, s_out, c_out, n_out, m_out,
         h_ref, proj_ref, mix_ref, mixed_ref, gate_ref, bcum_ref, gate_t_ref, bcum_t_ref) = refs

    masks = _Masks(seg)
    scale = HEAD_DIM ** -0.5
    tri = jnp.where(masks.causal, 1.0, 0.0).astype(F32)
    cos = cos_ref[...]
    sin = sin_ref[...]

    def prologue(blocks, rows_all):
        for blk in blocks:
            rows = pl.ds(blk * BLK, BLK)
            h_ref[rows, :] = _rmsnorm(x_ref[blk], g_pre_ref[...]).astype(BF16)
        gate_ref[rows_all, :] = _dot(h_ref[rows_all, :], w_gate_ref[...]) + b_gate_ref[...]
        for blk in blocks:
            rows = pl.ds(blk * BLK, BLK)
            gates = gate_ref[rows, :]
            bcum = _dot(tri, jax.nn.log_sigmoid(gates), precision=lax.Precision.HIGHEST)
            bcum_ref[rows, :] = bcum
            gate_t_ref[blk] = gates.T
            bcum_t_ref[blk] = bcum.T

    def rotary(t):
        return t * cos + pltpu.roll(t, HEAD_DIM // 2, axis=1) * sin

    def seq_rows(j):
        return slice(j * seg, (j + 1) * seg)

    def lane_mask(j):
        c = lax.broadcasted_iota(jnp.int32, (BLK, BLK), 1)
        return (c >= j * seg) & (c < (j + 1) * seg)

    def ret_head(proj, blk, head, col):
        rows = pl.ds(blk * BLK, BLK)
        lg = _ret_log_gamma(head)
        q = rotary(proj[rows,col:col + HEAD_DIM])
        k = rotary(proj[rows,GROUP_W + col:GROUP_W + col + HEAD_DIM]) * scale
        v = proj[rows,2 * GROUP_W + col:2 * GROUP_W + col + HEAD_DIM]
        g = proj[rows,3 * GROUP_W + col:3 * GROUP_W + col + HEAD_DIM]
        qb, kb, vb = q.astype(BF16), k.astype(BF16), v.astype(BF16)
        dmat = jnp.where(masks.causal, jnp.exp(jnp.where(masks.causal, masks.diff * lg, 0.0)), 0.0)
        xi = jnp.exp((masks.tpos + 1.0) * lg)
        zeta = jnp.exp((float(seg - 1) - masks.tpos) * lg)
        chunk_decay = math.exp(seg * lg)
        scores = _dot_nt(qb, kb)
        kz_t = (k * zeta).T
        inter = []
        kz_tb = kz_t.astype(BF16)
        updates = _dot(jnp.concatenate(
            [jnp.where(lane_mask(j), kz_tb, jnp.zeros_like(kz_tb)) for j in range(n_seq)], axis=0), vb)
        for j in range(n_seq):
            seq = blk * n_seq + j
            s_prev = s_in[seq, head]
            inter.append(_dot(q[seq_rows(j)].astype(BF16), s_prev.astype(BF16)))
            s_out[seq, head] = s_prev * chunk_decay + updates[j * HEAD_DIM:(j + 1) * HEAD_DIM]
        inter = jnp.concatenate(inter, axis=0)
        yield
        out = _dot((scores * dmat).astype(BF16), vb)
        yield
        out = out + inter * xi
        y = _groupnorm(out, g_ret_ref[:, head * HEAD_DIM:(head + 1) * HEAD_DIM]) * (g * jax.nn.sigmoid(g))
        mix_ref[rows, head * HEAD_DIM:(head + 1) * HEAD_DIM] = y.astype(BF16)

    def ml_head(proj, blk, head, col):
        rows = pl.ds(blk * BLK, BLK)
        q = proj[rows,col:col + HEAD_DIM]
        k = proj[rows,GROUP_W + col:GROUP_W + col + HEAD_DIM] * scale
        v = proj[rows,2 * GROUP_W + col:2 * GROUP_W + col + HEAD_DIM]
        o = proj[rows,3 * GROUP_W + col:3 * GROUP_W + col + HEAD_DIM]
        qb, kb, vb = q.astype(BF16), k.astype(BF16), v.astype(BF16)
        i_col = gate_ref[rows, head:head + 1]
        b_col = bcum_ref[rows, N_ML_HEADS + head:N_ML_HEADS + head + 1]
        i_row = gate_t_ref[blk, head:head + 1, :]
        b_row = bcum_t_ref[blk, N_ML_HEADS + head:N_ML_HEADS + head + 1, :]
        m_prev_seq = m_in[blk, head:head + 1, :]
        m_prev = _rows_from_seq(masks.col_is_rowseq, m_prev_seq)
        logw = jnp.where(masks.causal, b_col - b_row + i_row, -jnp.inf)
        m_t = jnp.maximum(b_col + m_prev, jnp.max(logw, axis=1, keepdims=True))
        w = jnp.exp(logw - m_t)
        inter_w = jnp.exp(b_col + m_prev - m_t)
        s = _dot_nt(qb, kb)
        m_new_seq = _seq_from_last_row(masks.row_is_last_of_colseq, m_t)
        b_last_seq = _seq_from_last_row(masks.row_is_last_of_colseq, b_col)
        decay_seq = jnp.exp(b_last_seq + m_prev_seq - m_new_seq)
        m_new = _rows_from_seq(masks.col_is_rowseq, m_new_seq)
        b_last = jnp.sum(jnp.where(masks.col_is_last_of_rowseq, b_row, 0.0), axis=1, keepdims=True)
        wl = jnp.exp(b_last - b_col + i_col - m_new)
        vw_t = (v * wl).T
        kw = k * wl
        qc, qn = [], []
        for j in range(n_seq):
            seq = blk * n_seq + j
            c_prev = c_in[seq, head]
            n_prev = n_in[seq, head:head + 1, :]
            qc.append(_dot_nt(q[seq_rows(j)].astype(BF16), c_prev.astype(BF16)))
            qn.append(jnp.sum(q[seq_rows(j)] * n_prev, axis=1, keepdims=True))
            decay = decay_seq[:, j:j + 1]
            n_out[seq, head:head + 1, :] = decay * n_prev + jnp.sum(kw[seq_rows(j)], axis=0, keepdims=True)
        qc = jnp.concatenate(qc, axis=0)
        qn = jnp.concatenate(qn, axis=0)
        m_out[blk, head:head + 1, :] = m_new_seq
        yield
        s = s * w
        num = _dot(s.astype(BF16), vb)
        den = jnp.sum(s, axis=1, keepdims=True)
        vw_tb = vw_t.astype(BF16)
        updates = _dot(jnp.concatenate(
            [jnp.where(lane_mask(j), vw_tb, jnp.zeros_like(vw_tb)) for j in range(n_seq)], axis=0), kb)
        for j in range(n_seq):
            seq = blk * n_seq + j
            c_out[seq, head] = (decay_seq[:, j:j + 1] * c_in[seq, head]
                                + updates[j * HEAD_DIM:(j + 1) * HEAD_DIM])
        yield
        num = num + inter_w * qc
        den = jnp.maximum(jnp.abs(den + inter_w * qn), jnp.exp(-m_t))
        hout = num / den
        y = _groupnorm(hout, g_ml_ref[:, head * HEAD_DIM:(head + 1) * HEAD_DIM]) * jax.nn.sigmoid(o)
        mix_ref[rows, RET_W + head * HEAD_DIM:RET_W + (head + 1) * HEAD_DIM] = y.astype(BF16)

    r_idx = lax.broadcasted_iota(jnp.int32, (BLK, BLK), 0)
    c_idx = lax.broadcasted_iota(jnp.int32, (BLK, BLK), 1)
    causal_t = r_idx <= c_idx
    lane_pos = lax.broadcasted_iota(jnp.int32, (1, BLK), 1).astype(F32)
    head_consts = {}

    def ret_consts(head):
        if head not in head_consts:
            lg = _ret_log_gamma(head)
            dmat_t = jnp.where(causal_t, jnp.exp(jnp.where(causal_t, (c_idx - r_idx).astype(F32) * lg, 0.0)), 0.0)
            xi_row = jnp.exp((lane_pos + 1.0) * lg)
            zeta = jnp.exp((float(BLK - 1) - r_idx.astype(F32)) * lg)
            head_consts[head] = (dmat_t, xi_row, zeta, math.exp(BLK * lg))
        return head_consts[head]

    def channel_norm_t(h_t):
        mu = jnp.mean(h_t, axis=0, keepdims=True)
        d = h_t - mu
        var = jnp.mean(d * d, axis=0, keepdims=True)
        return d * lax.rsqrt(var + EPS)

    def queries_of_pair(q_pair):
        zero = jnp.zeros_like(q_pair[0])
        return jnp.concatenate([jnp.concatenate([q_pair[0], zero], axis=1),
                                jnp.concatenate([zero, q_pair[1]], axis=1)], axis=0)

    def pair_cols(i):
        return slice(i * HEAD_DIM, (i + 1) * HEAD_DIM)

    def ret_pair_single(proj, blk, first_head):
        rows = pl.ds(blk * BLK, BLK)
        pair = range(HEADS_PER_GROUP)
        consts = [ret_consts(first_head + i) for i in pair]
        q = [rotary(proj[rows, pair_cols(i)]) for i in pair]
        k = [rotary(proj[rows, GROUP_W + i * HEAD_DIM:GROUP_W + (i + 1) * HEAD_DIM]) * scale for i in pair]
        v_tb = [proj[rows, 2 * GROUP_W + i * HEAD_DIM:2 * GROUP_W + (i + 1) * HEAD_DIM].T.astype(BF16)
                for i in pair]
        lhs = jnp.concatenate([
            jnp.concatenate([k[i].astype(BF16) for i in pair], axis=1),
            jnp.concatenate([st_ref[blk, first_head + i].astype(BF16) for i in pair], axis=1)], axis=0)
        first = _dot_nt(lhs, queries_of_pair([q[i].astype(BF16) for i in pair]))
        kz_b = [(k[i] * consts[i][2]).astype(BF16) for i in pair]
        yield
        out_t = []
        for i in pair:
            dmat_t, _, _, chunk_decay = consts[i]
            scores_t = first[:BLK, pair_cols(i)]
            both = _dot(v_tb[i], jnp.concatenate([kz_b[i], (scores_t * dmat_t).astype(BF16)], axis=1))
            st_ref[blk, first_head + i] = st_ref[blk, first_head + i] * chunk_decay + both[:, :HEAD_DIM]
            out_t.append(both[:, HEAD_DIM:])
        yield
        for i in pair:
            head = first_head + i
            out = out_t[i] + first[BLK:, pair_cols(i)] * consts[i][1]
            g = proj[rows, 3 * GROUP_W + i * HEAD_DIM:3 * GROUP_W + (i + 1) * HEAD_DIM]
            y = channel_norm_t(out).T * g_ret_ref[:, head * HEAD_DIM:(head + 1) * HEAD_DIM] * (g * jax.nn.sigmoid(g))
            mix_ref[rows, head * HEAD_DIM:(head + 1) * HEAD_DIM] = y.astype(BF16)

    def ml_pair_single(proj, blk, first_head):
        rows = pl.ds(blk * BLK, BLK)
        pair = range(HEADS_PER_GROUP)
        q = [proj[rows, pair_cols(i)] for i in pair]
        k = [proj[rows, GROUP_W + i * HEAD_DIM:GROUP_W + (i + 1) * HEAD_DIM] * scale for i in pair]
        kb = [k[i].astype(BF16) for i in pair]
        n_prev = [n_in[blk, first_head + i:first_head + i + 1, :] for i in pair]
        lhs = jnp.concatenate([
            jnp.concatenate(kb, axis=1),
            jnp.concatenate([c_in[blk, first_head + i].astype(BF16) for i in pair], axis=1),
            jnp.concatenate([jnp.broadcast_to(n_prev[i], (2 * SUBLANES, HEAD_DIM)).astype(BF16) for i in pair],
                            axis=1)], axis=0)
        first = _dot_nt(lhs, queries_of_pair([q[i].astype(BF16) for i in pair]))
        heads = [ml_head_single(proj, blk, first_head + i, i * HEAD_DIM, k[i], kb[i], n_prev[i],
                                first[:BLK, pair_cols(i)], first[BLK:2 * BLK, pair_cols(i)],
                                first[2 * BLK:2 * BLK + 1, pair_cols(i)]) for i in pair]
        for _ in range(3):
            for head in heads:
                next(head, None)
            yield

    def ml_head_single(proj, blk, head, col, k, kb, n_prev, s_t, qc_t, qn):
        rows = pl.ds(blk * BLK, BLK)
        v_t = proj[rows, 2 * GROUP_W + col:2 * GROUP_W + col + HEAD_DIM].T
        i_row = gate_t_ref[blk, head:head + 1, :]
        b_row = bcum_t_ref[blk, N_ML_HEADS + head:N_ML_HEADS + head + 1, :]
        key_term = gate_ref[rows, head:head + 1] - bcum_ref[rows, N_ML_HEADS + head:N_ML_HEADS + head + 1]
        m_prev = m_in[blk, head:head + 1, 0:1]
        logw_t = jnp.where(causal_t, b_row + key_term, -jnp.inf)
        m_t = jnp.maximum(b_row + m_prev, jnp.max(logw_t, axis=0, keepdims=True))
        w_t = jnp.exp(logw_t - m_t)
        inter_w = jnp.exp(b_row + m_prev - m_t)
        m_new = m_t[:, BLK - 1:BLK]
        b_last = b_row[:, BLK - 1:BLK]
        decay = jnp.exp(b_last + m_prev - m_new)
        wl = jnp.exp(b_last - b_row + i_row - m_new)
        m_out[blk, head:head + 1, :] = jnp.broadcast_to(m_new, (1, BLK))
        wk = _dot(jnp.broadcast_to(wl, (2 * SUBLANES, BLK)).astype(BF16), kb)[0:1, :]
        n_out[blk, head:head + 1, :] = decay * n_prev + wk
        kw_b = (k * jnp.exp(key_term + (b_last - m_new))).astype(BF16)
        v_tb = v_t.astype(BF16)
        yield
        s_t = s_t * w_t
        den = jnp.sum(s_t, axis=0, keepdims=True)
        both = _dot(v_tb, jnp.concatenate([kw_b, s_t.astype(BF16)], axis=1))
        c_out[blk, head] = decay * c_in[blk, head] + both[:, :HEAD_DIM]
        num_t = both[:, HEAD_DIM:]
        yield
        num_t = num_t + inter_w * qc_t
        den = jnp.maximum(jnp.abs(den + inter_w * qn), jnp.exp(-m_t))
        o = proj[rows, 3 * GROUP_W + col:3 * GROUP_W + col + HEAD_DIM]
        y = (channel_norm_t(num_t / den).T * g_ml_ref[:, head * HEAD_DIM:(head + 1) * HEAD_DIM]
             * jax.nn.sigmoid(o))
        mix_ref[rows, RET_W + head * HEAD_DIM:RET_W + (head + 1) * HEAD_DIM] = y.astype(BF16)

    n_ret_groups = N_RET_HEADS // HEADS_PER_GROUP
    n_groups = n_ret_groups + N_ML_HEADS // HEADS_PER_GROUP

    def first_head_of(grp):
        return (grp if grp < n_ret_groups else grp - n_ret_groups) * HEADS_PER_GROUP

    def project(grp, kind, rows_all):
        is_ret = grp < n_ret_groups
        c0 = (0 if is_ret else 4 * RET_W) + kind * (RET_W if is_ret else ML_W) + first_head_of(grp) * HEAD_DIM
        proj_ref[grp % 2, rows_all, kind * GROUP_W:(kind + 1) * GROUP_W] = _dot(
            h_ref[rows_all, :], w_in_ref[:, c0:c0 + GROUP_W])

    def out_project(grp, rows_all):
        cols = slice(grp * GROUP_W, (grp + 1) * GROUP_W)
        part = _dot(mix_ref[rows_all, cols], w_out_ref[cols, :])
        if grp == 0:
            mixed_ref[rows_all, :] = part
        else:
            mixed_ref[rows_all, :] += part

    def row_pipeline(blocks):
        rows_all = pl.ds(blocks[0] * BLK, len(blocks) * BLK)
        prologue(blocks, rows_all)
        for kind in range(4):
            project(0, kind, rows_all)
        yield
        for grp in range(n_groups):
            if n_seq == 1:
                pair_fn = ret_pair_single if grp < n_ret_groups else ml_pair_single
                heads = [pair_fn(proj_ref.at[grp % 2], blk, first_head_of(grp)) for blk in blocks]
            else:
                head_fn = ret_head if grp < n_ret_groups else ml_head
                heads = [head_fn(proj_ref.at[grp % 2], blk, first_head_of(grp) + hh, hh * HEAD_DIM)
                         for blk in blocks for hh in range(HEADS_PER_GROUP)]
            fillers = {0: [functools.partial(project, grp + 1, kind, rows_all) for kind in range(4)]
                       if grp + 1 < n_groups else [],
                       1: [functools.partial(out_project, grp - 1, rows_all)] if grp >= 1 else [],
                       2: []}
            for phase in range(3):
                todo = list(fillers[phase])
                every = max(1, len(heads) // max(1, len(todo)))
                for i, head in enumerate(heads):
                    next(head, None)
                    if todo and (i + 1) % every == 0:
                        todo.pop(0)()
                for filler in todo:
                    filler()
            yield
        last_cols = slice((n_groups - 1) * GROUP_W, n_groups * GROUP_W)
        last = _dot(mix_ref[rows_all, last_cols], w_out_ref[last_cols, :])
        for i, blk in enumerate(blocks):
            rows = pl.ds(blk * BLK, BLK)
            mixed = mixed_ref[rows, :] + last[i * BLK:(i + 1) * BLK]
            y_ref[blk] = x_ref[blk] + _rmsnorm(mixed, g_post_ref[...])

    n_runs = 2 if n_blk % 2 == 0 else 1
    per_run = n_blk // n_runs
    pipelines = [row_pipeline(list(range(i * per_run, (i + 1) * per_run))) for i in range(n_runs)]
    n_stages = n_groups + 2
    for step in range(n_stages + n_runs - 1):
        for lag, pipeline in enumerate(pipelines):
            if 0 <= step - lag < n_stages:
                next(pipeline, None)

    if fresh_state:
        @pl.when(pl.program_id(1) == pl.num_programs(1) - 1)
        def _():
            for blk in range(n_blk):
                for head in range(N_RET_HEADS):
                    s_out[blk, head] = st_ref[blk, head].T


def _ffn_kernel(*refs, seg, n_blk, fresh_state):
    n_seq_tile = n_blk * BLK // seg
    rows_tile = n_blk * BLK
    if fresh_state:
        (x_ref, g_pre_ref, w_up_ref, conv_w_ref, conv_b_ref, w_down_ref, g_post_ref,
         y_ref, buf_out, h_ref, act_ref, ffn_ref) = refs
        buf_in = buf_out

        @pl.when(pl.program_id(1) == 0)
        def _():
            buf_out[...] = jnp.zeros(buf_out.shape, F32)
    else:
        (x_ref, g_pre_ref, w_up_ref, conv_w_ref, conv_b_ref, w_down_ref, g_post_ref, buf_in,
         y_ref, buf_out, h_ref, act_ref, ffn_ref) = refs

    for blk in range(n_blk):
        rows = pl.ds(blk * BLK, BLK)
        h_ref[rows, :] = _rmsnorm(x_ref[blk], g_pre_ref[...]).astype(BF16)

    tpos = lax.broadcasted_iota(jnp.int32, (n_seq_tile, seg, FF_BLK), 1)

    def conv(cols):
        up = _dot(h_ref[...], w_up_ref[:, cols])
        prev0 = buf_in[:, 0:1, cols]
        prev1 = buf_in[:, 1:2, cols]
        up3 = up.reshape(n_seq_tile, seg, FF_BLK)
        sh1 = pltpu.roll(up, 1, axis=0).reshape(n_seq_tile, seg, FF_BLK)
        sh2 = pltpu.roll(up, 2, axis=0).reshape(n_seq_tile, seg, FF_BLK)
        sh1 = jnp.where(tpos == 0, prev1, sh1)
        sh2 = jnp.where(tpos == 0, prev0, jnp.where(tpos == 1, prev1, sh2))
        buf_out[:, 0:1, cols] = up3[:, seg - 2:seg - 1, :]
        buf_out[:, 1:2, cols] = up3[:, seg - 1:seg, :]
        w = conv_w_ref[:, cols]
        out = sh2 * w[0:1, :] + sh1 * w[1:2, :] + up3 * w[2:3, :] + conv_b_ref[:, cols]
        return out.reshape(rows_tile, FF_BLK)

    def activation(gate, val):
        c0 = -2.0 * math.log2(math.e) * math.sqrt(2.0 / math.pi)
        c1 = 0.044715 * c0
        return ((gate * val) / (1.0 + jnp.exp2(gate * (gate * gate * c1 + c0)))).astype(BF16)

    for j in range(D_FF // FF_BLK):
        cols = slice(j * FF_BLK, (j + 1) * FF_BLK)
        gate = conv(cols)
        val = conv(slice(D_FF + j * FF_BLK, D_FF + (j + 1) * FF_BLK))
        act_ref[:, cols] = activation(gate, val)

    ffn_ref[...] = _dot(act_ref[...], w_down_ref[...])
    for blk in range(n_blk):
        rows = pl.ds(blk * BLK, BLK)
        y_ref[blk] = x_ref[blk] + _rmsnorm(ffn_ref[rows, :], g_post_ref[...])


def _const_spec(shape):
    zeros = (0,) * len(shape)
    return pl.BlockSpec(shape, lambda *_: zeros, pipeline_mode=pl.Buffered(1))


def _compiler_params(n_grid_dims):
    return pltpu.CompilerParams(
        dimension_semantics=("arbitrary",) * n_grid_dims,
        vmem_limit_bytes=VMEM_LIMIT_BYTES)


def _mixer_scratch(n_blk, fresh_state):
    rows = n_blk * BLK
    carried = [pltpu.VMEM((n_blk, N_RET_HEADS, HEAD_DIM, HEAD_DIM), F32)] if fresh_state else []
    return [
        pltpu.VMEM((rows, D_MODEL), BF16),
        pltpu.VMEM((2, rows, 4 * GROUP_W), F32),
        pltpu.VMEM((rows, RET_W + ML_W), BF16),
        pltpu.VMEM((rows, D_MODEL), F32),
        pltpu.VMEM((rows, BLK), F32),
        pltpu.VMEM((rows, BLK), F32),
        pltpu.VMEM((n_blk, BLK, BLK), F32),
        pltpu.VMEM((n_blk, BLK, BLK), F32),
    ] + carried


def _mixer_weight_specs():
    return [
        _const_spec((1, D_MODEL)),
        _const_spec((D_MODEL, 4 * RET_W + 4 * ML_W + 2 * N_ML_HEADS)),
        _const_spec((D_MODEL, BLK)),
        _const_spec((1, BLK)),
        _const_spec((1, RET_W)),
        _const_spec((1, ML_W)),
        _const_spec((RET_W + ML_W, D_MODEL)),
        _const_spec((1, D_MODEL)),
    ]


def _state_shapes(n_seqs, n_blocks):
    return [
        jax.ShapeDtypeStruct((n_seqs, N_RET_HEADS, HEAD_DIM, HEAD_DIM), F32),
        jax.ShapeDtypeStruct((n_seqs, N_ML_HEADS, HEAD_DIM, HEAD_DIM), F32),
        jax.ShapeDtypeStruct((n_seqs, N_ML_HEADS, HEAD_DIM), F32),
        jax.ShapeDtypeStruct((n_blocks, N_ML_HEADS, BLK), F32),
    ]


def _state_specs(n_seq_tile, n_blk, index):
    return [
        pl.BlockSpec((n_seq_tile, N_RET_HEADS, HEAD_DIM, HEAD_DIM), lambda *g: (index(*g), 0, 0, 0)),
        pl.BlockSpec((n_seq_tile, N_ML_HEADS, HEAD_DIM, HEAD_DIM), lambda *g: (index(*g), 0, 0, 0)),
        pl.BlockSpec((n_seq_tile, N_ML_HEADS, HEAD_DIM), lambda *g: (index(*g), 0, 0)),
        pl.BlockSpec((n_blk, N_ML_HEADS, BLK), lambda *g: (index(*g), 0, 0)),
    ]


def _prompt_mixer(x, cos, sin, weights, n_blk):
    batch, seq_len, _ = x.shape
    grid = (batch // n_blk, seq_len // BLK)
    x_spec = pl.BlockSpec((n_blk, BLK, D_MODEL), lambda g, c: (g, c, 0))
    rope_spec = pl.BlockSpec((BLK, HEAD_DIM), lambda g, c: (c, 0))
    return pl.pallas_call(
        functools.partial(_mixer_kernel, seg=BLK, n_blk=n_blk, fresh_state=True),
        grid=grid,
        in_specs=[x_spec, rope_spec, rope_spec] + _mixer_weight_specs(),
        out_specs=[x_spec] + _state_specs(n_blk, n_blk, lambda g, c: g),
        out_shape=[jax.ShapeDtypeStruct(x.shape, F32)] + _state_shapes(batch, batch),
        scratch_shapes=_mixer_scratch(n_blk, fresh_state=True),
        compiler_params=_compiler_params(2),
        name="prompt_mixer",
    )(x, cos, sin, *weights)


def _sample_mixer(x, cos, sin, weights, states, seg, n_blk):
    n_blocks = x.shape[0]
    n_seq_tile = n_blk * BLK // seg
    grid = (n_blocks // n_blk,)
    x_spec = pl.BlockSpec((n_blk, BLK, D_MODEL), lambda g: (g, 0, 0))
    rope_spec = _const_spec((BLK, HEAD_DIM))
    state_specs = _state_specs(n_seq_tile, n_blk, lambda g: g)
    return pl.pallas_call(
        functools.partial(_mixer_kernel, seg=seg, n_blk=n_blk, fresh_state=False),
        grid=grid,
        in_specs=[x_spec, rope_spec, rope_spec] + _mixer_weight_specs() + state_specs,
        out_specs=[x_spec] + state_specs,
        out_shape=[jax.ShapeDtypeStruct(x.shape, F32)]
        + _state_shapes(n_blocks * BLK // seg, n_blocks),
        scratch_shapes=_mixer_scratch(n_blk, fresh_state=False),
        compiler_params=_compiler_params(1),
        name="sample_mixer",
    )(x, cos, sin, *weights, *states)


def _ffn_scratch(n_blk):
    rows = n_blk * BLK
    return [
        pltpu.VMEM((rows, D_MODEL), BF16),
        pltpu.VMEM((rows, D_FF), BF16),
        pltpu.VMEM((rows, D_MODEL), F32),
    ]


def _ffn_weight_specs():
    return [
        _const_spec((1, D_MODEL)),
        _const_spec((D_MODEL, 2 * D_FF)),
        _const_spec((CONV_W, 2 * D_FF)),
        _const_spec((1, 2 * D_FF)),
        _const_spec((D_FF, D_MODEL)),
        _const_spec((1, D_MODEL)),
    ]


def _prompt_ffn(x, weights, n_blk):
    batch, seq_len, _ = x.shape
    grid = (batch // n_blk, seq_len // BLK)
    x_spec = pl.BlockSpec((n_blk, BLK, D_MODEL), lambda g, c: (g, c, 0))
    buf_spec = pl.BlockSpec((n_blk, CONV_W - 1, 2 * D_FF), lambda g, c: (g, 0, 0))
    return pl.pallas_call(
        functools.partial(_ffn_kernel, seg=BLK, n_blk=n_blk, fresh_state=True),
        grid=grid,
        in_specs=[x_spec] + _ffn_weight_specs(),
        out_specs=[x_spec, buf_spec],
        out_shape=[jax.ShapeDtypeStruct(x.shape, F32),
                   jax.ShapeDtypeStruct((batch, CONV_W - 1, 2 * D_FF), F32)],
        scratch_shapes=_ffn_scratch(n_blk),
        compiler_params=_compiler_params(2),
        name="prompt_ffn",
    )(x, *weights)


def _sample_ffn(x, weights, conv_buf, seg, n_blk):
    n_blocks = x.shape[0]
    n_seq_tile = n_blk * BLK // seg
    grid = (n_blocks // n_blk,)
    x_spec = pl.BlockSpec((n_blk, BLK, D_MODEL), lambda g: (g, 0, 0))
    buf_spec = pl.BlockSpec((n_seq_tile, CONV_W - 1, 2 * D_FF), lambda g: (g, 0, 0))
    return pl.pallas_call(
        functools.partial(_ffn_kernel, seg=seg, n_blk=n_blk, fresh_state=False),
        grid=grid,
        in_specs=[x_spec] + _ffn_weight_specs() + [buf_spec],
        out_specs=[x_spec, buf_spec],
        out_shape=[jax.ShapeDtypeStruct(x.shape, F32),
                   jax.ShapeDtypeStruct(conv_buf.shape, F32)],
        scratch_shapes=_ffn_scratch(n_blk),
        compiler_params=_compiler_params(1),
        name="sample_ffn",
    )(x, *weights, conv_buf)


def _rope_tables(pos):
    freqs = ROPE_BASE ** (-jnp.arange(0, HEAD_DIM, 2, dtype=F32) / HEAD_DIM)
    ang = pos.astype(F32)[:, None] * freqs[None, :]
    cos, sin = jnp.cos(ang), jnp.sin(ang)
    return jnp.concatenate([cos, cos], axis=-1), jnp.concatenate([-sin, sin], axis=-1)


def kernel(x_prompt, x_sample, state_ret, state_mlstm_C, state_mlstm_n, state_mlstm_m, cache_ffn_conv, pre_mix_gain, w_in, b_gates, ret_head_gain, mlstm_head_gain, w_out, post_mix_gain, pre_ffn_gain, w_up, conv_w, conv_b, w_down, post_ffn_gain):
    depth = w_in.shape[0]
    assert depth == 1
    batch, seq_len, _ = x_prompt.shape
    dec_batch, dec_seq, _ = x_sample.shape
    assert seq_len % BLK == 0 and BLK % dec_seq == 0 and (dec_batch * dec_seq) % BLK == 0
    seqs_per_blk = BLK // dec_seq
    n_sample_blocks = dec_batch * dec_seq // BLK
    layer = 0

    n_gate = 2 * N_ML_HEADS
    w_gate = jnp.pad(w_in[layer][:, -n_gate:], ((0, 0), (0, BLK - n_gate))).astype(BF16)
    b_gate = jnp.pad(b_gates[layer], (0, BLK - n_gate)).reshape(1, BLK)
    mixer_weights = (
        pre_mix_gain[layer].reshape(1, D_MODEL), w_in[layer].astype(BF16), w_gate, b_gate,
        ret_head_gain[layer].reshape(1, RET_W), mlstm_head_gain[layer].reshape(1, ML_W),
        w_out[layer].astype(BF16), post_mix_gain[layer].reshape(1, D_MODEL))
    ffn_weights = (
        pre_ffn_gain[layer].reshape(1, D_MODEL), w_up[layer].astype(BF16), conv_w[layer],
        conv_b[layer].reshape(1, 2 * D_FF), w_down[layer].astype(BF16),
        post_ffn_gain[layer].reshape(1, D_MODEL))

    cos_p, sin_p = _rope_tables(jnp.arange(seq_len, dtype=jnp.int32))
    assert batch % PROMPT_BLOCKS_PER_STEP == 0
    x1_p, s_p, c_p, n_p, m_p = _prompt_mixer(x_prompt, cos_p, sin_p, mixer_weights, n_blk=PROMPT_BLOCKS_PER_STEP)
    y_p, buf_p = _prompt_ffn(x1_p, ffn_weights, n_blk=PROMPT_BLOCKS_PER_STEP)
    m_p = m_p[:, :, 0]

    pos_s = PAST_LEN + jnp.arange(dec_seq, dtype=jnp.int32)
    cos_s, sin_s = _rope_tables(jnp.tile(pos_s, seqs_per_blk))
    m_blocks = jnp.transpose(state_mlstm_m[layer].reshape(n_sample_blocks, seqs_per_blk, N_ML_HEADS), (0, 2, 1))
    m_blocks = jnp.pad(m_blocks, ((0, 0), (0, 0), (0, BLK - seqs_per_blk)))
    xs = x_sample.reshape(n_sample_blocks, BLK, D_MODEL)
    x1_s, s_s, c_s, n_s, m_s = _sample_mixer(
        xs, cos_s, sin_s, mixer_weights,
        (state_ret[layer], state_mlstm_C[layer], state_mlstm_n[layer], m_blocks),
        seg=dec_seq, n_blk=SAMPLE_MIXER_BLOCKS_PER_STEP)
    y_s, buf_s = _sample_ffn(x1_s, ffn_weights, cache_ffn_conv[layer], seg=dec_seq,
                             n_blk=SAMPLE_FFN_BLOCKS_PER_STEP)
    y_s = y_s.reshape(x_sample.shape)
    m_s = jnp.transpose(m_s[:, :, :seqs_per_blk], (0, 2, 1)).reshape(dec_batch, N_ML_HEADS)

    return (y_p, y_s, s_p[None], s_s[None], c_p[None], c_s[None], n_p[None], n_s[None],
            m_p[None], m_s[None], buf_p[None], buf_s[None])
```

```python
import functools
import math

import jax
import jax.numpy as jnp
from jax import lax
from jax.experimental import pallas as pl
from jax.experimental.pallas import tpu as pltpu

D_MODEL = 1024
HEAD_DIM = 128
N_RET_HEADS = 4
N_ML_HEADS = 4
RET_W = N_RET_HEADS * HEAD_DIM
ML_W = N_ML_HEADS * HEAD_DIM
D_FF = 2816
CONV_W = 3
PAST_LEN = 16384
ROPE_BASE = 10000.0
EPS = 1e-6
M_INIT = -1e30

MXU_WIDTH = 256
SUBLANES = 8
V7X_VMEM_BYTES = 64 * 1024 * 1024
VMEM_LIMIT_BYTES = V7X_VMEM_BYTES - 8 * 1024 * 1024

BLK = 128
HEADS_PER_GROUP = MXU_WIDTH // HEAD_DIM
GROUP_W = HEADS_PER_GROUP * HEAD_DIM
FF_BLK = MXU_WIDTH

PROMPT_BLOCKS_PER_STEP = 8
SAMPLE_MIXER_BLOCKS_PER_STEP = 1
SAMPLE_FFN_BLOCKS_PER_STEP = 2

F32 = jnp.float32
BF16 = jnp.bfloat16


def _dot(a, b, precision=None):
    return jnp.dot(a, b, preferred_element_type=F32, precision=precision)


def _dot_nt(a, b):
    return lax.dot_general(a, b, (((1,), (1,)), ((), ())), preferred_element_type=F32)


def _rmsnorm(x, g):
    return x * lax.rsqrt(jnp.mean(x * x, axis=-1, keepdims=True) + EPS) * g


def _groupnorm(h, g):
    mu = jnp.mean(h, axis=-1, keepdims=True)
    d = h - mu
    var = jnp.mean(d * d, axis=-1, keepdims=True)
    return d * lax.rsqrt(var + EPS) * g


def _ret_log_gamma(h):
    return math.log(1.0 - 2.0 ** (-5.0 - h))


class _Masks:
    def __init__(self, seg):
        shift = seg.bit_length() - 1
        r = lax.broadcasted_iota(jnp.int32, (BLK, BLK), 0)
        c = lax.broadcasted_iota(jnp.int32, (BLK, BLK), 1)
        rseq = r >> shift
        self.causal = (rseq == (c >> shift)) & (r >= c)
        self.tpos = (r & (seg - 1)).astype(F32)
        self.diff = (r - c).astype(F32)
        self.col_is_last_of_rowseq = c == (rseq * seg + (seg - 1))
        self.col_is_rowseq = c == rseq
        self.row_is_last_of_colseq = r == (c * seg + (seg - 1))


def _rows_from_seq(mask_col_is_rowseq, seq_row):
    return jnp.sum(jnp.where(mask_col_is_rowseq, seq_row, 0.0), axis=1, keepdims=True)


def _seq_from_last_row(mask_row_is_last_of_colseq, col):
    return jnp.sum(jnp.where(mask_row_is_last_of_colseq, col, 0.0), axis=0, keepdims=True)


def _mixer_kernel(*refs, seg, n_blk, fresh_state):
    n_seq = BLK // seg
    assert fresh_state == (n_seq == 1)
    if fresh_state:
        (x_ref, cos_ref, sin_ref, g_pre_ref, w_in_ref, w_gate_ref, b_gate_ref, g_ret_ref, g_ml_ref,
         w_out_ref, g_post_ref,
         y_ref, s_out, c_out, n_out, m_out,
         h_ref, proj_ref, mix_ref, mixed_ref, gate_ref, bcum_ref, gate_t_ref, bcum_t_ref, st_ref) = refs
        c_in, n_in, m_in = c_out, n_out, m_out

        @pl.when(pl.program_id(1) == 0)
        def _():
            st_ref[...] = jnp.zeros(st_ref.shape, F32)
            c_out[...] = jnp.zeros(c_out.shape, F32)
            n_out[...] = jnp.zeros(n_out.shape, F32)
            m_out[...] = jnp.full(m_out.shape, M_INIT, F32)
    else:
        (x_ref, cos_ref, sin_ref, g_pre_ref, w_in_ref, w_gate_ref, b_gate_ref, g_ret_ref, g_ml_ref,
         w_out_ref, g_post_ref, s_in, c_in, n_in, m_in,
         y_ref, s_out, c_out, n_out, m_out,
         h_ref, proj_ref, mix_ref, mixed_ref, gate_ref, bcum_ref, gate_t_ref, bcum_t_ref) = refs

    masks = _Masks(seg)
    scale = HEAD_DIM ** -0.5
    tri = jnp.where(masks.causal, 1.0, 0.0).astype(F32)
    cos = cos_ref[...]
    sin = sin_ref[...]

    def pre_norm(blocks):
        for blk in blocks:
            rows = pl.ds(blk * BLK, BLK)
            h_ref[rows, :] = _rmsnorm(x_ref[blk], g_pre_ref[...]).astype(BF16)

    def gate_projection(rows_all):
        gate_ref[rows_all, :] = _dot(h_ref[rows_all, :], w_gate_ref[...]) + b_gate_ref[...]

    def gate_cumsums(blocks):
        for blk in blocks:
            rows = pl.ds(blk * BLK, BLK)
            gates = gate_ref[rows, :]
            bcum = _dot(tri, jax.nn.log_sigmoid(gates), precision=lax.Precision.HIGHEST)
            bcum_ref[rows, :] = bcum
            gate_t_ref[blk] = gates.T
            bcum_t_ref[blk] = bcum.T

    def rotary(t):
        return t * cos + pltpu.roll(t, HEAD_DIM // 2, axis=1) * sin

    def seq_rows(j):
        return slice(j * seg, (j + 1) * seg)

    def lane_mask(j):
        c = lax.broadcasted_iota(jnp.int32, (BLK, BLK), 1)
        return (c >= j * seg) & (c < (j + 1) * seg)

    def ret_head(proj, blk, head, col):
        rows = pl.ds(blk * BLK, BLK)
        lg = _ret_log_gamma(head)
        q = rotary(proj[rows,col:col + HEAD_DIM])
        k = rotary(proj[rows,GROUP_W + col:GROUP_W + col + HEAD_DIM]) * scale
        v = proj[rows,2 * GROUP_W + col:2 * GROUP_W + col + HEAD_DIM]
        g = proj[rows,3 * GROUP_W + col:3 * GROUP_W + col + HEAD_DIM]
        qb, kb, vb = q.astype(BF16), k.astype(BF16), v.astype(BF16)
        dmat = jnp.where(masks.causal, jnp.exp(jnp.where(masks.causal, masks.diff * lg, 0.0)), 0.0)
        xi = jnp.exp((masks.tpos + 1.0) * lg)
        zeta = jnp.exp((float(seg - 1) - masks.tpos) * lg)
        chunk_decay = math.exp(seg * lg)
        scores = _dot_nt(qb, kb)
        kz_t = (k * zeta).T
        inter = []
        kz_tb = kz_t.astype(BF16)
        updates = _dot(jnp.concatenate(
            [jnp.where(lane_mask(j), kz_tb, jnp.zeros_like(kz_tb)) for j in range(n_seq)], axis=0), vb)
        for j in range(n_seq):
            seq = blk * n_seq + j
            s_prev = s_in[seq, head]
            inter.append(_dot(q[seq_rows(j)].astype(BF16), s_prev.astype(BF16)))
            s_out[seq, head] = s_prev * chunk_decay + updates[j * HEAD_DIM:(j + 1) * HEAD_DIM]
        inter = jnp.concatenate(inter, axis=0)
        yield
        out = _dot((scores * dmat).astype(BF16), vb)
        yield
        out = out + inter * xi
        y = _groupnorm(out, g_ret_ref[:, head * HEAD_DIM:(head + 1) * HEAD_DIM]) * (g * jax.nn.sigmoid(g))
        mix_ref[rows, head * HEAD_DIM:(head + 1) * HEAD_DIM] = y.astype(BF16)

    def ml_head(proj, blk, head, col):
        rows = pl.ds(blk * BLK, BLK)
        q = proj[rows,col:col + HEAD_DIM]
        k = proj[rows,GROUP_W + col:GROUP_W + col + HEAD_DIM] * scale
        v = proj[rows,2 * GROUP_W + col:2 * GROUP_W + col + HEAD_DIM]
        o = proj[rows,3 * GROUP_W + col:3 * GROUP_W + col + HEAD_DIM]
        qb, kb, vb = q.astype(BF16), k.astype(BF16), v.astype(BF16)
        i_col = gate_ref[rows, head:head + 1]
        b_col = bcum_ref[rows, N_ML_HEADS + head:N_ML_HEADS + head + 1]
        i_row = gate_t_ref[blk, head:head + 1, :]
        b_row = bcum_t_ref[blk, N_ML_HEADS + head:N_ML_HEADS + head + 1, :]
        m_prev_seq = m_in[blk, head:head + 1, :]
        m_prev = _rows_from_seq(masks.col_is_rowseq, m_prev_seq)
        logw = jnp.where(masks.causal, b_col - b_row + i_row, -jnp.inf)
        m_t = jnp.maximum(b_col + m_prev, jnp.max(logw, axis=1, keepdims=True))
        w = jnp.exp(logw - m_t)
        inter_w = jnp.exp(b_col + m_prev - m_t)
        s = _dot_nt(qb, kb)
        m_new_seq = _seq_from_last_row(masks.row_is_last_of_colseq, m_t)
        b_last_seq = _seq_from_last_row(masks.row_is_last_of_colseq, b_col)
        decay_seq = jnp.exp(b_last_seq + m_prev_seq - m_new_seq)
        m_new = _rows_from_seq(masks.col_is_rowseq, m_new_seq)
        b_last = jnp.sum(jnp.where(masks.col_is_last_of_rowseq, b_row, 0.0), axis=1, keepdims=True)
        wl = jnp.exp(b_last - b_col + i_col - m_new)
        vw_t = (v * wl).T
        kw = k * wl
        qc, qn = [], []
        for j in range(n_seq):
            seq = blk * n_seq + j
            c_prev = c_in[seq, head]
            n_prev = n_in[seq, head:head + 1, :]
            qc.append(_dot_nt(q[seq_rows(j)].astype(BF16), c_prev.astype(BF16)))
            qn.append(jnp.sum(q[seq_rows(j)] * n_prev, axis=1, keepdims=True))
            decay = decay_seq[:, j:j + 1]
            n_out[seq, head:head + 1, :] = decay * n_prev + jnp.sum(kw[seq_rows(j)], axis=0, keepdims=True)
        qc = jnp.concatenate(qc, axis=0)
        qn = jnp.concatenate(qn, axis=0)
        m_out[blk, head:head + 1, :] = m_new_seq
        yield
        s = s * w
        num = _dot(s.astype(BF16), vb)
        den = jnp.sum(s, axis=1, keepdims=True)
        vw_tb = vw_t.astype(BF16)
        updates = _dot(jnp.concatenate(
            [jnp.where(lane_mask(j), vw_tb, jnp.zeros_like(vw_tb)) for j in range(n_seq)], axis=0), kb)
        for j in range(n_seq):
            seq = blk * n_seq + j
            c_out[seq, head] = (decay_seq[:, j:j + 1] * c_in[seq, head]
                                + updates[j * HEAD_DIM:(j + 1) * HEAD_DIM])
        yield
        num = num + inter_w * qc
        den = jnp.maximum(jnp.abs(den + inter_w * qn), jnp.exp(-m_t))
        hout = num / den
        y = _groupnorm(hout, g_ml_ref[:, head * HEAD_DIM:(head + 1) * HEAD_DIM]) * jax.nn.sigmoid(o)
        mix_ref[rows, RET_W + head * HEAD_DIM:RET_W + (head + 1) * HEAD_DIM] = y.astype(BF16)

    r_idx = lax.broadcasted_iota(jnp.int32, (BLK, BLK), 0)
    c_idx = lax.broadcasted_iota(jnp.int32, (BLK, BLK), 1)
    causal_t = r_idx <= c_idx
    lane_pos = lax.broadcasted_iota(jnp.int32, (1, BLK), 1).astype(F32)
    head_consts = {}

    def ret_consts(head):
        if head not in head_consts:
            lg = _ret_log_gamma(head)
            dmat_t = jnp.where(causal_t, jnp.exp(jnp.where(causal_t, (c_idx - r_idx).astype(F32) * lg, 0.0)), 0.0)
            xi_row = jnp.exp((lane_pos + 1.0) * lg)
            zeta = jnp.exp((float(BLK - 1) - r_idx.astype(F32)) * lg)
            head_consts[head] = (dmat_t, xi_row, zeta, math.exp(BLK * lg))
        return head_consts[head]

    def channel_norm_t(h_t):
        mu = jnp.mean(h_t, axis=0, keepdims=True)
        d = h_t - mu
        var = jnp.mean(d * d, axis=0, keepdims=True)
        return d * lax.rsqrt(var + EPS)

    def queries_of_pair(q_pair):
        zero = jnp.zeros_like(q_pair[0])
        return jnp.concatenate([jnp.concatenate([q_pair[0], zero], axis=1),
                                jnp.concatenate([zero, q_pair[1]], axis=1)], axis=0)

    def pair_cols(i):
        return slice(i * HEAD_DIM, (i + 1) * HEAD_DIM)

    def ret_pair_single(proj, blk, first_head):
        rows = pl.ds(blk * BLK, BLK)
        pair = range(HEADS_PER_GROUP)
        consts = [ret_consts(first_head + i) for i in pair]
        q = [rotary(proj[rows, pair_cols(i)]) for i in pair]
        k = [rotary(proj[rows, GROUP_W + i * HEAD_DIM:GROUP_W + (i + 1) * HEAD_DIM]) * scale for i in pair]
        v_tb = [proj[rows, 2 * GROUP_W + i * HEAD_DIM:2 * GROUP_W + (i + 1) * HEAD_DIM].T.astype(BF16)
                for i in pair]
        lhs = jnp.concatenate([
            jnp.concatenate([k[i].astype(BF16) for i in pair], axis=1),
            jnp.concatenate([st_ref[blk, first_head + i].astype(BF16) for i in pair], axis=1)], axis=0)
        first = _dot_nt(lhs, queries_of_pair([q[i].astype(BF16) for i in pair]))
        kz_b = [(k[i] * consts[i][2]).astype(BF16) for i in pair]
        yield
        out_t = []
        for i in pair:
            dmat_t, _, _, chunk_decay = consts[i]
            scores_t = first[:BLK, pair_cols(i)]
            both = _dot(v_tb[i], jnp.concatenate([kz_b[i], (scores_t * dmat_t).astype(BF16)], axis=1))
            st_ref[blk, first_head + i] = st_ref[blk, first_head + i] * chunk_decay + both[:, :HEAD_DIM]
            out_t.append(both[:, HEAD_DIM:])
        yield
        for i in pair:
            head = first_head + i
            out = out_t[i] + first[BLK:, pair_cols(i)] * consts[i][1]
            g = proj[rows, 3 * GROUP_W + i * HEAD_DIM:3 * GROUP_W + (i + 1) * HEAD_DIM]
            y = channel_norm_t(out).T * g_ret_ref[:, head * HEAD_DIM:(head + 1) * HEAD_DIM] * (g * jax.nn.sigmoid(g))
            mix_ref[rows, head * HEAD_DIM:(head + 1) * HEAD_DIM] = y.astype(BF16)

    def ml_pair_single(proj, blk, first_head):
        rows = pl.ds(blk * BLK, BLK)
        pair = range(HEADS_PER_GROUP)
        q = [proj[rows, pair_cols(i)] for i in pair]
        k = [proj[rows, GROUP_W + i * HEAD_DIM:GROUP_W + (i + 1) * HEAD_DIM] * scale for i in pair]
        kb = [k[i].astype(BF16) for i in pair]
        n_prev = [n_in[blk, first_head + i:first_head + i + 1, :] for i in pair]
        lhs = jnp.concatenate([
            jnp.concatenate(kb, axis=1),
            jnp.concatenate([c_in[blk, first_head + i].astype(BF16) for i in pair], axis=1),
            jnp.concatenate([jnp.broadcast_to(n_prev[i], (2 * SUBLANES, HEAD_DIM)).astype(BF16) for i in pair],
                            axis=1)], axis=0)
        first = _dot_nt(lhs, queries_of_pair([q[i].astype(BF16) for i in pair]))
        heads = [ml_head_single(proj, blk, first_head + i, i * HEAD_DIM, k[i], kb[i], n_prev[i],
                                first[:BLK, pair_cols(i)], first[BLK:2 * BLK, pair_cols(i)],
                                first[2 * BLK:2 * BLK + 1, pair_cols(i)]) for i in pair]
        for _ in range(3):
            for head in heads:
                next(head, None)
            yield

    def ml_head_single(proj, blk, head, col, k, kb, n_prev, s_t, qc_t, qn):
        rows = pl.ds(blk * BLK, BLK)
        v_t = proj[rows, 2 * GROUP_W + col:2 * GROUP_W + col + HEAD_DIM].T
        i_row = gate_t_ref[blk, head:head + 1, :]
        b_row = bcum_t_ref[blk, N_ML_HEADS + head:N_ML_HEADS + head + 1, :]
        key_term = gate_ref[rows, head:head + 1] - bcum_ref[rows, N_ML_HEADS + head:N_ML_HEADS + head + 1]
        m_prev = m_in[blk, head:head + 1, 0:1]
        logw_t = jnp.where(causal_t, b_row + key_term, -jnp.inf)
        m_t = jnp.maximum(b_row + m_prev, jnp.max(logw_t, axis=0, keepdims=True))
        w_t = jnp.exp(logw_t - m_t)
        inter_w = jnp.exp(b_row + m_prev - m_t)
        m_new = m_t[:, BLK - 1:BLK]
        b_last = b_row[:, BLK - 1:BLK]
        decay = jnp.exp(b_last + m_prev - m_new)
        wl = jnp.exp(b_last - b_row + i_row - m_new)
        m_out[blk, head:head + 1, :] = jnp.broadcast_to(m_new, (1, BLK))
        wk = _dot(jnp.broadcast_to(wl, (2 * SUBLANES, BLK)).astype(BF16), kb)[0:1, :]
        n_out[blk, head:head + 1, :] = decay * n_prev + wk
        kw_b = (k * jnp.exp(key_term + (b_last - m_new))).astype(BF16)
        v_tb = v_t.astype(BF16)
        yield
        s_t = s_t * w_t
        den = jnp.sum(s_t, axis=0, keepdims=True)
        both = _dot(v_tb, jnp.concatenate([kw_b, s_t.astype(BF16)], axis=1))
        c_out[blk, head] = decay * c_in[blk, head] + both[:, :HEAD_DIM]
        num_t = both[:, HEAD_DIM:]
        yield
        num_t = num_t + inter_w * qc_t
        den = jnp.maximum(jnp.abs(den + inter_w * qn), jnp.exp(-m_t))
        o = proj[rows, 3 * GROUP_W + col:3 * GROUP_W + col + HEAD_DIM]
        y = (channel_norm_t(num_t / den).T * g_ml_ref[:, head * HEAD_DIM:(head + 1) * HEAD_DIM]
             * jax.nn.sigmoid(o))
        mix_ref[rows, RET_W + head * HEAD_DIM:RET_W + (head + 1) * HEAD_DIM] = y.astype(BF16)

    n_ret_groups = N_RET_HEADS // HEADS_PER_GROUP
    n_groups = n_ret_groups + N_ML_HEADS // HEADS_PER_GROUP

    def first_head_of(grp):
        return (grp if grp < n_ret_groups else grp - n_ret_groups) * HEADS_PER_GROUP

    def project(grp, kind, rows_all):
        is_ret = grp < n_ret_groups
        c0 = (0 if is_ret else 4 * RET_W) + kind * (RET_W if is_ret else ML_W) + first_head_of(grp) * HEAD_DIM
        proj_ref[grp % 2, rows_all, kind * GROUP_W:(kind + 1) * GROUP_W] = _dot(
            h_ref[rows_all, :], w_in_ref[:, c0:c0 + GROUP_W])

    def out_project(grp, rows_all):
        cols = slice(grp * GROUP_W, (grp + 1) * GROUP_W)
        part = _dot(mix_ref[rows_all, cols], w_out_ref[cols, :])
        if grp == 0:
            mixed_ref[rows_all, :] = part
        else:
            mixed_ref[rows_all, :] += part

    def row_pipeline(blocks):
        rows_all = pl.ds(blocks[0] * BLK, len(blocks) * BLK)
        pre_norm(blocks)
        for kind in range(4):
            project(0, kind, rows_all)
        yield
        for grp in range(n_groups):
            if n_seq == 1:
                pair_fn = ret_pair_single if grp < n_ret_groups else ml_pair_single
                heads = [pair_fn(proj_ref.at[grp % 2], blk, first_head_of(grp)) for blk in blocks]
            else:
                head_fn = ret_head if grp < n_ret_groups else ml_head
                heads = [head_fn(proj_ref.at[grp % 2], blk, first_head_of(grp) + hh, hh * HEAD_DIM)
                         for blk in blocks for hh in range(HEADS_PER_GROUP)]
            fillers = {0: [functools.partial(project, grp + 1, kind, rows_all) for kind in range(4)]
                       if grp + 1 < n_groups else [],
                       1: [functools.partial(out_project, grp - 1, rows_all)] if grp >= 1 else [],
                       2: []}
            if grp == 0:
                fillers[0].append(functools.partial(gate_projection, rows_all))
                fillers[1].append(functools.partial(gate_cumsums, blocks))
            for phase in range(3):
                todo = list(fillers[phase])
                every = max(1, len(heads) // max(1, len(todo)))
                for i, head in enumerate(heads):
                    next(head, None)
                    if todo and (i + 1) % every == 0:
                        todo.pop(0)()
                for filler in todo:
                    filler()
            yield
        last_cols = slice((n_groups - 1) * GROUP_W, n_groups * GROUP_W)
        last = _dot(mix_ref[rows_all, last_cols], w_out_ref[last_cols, :])
        for i, blk in enumerate(blocks):
            rows = pl.ds(blk * BLK, BLK)
            mixed = mixed_ref[rows, :] + last[i * BLK:(i + 1) * BLK]
            y_ref[blk] = x_ref[blk] + _rmsnorm(mixed, g_post_ref[...])

    n_runs = 2 if n_blk % 2 == 0 else 1
    per_run = n_blk // n_runs
    pipelines = [row_pipeline(list(range(i * per_run, (i + 1) * per_run))) for i in range(n_runs)]
    n_stages = n_groups + 2
    for step in range(n_stages + n_runs - 1):
        for lag, pipeline in enumerate(pipelines):
            if 0 <= step - lag < n_stages:
                next(pipeline, None)

    if fresh_state:
        @pl.when(pl.program_id(1) == pl.num_programs(1) - 1)
        def _():
            for blk in range(n_blk):
                for head in range(N_RET_HEADS):
                    s_out[blk, head] = st_ref[blk, head].T


def _ffn_kernel(*refs, seg, n_blk, fresh_state):
    n_seq_tile = n_blk * BLK // seg
    rows_tile = n_blk * BLK
    if fresh_state:
        (x_ref, g_pre_ref, w_up_ref, conv_w_ref, conv_b_ref, w_down_ref, g_post_ref,
         y_ref, buf_out, h_ref, act_ref, ffn_ref) = refs
        buf_in = buf_out

        @pl.when(pl.program_id(1) == 0)
        def _():
            buf_out[...] = jnp.zeros(buf_out.shape, F32)
    else:
        (x_ref, g_pre_ref, w_up_ref, conv_w_ref, conv_b_ref, w_down_ref, g_post_ref, buf_in,
         y_ref, buf_out, h_ref, act_ref, ffn_ref) = refs

    for blk in range(n_blk):
        rows = pl.ds(blk * BLK, BLK)
        h_ref[rows, :] = _rmsnorm(x_ref[blk], g_pre_ref[...]).astype(BF16)

    tpos = lax.broadcasted_iota(jnp.int32, (n_seq_tile, seg, FF_BLK), 1)

    def conv(cols):
        up = _dot(h_ref[...], w_up_ref[:, cols])
        prev0 = buf_in[:, 0:1, cols]
        prev1 = buf_in[:, 1:2, cols]
        up3 = up.reshape(n_seq_tile, seg, FF_BLK)
        sh1 = pltpu.roll(up, 1, axis=0).reshape(n_seq_tile, seg, FF_BLK)
        sh2 = pltpu.roll(up, 2, axis=0).reshape(n_seq_tile, seg, FF_BLK)
        sh1 = jnp.where(tpos == 0, prev1, sh1)
        sh2 = jnp.where(tpos == 0, prev0, jnp.where(tpos == 1, prev1, sh2))
        buf_out[:, 0:1, cols] = up3[:, seg - 2:seg - 1, :]
        buf_out[:, 1:2, cols] = up3[:, seg - 1:seg, :]
        w = conv_w_ref[:, cols]
        out = sh2 * w[0:1, :] + sh1 * w[1:2, :] + up3 * w[2:3, :] + conv_b_ref[:, cols]
        return out.reshape(rows_tile, FF_BLK)

    def activation(gate, val):
        c0 = -2.0 * math.log2(math.e) * math.sqrt(2.0 / math.pi)
        c1 = 0.044715 * c0
        return ((gate * val) / (1.0 + jnp.exp2(gate * (gate * gate * c1 + c0)))).astype(BF16)

    for j in range(D_FF // FF_BLK):
        cols = slice(j * FF_BLK, (j + 1) * FF_BLK)
        gate = conv(cols)
        val = conv(slice(D_FF + j * FF_BLK, D_FF + (j + 1) * FF_BLK))
        act_ref[:, cols] = activation(gate, val)

    ffn_ref[...] = _dot(act_ref[...], w_down_ref[...])
    for blk in range(n_blk):
        rows = pl.ds(blk * BLK, BLK)
        y_ref[blk] = x_ref[blk] + _rmsnorm(ffn_ref[rows, :], g_post_ref[...])


def _const_spec(shape):
    zeros = (0,) * len(shape)
    return pl.BlockSpec(shape, lambda *_: zeros, pipeline_mode=pl.Buffered(1))


def _compiler_params(n_grid_dims):
    return pltpu.CompilerParams(
        dimension_semantics=("arbitrary",) * n_grid_dims,
        vmem_limit_bytes=VMEM_LIMIT_BYTES)


def _mixer_scratch(n_blk, fresh_state):
    rows = n_blk * BLK
    carried = [pltpu.VMEM((n_blk, N_RET_HEADS, HEAD_DIM, HEAD_DIM), F32)] if fresh_state else []
    return [
        pltpu.VMEM((rows, D_MODEL), BF16),
        pltpu.VMEM((2, rows, 4 * GROUP_W), F32),
        pltpu.VMEM((rows, RET_W + ML_W), BF16),
        pltpu.VMEM((rows, D_MODEL), F32),
        pltpu.VMEM((rows, BLK), F32),
        pltpu.VMEM((rows, BLK), F32),
        pltpu.VMEM((n_blk, BLK, BLK), F32),
        pltpu.VMEM((n_blk, BLK, BLK), F32),
    ] + carried


def _mixer_weight_specs():
    return [
        _const_spec((1, D_MODEL)),
        _const_spec((D_MODEL, 4 * RET_W + 4 * ML_W + 2 * N_ML_HEADS)),
        _const_spec((D_MODEL, BLK)),
        _const_spec((1, BLK)),
        _const_spec((1, RET_W)),
        _const_spec((1, ML_W)),
        _const_spec((RET_W + ML_W, D_MODEL)),
        _const_spec((1, D_MODEL)),
    ]


def _state_shapes(n_seqs, n_blocks):
    return [
        jax.ShapeDtypeStruct((n_seqs, N_RET_HEADS, HEAD_DIM, HEAD_DIM), F32),
        jax.ShapeDtypeStruct((n_seqs, N_ML_HEADS, HEAD_DIM, HEAD_DIM), F32),
        jax.ShapeDtypeStruct((n_seqs, N_ML_HEADS, HEAD_DIM), F32),
        jax.ShapeDtypeStruct((n_blocks, N_ML_HEADS, BLK), F32),
    ]


def _state_specs(n_seq_tile, n_blk, index):
    return [
        pl.BlockSpec((n_seq_tile, N_RET_HEADS, HEAD_DIM, HEAD_DIM), lambda *g: (index(*g), 0, 0, 0)),
        pl.BlockSpec((n_seq_tile, N_ML_HEADS, HEAD_DIM, HEAD_DIM), lambda *g: (index(*g), 0, 0, 0)),
        pl.BlockSpec((n_seq_tile, N_ML_HEADS, HEAD_DIM), lambda *g: (index(*g), 0, 0)),
        pl.BlockSpec((n_blk, N_ML_HEADS, BLK), lambda *g: (index(*g), 0, 0)),
    ]


def _prompt_mixer(x, cos, sin, weights, n_blk):
    batch, seq_len, _ = x.shape
    grid = (batch // n_blk, seq_len // BLK)
    x_spec = pl.BlockSpec((n_blk, BLK, D_MODEL), lambda g, c: (g, c, 0))
    rope_spec = pl.BlockSpec((BLK, HEAD_DIM), lambda g, c: (c, 0))
    return pl.pallas_call(
        functools.partial(_mixer_kernel, seg=BLK, n_blk=n_blk, fresh_state=True),
        grid=grid,
        in_specs=[x_spec, rope_spec, rope_spec] + _mixer_weight_specs(),
        out_specs=[x_spec] + _state_specs(n_blk, n_blk, lambda g, c: g),
        out_shape=[jax.ShapeDtypeStruct(x.shape, F32)] + _state_shapes(batch, batch),
        scratch_shapes=_mixer_scratch(n_blk, fresh_state=True),
        compiler_params=_compiler_params(2),
        name="prompt_mixer",
    )(x, cos, sin, *weights)


def _sample_mixer(x, cos, sin, weights, states, seg, n_blk):
    n_blocks = x.shape[0]
    n_seq_tile = n_blk * BLK // seg
    grid = (n_blocks // n_blk,)
    x_spec = pl.BlockSpec((n_blk, BLK, D_MODEL), lambda g: (g, 0, 0))
    rope_spec = _const_spec((BLK, HEAD_DIM))
    state_specs = _state_specs(n_seq_tile, n_blk, lambda g: g)
    return pl.pallas_call(
        functools.partial(_mixer_kernel, seg=seg, n_blk=n_blk, fresh_state=False),
        grid=grid,
        in_specs=[x_spec, rope_spec, rope_spec] + _mixer_weight_specs() + state_specs,
        out_specs=[x_spec] + state_specs,
        out_shape=[jax.ShapeDtypeStruct(x.shape, F32)]
        + _state_shapes(n_blocks * BLK // seg, n_blocks),
        scratch_shapes=_mixer_scratch(n_blk, fresh_state=False),
        compiler_params=_compiler_params(1),
        name="sample_mixer",
    )(x, cos, sin, *weights, *states)


def _ffn_scratch(n_blk):
    rows = n_blk * BLK
    return [
        pltpu.VMEM((rows, D_MODEL), BF16),
        pltpu.VMEM((rows, D_FF), BF16),
        pltpu.VMEM((rows, D_MODEL), F32),
    ]


def _ffn_weight_specs():
    return [
        _const_spec((1, D_MODEL)),
        _const_spec((D_MODEL, 2 * D_FF)),
        _const_spec((CONV_W, 2 * D_FF)),
        _const_spec((1, 2 * D_FF)),
        _const_spec((D_FF, D_MODEL)),
        _const_spec((1, D_MODEL)),
    ]


def _prompt_ffn(x, weights, n_blk):
    batch, seq_len, _ = x.shape
    grid = (batch // n_blk, seq_len // BLK)
    x_spec = pl.BlockSpec((n_blk, BLK, D_MODEL), lambda g, c: (g, c, 0))
    buf_spec = pl.BlockSpec((n_blk, CONV_W - 1, 2 * D_FF), lambda g, c: (g, 0, 0))
    return pl.pallas_call(
        functools.partial(_ffn_kernel, seg=BLK, n_blk=n_blk, fresh_state=True),
        grid=grid,
        in_specs=[x_spec] + _ffn_weight_specs(),
        out_specs=[x_spec, buf_spec],
        out_shape=[jax.ShapeDtypeStruct(x.shape, F32),
                   jax.ShapeDtypeStruct((batch, CONV_W - 1, 2 * D_FF), F32)],
        scratch_shapes=_ffn_scratch(n_blk),
        compiler_params=_compiler_params(2),
        name="prompt_ffn",
    )(x, *weights)


def _sample_ffn(x, weights, conv_buf, seg, n_blk):
    n_blocks = x.shape[0]
    n_seq_tile = n_blk * BLK // seg
    grid = (n_blocks // n_blk,)
    x_spec = pl.BlockSpec((n_blk, BLK, D_MODEL), lambda g: (g, 0, 0))
    buf_spec = pl.BlockSpec((n_seq_tile, CONV_W - 1, 2 * D_FF), lambda g: (g, 0, 0))
    return pl.pallas_call(
        functools.partial(_ffn_kernel, seg=seg, n_blk=n_blk, fresh_state=False),
        grid=grid,
        in_specs=[x_spec] + _ffn_weight_specs() + [buf_spec],
        out_specs=[x_spec, buf_spec],
        out_shape=[jax.ShapeDtypeStruct(x.shape, F32),
                   jax.ShapeDtypeStruct(conv_buf.shape, F32)],
        scratch_shapes=_ffn_scratch(n_blk),
        compiler_params=_compiler_params(1),
        name="sample_ffn",
    )(x, *weights, conv_buf)


def _rope_tables(pos):
    freqs = ROPE_BASE ** (-jnp.arange(0, HEAD_DIM, 2, dtype=F32) / HEAD_DIM)
    ang = pos.astype(F32)[:, None] * freqs[None, :]
    cos, sin = jnp.cos(ang), jnp.sin(ang)
    return jnp.concatenate([cos, cos], axis=-1), jnp.concatenate([-sin, sin], axis=-1)


def kernel(x_prompt, x_sample, state_ret, state_mlstm_C, state_mlstm_n, state_mlstm_m, cache_ffn_conv, pre_mix_gain, w_in, b_gates, ret_head_gain, mlstm_head_gain, w_out, post_mix_gain, pre_ffn_gain, w_up, conv_w, conv_b, w_down, post_ffn_gain):
    depth = w_in.shape[0]
    assert depth == 1
    batch, seq_len, _ = x_prompt.shape
    dec_batch, dec_seq, _ = x_sample.shape
    assert seq_len % BLK == 0 and BLK % dec_seq == 0 and (dec_batch * dec_seq) % BLK == 0
    seqs_per_blk = BLK // dec_seq
    n_sample_blocks = dec_batch * dec_seq // BLK
    layer = 0

    n_gate = 2 * N_ML_HEADS
    w_gate = jnp.pad(w_in[layer][:, -n_gate:], ((0, 0), (0, BLK - n_gate))).astype(BF16)
    b_gate = jnp.pad(b_gates[layer], (0, BLK - n_gate)).reshape(1, BLK)
    mixer_weights = (
        pre_mix_gain[layer].reshape(1, D_MODEL), w_in[layer].astype(BF16), w_gate, b_gate,
        ret_head_gain[layer].reshape(1, RET_W), mlstm_head_gain[layer].reshape(1, ML_W),
        w_out[layer].astype(BF16), post_mix_gain[layer].reshape(1, D_MODEL))
    ffn_weights = (
        pre_ffn_gain[layer].reshape(1, D_MODEL), w_up[layer].astype(BF16), conv_w[layer],
        conv_b[layer].reshape(1, 2 * D_FF), w_down[layer].astype(BF16),
        post_ffn_gain[layer].reshape(1, D_MODEL))

    cos_p, sin_p = _rope_tables(jnp.arange(seq_len, dtype=jnp.int32))
    assert batch % PROMPT_BLOCKS_PER_STEP == 0
    x1_p, s_p, c_p, n_p, m_p = _prompt_mixer(x_prompt, cos_p, sin_p, mixer_weights, n_blk=PROMPT_BLOCKS_PER_STEP)
    y_p, buf_p = _prompt_ffn(x1_p, ffn_weights, n_blk=PROMPT_BLOCKS_PER_STEP)
    m_p = m_p[:, :, 0]

    pos_s = PAST_LEN + jnp.arange(dec_seq, dtype=jnp.int32)
    cos_s, sin_s = _rope_tables(jnp.tile(pos_s, seqs_per_blk))
    m_blocks = jnp.transpose(state_mlstm_m[layer].reshape(n_sample_blocks, seqs_per_blk, N_ML_HEADS), (0, 2, 1))
    m_blocks = jnp.pad(m_blocks, ((0, 0), (0, 0), (0, BLK - seqs_per_blk)))
    xs = x_sample.reshape(n_sample_blocks, BLK, D_MODEL)
    x1_s, s_s, c_s, n_s, m_s = _sample_mixer(
        xs, cos_s, sin_s, mixer_weights,
        (state_ret[layer], state_mlstm_C[layer], state_mlstm_n[layer], m_blocks),
        seg=dec_seq, n_blk=SAMPLE_MIXER_BLOCKS_PER_STEP)
    y_s, buf_s = _sample_ffn(x1_s, ffn_weights, cache_ffn_conv[layer], seg=dec_seq,
                             n_blk=SAMPLE_FFN_BLOCKS_PER_STEP)
    y_s = y_s.reshape(x_sample.shape)
    m_s = jnp.transpose(m_s[:, :, :seqs_per_blk], (0, 2, 1)).reshape(dec_batch, N_ML_HEADS)

    return (y_p, y_s, s_p[None], s_s[None], c_p[None], c_s[None], n_p[None], n_s[None],
            m_p[None], m_s[None], buf_p[None], buf_s[None])
```

```python
import functools
import math

import jax
import jax.numpy as jnp
from jax import lax
from jax.experimental import pallas as pl
from jax.experimental.pallas import tpu as pltpu

D_MODEL = 1024
HEAD_DIM = 128
N_RET_HEADS = 4
N_ML_HEADS = 4
RET_W = N_RET_HEADS * HEAD_DIM
ML_W = N_ML_HEADS * HEAD_DIM
D_FF = 2816
CONV_W = 3
PAST_LEN = 16384
ROPE_BASE = 10000.0
EPS = 1e-6
M_INIT = -1e30

MXU_WIDTH = 256
SUBLANES = 8
V7X_VMEM_BYTES = 64 * 1024 * 1024
VMEM_LIMIT_BYTES = V7X_VMEM_BYTES - 8 * 1024 * 1024

BLK = 128
HEADS_PER_GROUP = MXU_WIDTH // HEAD_DIM
GROUP_W = HEADS_PER_GROUP * HEAD_DIM
FF_BLK = MXU_WIDTH

PROMPT_BLOCKS_PER_STEP = 8
SAMPLE_MIXER_BLOCKS_PER_STEP = 1
SAMPLE_FFN_BLOCKS_PER_STEP = 2

F32 = jnp.float32
BF16 = jnp.bfloat16


def _dot(a, b, precision=None):
    return jnp.dot(a, b, preferred_element_type=F32, precision=precision)


def _dot_nt(a, b):
    return lax.dot_general(a, b, (((1,), (1,)), ((), ())), preferred_element_type=F32)


def _rmsnorm(x, g):
    return x * lax.rsqrt(jnp.mean(x * x, axis=-1, keepdims=True) + EPS) * g


def _groupnorm(h, g):
    mu = jnp.mean(h, axis=-1, keepdims=True)
    d = h - mu
    var = jnp.mean(d * d, axis=-1, keepdims=True)
    return d * lax.rsqrt(var + EPS) * g


def _ret_log_gamma(h):
    return math.log(1.0 - 2.0 ** (-5.0 - h))


class _Masks:
    def __init__(self, seg):
        shift = seg.bit_length() - 1
        r = lax.broadcasted_iota(jnp.int32, (BLK, BLK), 0)
        c = lax.broadcasted_iota(jnp.int32, (BLK, BLK), 1)
        rseq = r >> shift
        self.causal = (rseq == (c >> shift)) & (r >= c)
        self.tpos = (r & (seg - 1)).astype(F32)
        self.diff = (r - c).astype(F32)
        self.col_is_last_of_rowseq = c == (rseq * seg + (seg - 1))
        self.col_is_rowseq = c == rseq
        self.row_is_last_of_colseq = r == (c * seg + (seg - 1))


def _rows_from_seq(mask_col_is_rowseq, seq_row):
    return jnp.sum(jnp.where(mask_col_is_rowseq, seq_row, 0.0), axis=1, keepdims=True)


def _seq_from_last_row(mask_row_is_last_of_colseq, col):
    return jnp.sum(jnp.where(mask_row_is_last_of_colseq, col, 0.0), axis=0, keepdims=True)


def _mixer_kernel(*refs, seg, n_blk, fresh_state):
    n_seq = BLK // seg
    assert fresh_state == (n_seq == 1)
    if fresh_state:
        (x_ref, cos_ref, sin_ref, g_pre_ref, w_in_ref, w_gate_ref, b_gate_ref, g_ret_ref, g_ml_ref,
         w_out_ref, g_post_ref,
         y_ref, s_out, c_out, n_out, m_out,
         h_ref, proj_ref, mix_ref, mixed_ref, gate_ref, bcum_ref, gate_t_ref, bcum_t_ref, st_ref) = refs
        c_in, n_in, m_in = c_out, n_out, m_out

        @pl.when(pl.program_id(1) == 0)
        def _():
            st_ref[...] = jnp.zeros(st_ref.shape, F32)
            c_out[...] = jnp.zeros(c_out.shape, F32)
            n_out[...] = jnp.zeros(n_out.shape, F32)
            m_out[...] = jnp.full(m_out.shape, M_INIT, F32)
    else:
        (x_ref, cos_ref, sin_ref, g_pre_ref, w_in_ref, w_gate_ref, b_gate_ref, g_ret_ref, g_ml_ref,
         w_out_ref, g_post_ref, s_in, c_in, n_in, m_in,
         y_ref, s_out, c_out, n_out, m_out,
         h_ref, proj_ref, mix_ref, mixed_ref, gate_ref, bcum_ref, gate_t_ref, bcum_t_ref) = refs

    masks = _Masks(seg)
    scale = HEAD_DIM ** -0.5
    tri = jnp.where(masks.causal, 1.0, 0.0).astype(F32)
    cos = cos_ref[...]
    sin = sin_ref[...]

    def pre_norm(blocks):
        for blk in blocks:
            rows = pl.ds(blk * BLK, BLK)
            h_ref[rows, :] = _rmsnorm(x_ref[blk], g_pre_ref[...]).astype(BF16)

    def gate_projection(rows_all):
        gate_ref[rows_all, :] = _dot(h_ref[rows_all, :], w_gate_ref[...]) + b_gate_ref[...]

    def gate_cumsums(blocks):
        for blk in blocks:
            rows = pl.ds(blk * BLK, BLK)
            gates = gate_ref[rows, :]
            bcum = _dot(tri, jax.nn.log_sigmoid(gates), precision=lax.Precision.HIGHEST)
            bcum_ref[rows, :] = bcum
            gate_t_ref[blk] = gates.T
            bcum_t_ref[blk] = bcum.T

    def rotary(t):
        return t * cos + pltpu.roll(t, HEAD_DIM // 2, axis=1) * sin

    def seq_rows(j):
        return slice(j * seg, (j + 1) * seg)

    def lane_mask(j):
        c = lax.broadcasted_iota(jnp.int32, (BLK, BLK), 1)
        return (c >= j * seg) & (c < (j + 1) * seg)

    def ret_head(proj, blk, head, col):
        rows = pl.ds(blk * BLK, BLK)
        lg = _ret_log_gamma(head)
        q = rotary(proj[rows,col:col + HEAD_DIM])
        k = rotary(proj[rows,GROUP_W + col:GROUP_W + col + HEAD_DIM]) * scale
        v = proj[rows,2 * GROUP_W + col:2 * GROUP_W + col + HEAD_DIM]
        g = proj[rows,3 * GROUP_W + col:3 * GROUP_W + col + HEAD_DIM]
        qb, kb, vb = q.astype(BF16), k.astype(BF16), v.astype(BF16)
        dmat = jnp.where(masks.causal, jnp.exp(jnp.where(masks.causal, masks.diff * lg, 0.0)), 0.0)
        xi = jnp.exp((masks.tpos + 1.0) * lg)
        zeta = jnp.exp((float(seg - 1) - masks.tpos) * lg)
        chunk_decay = math.exp(seg * lg)
        scores = _dot_nt(qb, kb)
        kz_t = (k * zeta).T
        inter = []
        kz_tb = kz_t.astype(BF16)
        updates = _dot(jnp.concatenate(
            [jnp.where(lane_mask(j), kz_tb, jnp.zeros_like(kz_tb)) for j in range(n_seq)], axis=0), vb)
        for j in range(n_seq):
            seq = blk * n_seq + j
            s_prev = s_in[seq, head]
            inter.append(_dot(q[seq_rows(j)].astype(BF16), s_prev.astype(BF16)))
            s_out[seq, head] = s_prev * chunk_decay + updates[j * HEAD_DIM:(j + 1) * HEAD_DIM]
        inter = jnp.concatenate(inter, axis=0)
        yield
        out = _dot((scores * dmat).astype(BF16), vb)
        yield
        out = out + inter * xi
        y = _groupnorm(out, g_ret_ref[:, head * HEAD_DIM:(head + 1) * HEAD_DIM]) * (g * jax.nn.sigmoid(g))
        mix_ref[rows, head * HEAD_DIM:(head + 1) * HEAD_DIM] = y.astype(BF16)

    def ml_head(proj, blk, head, col):
        rows = pl.ds(blk * BLK, BLK)
        q = proj[rows,col:col + HEAD_DIM]
        k = proj[rows,GROUP_W + col:GROUP_W + col + HEAD_DIM] * scale
        v = proj[rows,2 * GROUP_W + col:2 * GROUP_W + col + HEAD_DIM]
        o = proj[rows,3 * GROUP_W + col:3 * GROUP_W + col + HEAD_DIM]
        qb, kb, vb = q.astype(BF16), k.astype(BF16), v.astype(BF16)
        i_col = gate_ref[rows, head:head + 1]
        b_col = bcum_ref[rows, N_ML_HEADS + head:N_ML_HEADS + head + 1]
        i_row = gate_t_ref[blk, head:head + 1, :]
        b_row = bcum_t_ref[blk, N_ML_HEADS + head:N_ML_HEADS + head + 1, :]
        m_prev_seq = m_in[blk, head:head + 1, :]
        m_prev = _rows_from_seq(masks.col_is_rowseq, m_prev_seq)
        logw = jnp.where(masks.causal, b_col - b_row + i_row, -jnp.inf)
        m_t = jnp.maximum(b_col + m_prev, jnp.max(logw, axis=1, keepdims=True))
        w = jnp.exp(logw - m_t)
        inter_w = jnp.exp(b_col + m_prev - m_t)
        s = _dot_nt(qb, kb)
        m_new_seq = _seq_from_last_row(masks.row_is_last_of_colseq, m_t)
        b_last_seq = _seq_from_last_row(masks.row_is_last_of_colseq, b_col)
        decay_seq = jnp.exp(b_last_seq + m_prev_seq - m_new_seq)
        m_new = _rows_from_seq(masks.col_is_rowseq, m_new_seq)
        b_last = jnp.sum(jnp.where(masks.col_is_last_of_rowseq, b_row, 0.0), axis=1, keepdims=True)
        wl = jnp.exp(b_last - b_col + i_col - m_new)
        vw_t = (v * wl).T
        kw = k * wl
        qc, qn = [], []
        for j in range(n_seq):
            seq = blk * n_seq + j
            c_prev = c_in[seq, head]
            n_prev = n_in[seq, head:head + 1, :]
            qc.append(_dot_nt(q[seq_rows(j)].astype(BF16), c_prev.astype(BF16)))
            qn.append(jnp.sum(q[seq_rows(j)] * n_prev, axis=1, keepdims=True))
            decay = decay_seq[:, j:j + 1]
            n_out[seq, head:head + 1, :] = decay * n_prev + jnp.sum(kw[seq_rows(j)], axis=0, keepdims=True)
        qc = jnp.concatenate(qc, axis=0)
        qn = jnp.concatenate(qn, axis=0)
        m_out[blk, head:head + 1, :] = m_new_seq
        yield
        s = s * w
        num = _dot(s.astype(BF16), vb)
        den = jnp.sum(s, axis=1, keepdims=True)
        vw_tb = vw_t.astype(BF16)
        updates = _dot(jnp.concatenate(
            [jnp.where(lane_mask(j), vw_tb, jnp.zeros_like(vw_tb)) for j in range(n_seq)], axis=0), kb)
        for j in range(n_seq):
            seq = blk * n_seq + j
            c_out[seq, head] = (decay_seq[:, j:j + 1] * c_in[seq, head]
                                + updates[j * HEAD_DIM:(j + 1) * HEAD_DIM])
        yield
        num = num + inter_w * qc
        den = jnp.maximum(jnp.abs(den + inter_w * qn), jnp.exp(-m_t))
        hout = num / den
        y = _groupnorm(hout, g_ml_ref[:, head * HEAD_DIM:(head + 1) * HEAD_DIM]) * jax.nn.sigmoid(o)
        mix_ref[rows, RET_W + head * HEAD_DIM:RET_W + (head + 1) * HEAD_DIM] = y.astype(BF16)

    r_idx = lax.broadcasted_iota(jnp.int32, (BLK, BLK), 0)
    c_idx = lax.broadcasted_iota(jnp.int32, (BLK, BLK), 1)
    causal_t = r_idx <= c_idx
    lane_pos = lax.broadcasted_iota(jnp.int32, (1, BLK), 1).astype(F32)
    head_consts = {}

    def ret_consts(head):
        if head not in head_consts:
            lg = _ret_log_gamma(head)
            dmat_t = jnp.where(causal_t, jnp.exp(jnp.where(causal_t, (c_idx - r_idx).astype(F32) * lg, 0.0)), 0.0)
            xi_row = jnp.exp((lane_pos + 1.0) * lg)
            zeta = jnp.exp((float(BLK - 1) - r_idx.astype(F32)) * lg)
            head_consts[head] = (dmat_t, xi_row, zeta, math.exp(BLK * lg))
        return head_consts[head]

    def channel_norm_t(h_t):
        mu = jnp.mean(h_t, axis=0, keepdims=True)
        d = h_t - mu
        var = jnp.mean(d * d, axis=0, keepdims=True)
        return d * lax.rsqrt(var + EPS)

    def queries_of_pair(q_pair):
        zero = jnp.zeros_like(q_pair[0])
        return jnp.concatenate([jnp.concatenate([q_pair[0], zero], axis=1),
                                jnp.concatenate([zero, q_pair[1]], axis=1)], axis=0)

    def pair_cols(i):
        return slice(i * HEAD_DIM, (i + 1) * HEAD_DIM)

    def ret_pair_single(proj, blk, first_head):
        rows = pl.ds(blk * BLK, BLK)
        pair = range(HEADS_PER_GROUP)
        consts = [ret_consts(first_head + i) for i in pair]
        q = [rotary(proj[rows, pair_cols(i)]) for i in pair]
        k = [rotary(proj[rows, GROUP_W + i * HEAD_DIM:GROUP_W + (i + 1) * HEAD_DIM]) * scale for i in pair]
        v_tb = [proj[rows, 2 * GROUP_W + i * HEAD_DIM:2 * GROUP_W + (i + 1) * HEAD_DIM].T.astype(BF16)
                for i in pair]
        lhs = jnp.concatenate([
            jnp.concatenate([k[i].astype(BF16) for i in pair], axis=1),
            jnp.concatenate([st_ref[blk, first_head + i].astype(BF16) for i in pair], axis=1)], axis=0)
        first = _dot_nt(lhs, queries_of_pair([q[i].astype(BF16) for i in pair]))
        kz_b = [(k[i] * consts[i][2]).astype(BF16) for i in pair]
        yield
        out_t = []
        for i in pair:
            dmat_t, _, _, chunk_decay = consts[i]
            scores_t = first[:BLK, pair_cols(i)]
            both = _dot(v_tb[i], jnp.concatenate([kz_b[i], (scores_t * dmat_t).astype(BF16)], axis=1))
            st_ref[blk, first_head + i] = st_ref[blk, first_head + i] * chunk_decay + both[:, :HEAD_DIM]
            out_t.append(both[:, HEAD_DIM:])
        yield
        for i in pair:
            head = first_head + i
            out = out_t[i] + first[BLK:, pair_cols(i)] * consts[i][1]
            g = proj[rows, 3 * GROUP_W + i * HEAD_DIM:3 * GROUP_W + (i + 1) * HEAD_DIM]
            y = channel_norm_t(out).T * g_ret_ref[:, head * HEAD_DIM:(head + 1) * HEAD_DIM] * (g * jax.nn.sigmoid(g))
            mix_ref[rows, head * HEAD_DIM:(head + 1) * HEAD_DIM] = y.astype(BF16)

    def ml_pair_single(proj, blk, first_head):
        rows = pl.ds(blk * BLK, BLK)
        pair = range(HEADS_PER_GROUP)
        q = [proj[rows, pair_cols(i)] for i in pair]
        k = [proj[rows, GROUP_W + i * HEAD_DIM:GROUP_W + (i + 1) * HEAD_DIM] * scale for i in pair]
        kb = [k[i].astype(BF16) for i in pair]
        n_prev = [n_in[blk, first_head + i:first_head + i + 1, :] for i in pair]
        lhs = jnp.concatenate([
            jnp.concatenate(kb, axis=1),
            jnp.concatenate([c_in[blk, first_head + i].astype(BF16) for i in pair], axis=1),
            jnp.concatenate([jnp.broadcast_to(n_prev[i], (2 * SUBLANES, HEAD_DIM)).astype(BF16) for i in pair],
                            axis=1)], axis=0)
        first = _dot_nt(lhs, queries_of_pair([q[i].astype(BF16) for i in pair]))
        heads = [ml_head_single(proj, blk, first_head + i, i * HEAD_DIM, k[i], kb[i], n_prev[i],
                                first[:BLK, pair_cols(i)], first[BLK:2 * BLK, pair_cols(i)],
                                first[2 * BLK:2 * BLK + 1, pair_cols(i)]) for i in pair]
        for _ in range(3):
            for head in heads:
                next(head, None)
            yield

    def ml_head_single(proj, blk, head, col, k, kb, n_prev, s_t, qc_t, qn):
        rows = pl.ds(blk * BLK, BLK)
        v_t = proj[rows, 2 * GROUP_W + col:2 * GROUP_W + col + HEAD_DIM].T
        i_row = gate_t_ref[blk, head:head + 1, :]
        b_row = bcum_t_ref[blk, N_ML_HEADS + head:N_ML_HEADS + head + 1, :]
        key_term = gate_ref[rows, head:head + 1] - bcum_ref[rows, N_ML_HEADS + head:N_ML_HEADS + head + 1]
        m_prev = m_in[blk, head:head + 1, 0:1]
        logw_t = jnp.where(causal_t, b_row + key_term, -jnp.inf)
        m_t = jnp.maximum(b_row + m_prev, jnp.max(logw_t, axis=0, keepdims=True))
        w_t = jnp.exp(logw_t - m_t)
        inter_w = jnp.exp(b_row + m_prev - m_t)
        m_new = m_t[:, BLK - 1:BLK]
        b_last = b_row[:, BLK - 1:BLK]
        decay = jnp.exp(b_last + m_prev - m_new)
        wl = jnp.exp(b_last - b_row + i_row - m_new)
        m_out[blk, head:head + 1, :] = jnp.broadcast_to(m_new, (1, BLK))
        wk = _dot(jnp.broadcast_to(wl, (2 * SUBLANES, BLK)).astype(BF16), kb)[0:1, :]
        n_out[blk, head:head + 1, :] = decay * n_prev + wk
        kw_b = (k * jnp.exp(key_term + (b_last - m_new))).astype(BF16)
        v_tb = v_t.astype(BF16)
        yield
        s_t = s_t * w_t
        den = jnp.sum(s_t, axis=0, keepdims=True)
        both = _dot(v_tb, jnp.concatenate([kw_b, s_t.astype(BF16)], axis=1))
        c_out[blk, head] = decay * c_in[blk, head] + both[:, :HEAD_DIM]
        num_t = both[:, HEAD_DIM:]
        yield
        num_t = num_t + inter_w * qc_t
        den = jnp.maximum(jnp.abs(den + inter_w * qn), jnp.exp(-m_t))
        o = proj[rows, 3 * GROUP_W + col:3 * GROUP_W + col + HEAD_DIM]
        y = (channel_norm_t(num_t / den).T * g_ml_ref[:, head * HEAD_DIM:(head + 1) * HEAD_DIM]
             * jax.nn.sigmoid(o))
        mix_ref[rows, RET_W + head * HEAD_DIM:RET_W + (head + 1) * HEAD_DIM] = y.astype(BF16)

    n_ret_groups = N_RET_HEADS // HEADS_PER_GROUP
    n_groups = n_ret_groups + N_ML_HEADS // HEADS_PER_GROUP

    def first_head_of(grp):
        return (grp if grp < n_ret_groups else grp - n_ret_groups) * HEADS_PER_GROUP

    def project(grp, kind, rows_all):
        is_ret = grp < n_ret_groups
        c0 = (0 if is_ret else 4 * RET_W) + kind * (RET_W if is_ret else ML_W) + first_head_of(grp) * HEAD_DIM
        proj_ref[grp % 2, rows_all, kind * GROUP_W:(kind + 1) * GROUP_W] = _dot(
            h_ref[rows_all, :], w_in_ref[:, c0:c0 + GROUP_W])

    def out_project(grp, rows_all):
        cols = slice(grp * GROUP_W, (grp + 1) * GROUP_W)
        part = _dot(mix_ref[rows_all, cols], w_out_ref[cols, :])
        if grp == 0:
            mixed_ref[rows_all, :] = part
        else:
            mixed_ref[rows_all, :] += part

    def row_pipeline(blocks):
        rows_all = pl.ds(blocks[0] * BLK, len(blocks) * BLK)
        pre_norm(blocks)
        for kind in range(4):
            project(0, kind, rows_all)
        yield
        for grp in range(n_groups):
            if n_seq == 1:
                pair_fn = ret_pair_single if grp < n_ret_groups else ml_pair_single
                heads = [pair_fn(proj_ref.at[grp % 2], blk, first_head_of(grp)) for blk in blocks]
            else:
                head_fn = ret_head if grp < n_ret_groups else ml_head
                heads = [head_fn(proj_ref.at[grp % 2], blk, first_head_of(grp) + hh, hh * HEAD_DIM)
                         for blk in blocks for hh in range(HEADS_PER_GROUP)]
            fillers = {0: [functools.partial(project, grp + 1, kind, rows_all) for kind in range(4)]
                       if grp + 1 < n_groups else [],
                       1: [functools.partial(out_project, grp - 1, rows_all)] if grp >= 1 else [],
                       2: []}
            if grp == 0:
                fillers[0].append(functools.partial(gate_projection, rows_all))
                fillers[1].append(functools.partial(gate_cumsums, blocks))
            for phase in range(3):
                todo = list(fillers[phase])
                every = max(1, len(heads) // max(1, len(todo)))
                for i, head in enumerate(heads):
                    next(head, None)
                    if todo and (i + 1) % every == 0:
                        todo.pop(0)()
                for filler in todo:
                    filler()
            yield
        last_cols = slice((n_groups - 1) * GROUP_W, n_groups * GROUP_W)
        last = _dot(mix_ref[rows_all, last_cols], w_out_ref[last_cols, :])
        for i, blk in enumerate(blocks):
            rows = pl.ds(blk * BLK, BLK)
            mixed = mixed_ref[rows, :] + last[i * BLK:(i + 1) * BLK]
            y_ref[blk] = x_ref[blk] + _rmsnorm(mixed, g_post_ref[...])

    n_runs = 2 if n_blk % 2 == 0 else 1
    per_run = n_blk // n_runs
    pipelines = [row_pipeline(list(range(i * per_run, (i + 1) * per_run))) for i in range(n_runs)]
    n_stages = n_groups + 2
    for step in range(n_stages + n_runs - 1):
        for lag, pipeline in enumerate(pipelines):
            if 0 <= step - lag < n_stages:
                next(pipeline, None)

    if fresh_state:
        @pl.when(pl.program_id(1) == pl.num_programs(1) - 1)
        def _():
            for blk in range(n_blk):
                for head in range(N_RET_HEADS):
                    s_out[blk, head] = st_ref[blk, head].T


def _ffn_kernel(*refs, seg, n_blk, fresh_state):
    n_seq_tile = n_blk * BLK // seg
    rows_tile = n_blk * BLK
    if fresh_state:
        (x_ref, g_pre_ref, w_up_ref, conv_w_ref, conv_b_ref, w_down_ref, g_post_ref,
         y_ref, buf_out, h_ref, act_ref, ffn_ref) = refs
        buf_in = buf_out

        @pl.when(pl.program_id(1) == 0)
        def _():
            buf_out[...] = jnp.zeros(buf_out.shape, F32)
    else:
        (x_ref, g_pre_ref, w_up_ref, conv_w_ref, conv_b_ref, w_down_ref, g_post_ref, buf_in,
         y_ref, buf_out, h_ref, act_ref, ffn_ref) = refs

    for blk in range(n_blk):
        rows = pl.ds(blk * BLK, BLK)
        h_ref[rows, :] = _rmsnorm(x_ref[blk], g_pre_ref[...]).astype(BF16)

    tpos = lax.broadcasted_iota(jnp.int32, (n_seq_tile, seg, FF_BLK), 1)

    def conv(cols):
        up = _dot(h_ref[...], w_up_ref[:, cols])
        prev0 = buf_in[:, 0:1, cols]
        prev1 = buf_in[:, 1:2, cols]
        up3 = up.reshape(n_seq_tile, seg, FF_BLK)
        sh1 = pltpu.roll(up, 1, axis=0).reshape(n_seq_tile, seg, FF_BLK)
        sh2 = pltpu.roll(up, 2, axis=0).reshape(n_seq_tile, seg, FF_BLK)
        sh1 = jnp.where(tpos == 0, prev1, sh1)
        sh2 = jnp.where(tpos == 0, prev0, jnp.where(tpos == 1, prev1, sh2))
        buf_out[:, 0:1, cols] = up3[:, seg - 2:seg - 1, :]
        buf_out[:, 1:2, cols] = up3[:, seg - 1:seg, :]
        w = conv_w_ref[:, cols]
        out = sh2 * w[0:1, :] + sh1 * w[1:2, :] + up3 * w[2:3, :] + conv_b_ref[:, cols]
        return out.reshape(rows_tile, FF_BLK)

    def activation(gate, val):
        c0 = -2.0 * math.log2(math.e) * math.sqrt(2.0 / math.pi)
        c1 = 0.044715 * c0
        return ((gate * val) / (1.0 + jnp.exp2(gate * (gate * gate * c1 + c0)))).astype(BF16)

    for j in range(D_FF // FF_BLK):
        cols = slice(j * FF_BLK, (j + 1) * FF_BLK)
        gate = conv(cols)
        val = conv(slice(D_FF + j * FF_BLK, D_FF + (j + 1) * FF_BLK))
        act_ref[:, cols] = activation(gate, val)

    ffn_ref[...] = _dot(act_ref[...], w_down_ref[...])
    for blk in range(n_blk):
        rows = pl.ds(blk * BLK, BLK)
        y_ref[blk] = x_ref[blk] + _rmsnorm(ffn_ref[rows, :], g_post_ref[...])


def _const_spec(shape):
    zeros = (0,) * len(shape)
    return pl.BlockSpec(shape, lambda *_: zeros, pipeline_mode=pl.Buffered(1))


def _compiler_params(n_grid_dims):
    return pltpu.CompilerParams(
        dimension_semantics=("arbitrary",) * n_grid_dims,
        vmem_limit_bytes=VMEM_LIMIT_BYTES)


def _mixer_scratch(n_blk, fresh_state):
    rows = n_blk * BLK
    carried = [pltpu.VMEM((n_blk, N_RET_HEADS, HEAD_DIM, HEAD_DIM), F32)] if fresh_state else []
    return [
        pltpu.VMEM((rows, D_MODEL), BF16),
        pltpu.VMEM((2, rows, 4 * GROUP_W), F32),
        pltpu.VMEM((rows, RET_W + ML_W), BF16),
        pltpu.VMEM((rows, D_MODEL), F32),
        pltpu.VMEM((rows, BLK), F32),
        pltpu.VMEM((rows, BLK), F32),
        pltpu.VMEM((n_blk, BLK, BLK), F32),
        pltpu.VMEM((n_blk, BLK, BLK), F32),
    ] + carried


def _mixer_weight_specs():
    return [
        _const_spec((1, D_MODEL)),
        _const_spec((D_MODEL, 4 * RET_W + 4 * ML_W + 2 * N_ML_HEADS)),
        _const_spec((D_MODEL, BLK)),
        _const_spec((1, BLK)),
        _const_spec((1, RET_W)),
        _const_spec((1, ML_W)),
        _const_spec((RET_W + ML_W, D_MODEL)),
        _const_spec((1, D_MODEL)),
    ]


def _state_shapes(n_seqs, n_blocks):
    return [
        jax.ShapeDtypeStruct((n_seqs, N_RET_HEADS, HEAD_DIM, HEAD_DIM), F32),
        jax.ShapeDtypeStruct((n_seqs, N_ML_HEADS, HEAD_DIM, HEAD_DIM), F32),
        jax.ShapeDtypeStruct((n_seqs, N_ML_HEADS, HEAD_DIM), F32),
        jax.ShapeDtypeStruct((n_blocks, N_ML_HEADS, BLK), F32),
    ]


def _state_specs(n_seq_tile, n_blk, index):
    return [
        pl.BlockSpec((n_seq_tile, N_RET_HEADS, HEAD_DIM, HEAD_DIM), lambda *g: (index(*g), 0, 0, 0)),
        pl.BlockSpec((n_seq_tile, N_ML_HEADS, HEAD_DIM, HEAD_DIM), lambda *g: (index(*g), 0, 0, 0)),
        pl.BlockSpec((n_seq_tile, N_ML_HEADS, HEAD_DIM), lambda *g: (index(*g), 0, 0)),
        pl.BlockSpec((n_blk, N_ML_HEADS, BLK), lambda *g: (index(*g), 0, 0)),
    ]


def _prompt_mixer(x, cos, sin, weights, n_blk):
    batch, seq_len, _ = x.shape
    grid = (batch // n_blk, seq_len // BLK)
    x_spec = pl.BlockSpec((n_blk, BLK, D_MODEL), lambda g, c: (g, c, 0))
    rope_spec = pl.BlockSpec((BLK, HEAD_DIM), lambda g, c: (c, 0))
    return pl.pallas_call(
        functools.partial(_mixer_kernel, seg=BLK, n_blk=n_blk, fresh_state=True),
        grid=grid,
        in_specs=[x_spec, rope_spec, rope_spec] + _mixer_weight_specs(),
        out_specs=[x_spec] + _state_specs(n_blk, n_blk, lambda g, c: g),
        out_shape=[jax.ShapeDtypeStruct(x.shape, F32)] + _state_shapes(batch, batch),
        scratch_shapes=_mixer_scratch(n_blk, fresh_state=True),
        compiler_params=_compiler_params(2),
        name="prompt_mixer",
    )(x, cos, sin, *weights)


def _sample_mixer(x, cos, sin, weights, states, seg, n_blk):
    n_blocks = x.shape[0]
    n_seq_tile = n_blk * BLK // seg
    grid = (n_blocks // n_blk,)
    x_spec = pl.BlockSpec((n_blk, BLK, D_MODEL), lambda g: (g, 0, 0))
    rope_spec = _const_spec((BLK, HEAD_DIM))
    state_specs = _state_specs(n_seq_tile, n_blk, lambda g: g)
    return pl.pallas_call(
        functools.partial(_mixer_kernel, seg=seg, n_blk=n_blk, fresh_state=False),
        grid=grid,
        in_specs=[x_spec, rope_spec, rope_spec] + _mixer_weight_specs() + state_specs,
        out_specs=[x_spec] + state_specs,
        out_shape=[jax.ShapeDtypeStruct(x.shape, F32)]
        + _state_shapes(n_blocks * BLK // seg, n_blocks),
        scratch_shapes=_mixer_scratch(n_blk, fresh_state=False),
        compiler_params=_compiler_params(1),
        name="sample_mixer",
    )(x, cos, sin, *weights, *states)


def _ffn_scratch(n_blk):
    rows = n_blk * BLK
    return [
        pltpu.VMEM((rows, D_MODEL), BF16),
        pltpu.VMEM((rows, D_FF), BF16),
        pltpu.VMEM((rows, D_MODEL), F32),
    ]


def _ffn_weight_specs():
    return [
        _const_spec((1, D_MODEL)),
        _const_spec((D_MODEL, 2 * D_FF)),
        _const_spec((CONV_W, 2 * D_FF)),
        _const_spec((1, 2 * D_FF)),
        _const_spec((D_FF, D_MODEL)),
        _const_spec((1, D_MODEL)),
    ]


def _prompt_ffn(x, weights, n_blk):
    batch, seq_len, _ = x.shape
    grid = (batch // n_blk, seq_len // BLK)
    x_spec = pl.BlockSpec((n_blk, BLK, D_MODEL), lambda g, c: (g, c, 0))
    buf_spec = pl.BlockSpec((n_blk, CONV_W - 1, 2 * D_FF), lambda g, c: (g, 0, 0))
    return pl.pallas_call(
        functools.partial(_ffn_kernel, seg=BLK, n_blk=n_blk, fresh_state=True),
        grid=grid,
        in_specs=[x_spec] + _ffn_weight_specs(),
        out_specs=[x_spec, buf_spec],
        out_shape=[jax.ShapeDtypeStruct(x.shape, F32),
                   jax.ShapeDtypeStruct((batch, CONV_W - 1, 2 * D_FF), F32)],
        scratch_shapes=_ffn_scratch(n_blk),
        compiler_params=_compiler_params(2),
        name="prompt_ffn",
    )(x, *weights)


def _sample_ffn(x, weights, conv_buf, seg, n_blk):
    n_blocks = x.shape[0]
    n_seq_tile = n_blk * BLK // seg
    grid = (n_blocks // n_blk,)
    x_spec = pl.BlockSpec((n_blk, BLK, D_MODEL), lambda g: (g, 0, 0))
    buf_spec = pl.BlockSpec((n_seq_tile, CONV_W - 1, 2 * D_FF), lambda g: (g, 0, 0))
    return pl.pallas_call(
        functools.partial(_ffn_kernel, seg=seg, n_blk=n_blk, fresh_state=False),
        grid=grid,
        in_specs=[x_spec] + _ffn_weight_specs() + [buf_spec],
        out_specs=[x_spec, buf_spec],
        out_shape=[jax.ShapeDtypeStruct(x.shape, F32),
                   jax.ShapeDtypeStruct(conv_buf.shape, F32)],
        scratch_shapes=_ffn_scratch(n_blk),
        compiler_params=_compiler_params(1),
        name="sample_ffn",
    )(x, *weights, conv_buf)


def _rope_freqs():
    return ROPE_BASE ** (-jnp.arange(0, HEAD_DIM, 2, dtype=F32) / HEAD_DIM)


def _rope_layout(cos, sin):
    return jnp.concatenate([cos, cos], axis=-1), jnp.concatenate([-sin, sin], axis=-1)


def _rope_tables(pos):
    ang = pos.astype(F32)[:, None] * _rope_freqs()[None, :]
    return _rope_layout(jnp.cos(ang), jnp.sin(ang))


def _rope_tables_by_chunk(n_chunks):
    freqs = _rope_freqs()[None, :]
    base = (BLK * jnp.arange(n_chunks, dtype=jnp.int32)).astype(F32)[:, None] * freqs
    step = jnp.arange(BLK, dtype=jnp.int32).astype(F32)[:, None] * freqs
    cos_b, sin_b = jnp.cos(base)[:, None, :], jnp.sin(base)[:, None, :]
    cos_s, sin_s = jnp.cos(step)[None], jnp.sin(step)[None]
    cos = (cos_b * cos_s - sin_b * sin_s).reshape(n_chunks * BLK, HEAD_DIM // 2)
    sin = (sin_b * cos_s + cos_b * sin_s).reshape(n_chunks * BLK, HEAD_DIM // 2)
    return _rope_layout(cos, sin)


def kernel(x_prompt, x_sample, state_ret, state_mlstm_C, state_mlstm_n, state_mlstm_m, cache_ffn_conv, pre_mix_gain, w_in, b_gates, ret_head_gain, mlstm_head_gain, w_out, post_mix_gain, pre_ffn_gain, w_up, conv_w, conv_b, w_down, post_ffn_gain):
    depth = w_in.shape[0]
    assert depth == 1
    batch, seq_len, _ = x_prompt.shape
    dec_batch, dec_seq, _ = x_sample.shape
    assert seq_len % BLK == 0 and BLK % dec_seq == 0 and (dec_batch * dec_seq) % BLK == 0
    seqs_per_blk = BLK // dec_seq
    n_sample_blocks = dec_batch * dec_seq // BLK
    layer = 0

    n_gate = 2 * N_ML_HEADS
    w_gate = jnp.pad(w_in[layer][:, -n_gate:], ((0, 0), (0, BLK - n_gate))).astype(BF16)
    b_gate = jnp.pad(b_gates[layer], (0, BLK - n_gate)).reshape(1, BLK)
    mixer_weights = (
        pre_mix_gain[layer].reshape(1, D_MODEL), w_in[layer].astype(BF16), w_gate, b_gate,
        ret_head_gain[layer].reshape(1, RET_W), mlstm_head_gain[layer].reshape(1, ML_W),
        w_out[layer].astype(BF16), post_mix_gain[layer].reshape(1, D_MODEL))
    ffn_weights = (
        pre_ffn_gain[layer].reshape(1, D_MODEL), w_up[layer].astype(BF16), conv_w[layer],
        conv_b[layer].reshape(1, 2 * D_FF), w_down[layer].astype(BF16),
        post_ffn_gain[layer].reshape(1, D_MODEL))

    cos_p, sin_p = _rope_tables_by_chunk(seq_len // BLK)
    assert batch % PROMPT_BLOCKS_PER_STEP == 0
    x1_p, s_p, c_p, n_p, m_p = _prompt_mixer(x_prompt, cos_p, sin_p, mixer_weights, n_blk=PROMPT_BLOCKS_PER_STEP)
    y_p, buf_p = _prompt_ffn(x1_p, ffn_weights, n_blk=PROMPT_BLOCKS_PER_STEP)
    m_p = m_p[:, :, 0]

    pos_s = PAST_LEN + jnp.arange(dec_seq, dtype=jnp.int32)
    cos_s, sin_s = _rope_tables(jnp.tile(pos_s, seqs_per_blk))
    m_blocks = jnp.transpose(state_mlstm_m[layer].reshape(n_sample_blocks, seqs_per_blk, N_ML_HEADS), (0, 2, 1))
    m_blocks = jnp.pad(m_blocks, ((0, 0), (0, 0), (0, BLK - seqs_per_blk)))
    xs = x_sample.reshape(n_sample_blocks, BLK, D_MODEL)
    x1_s, s_s, c_s, n_s, m_s = _sample_mixer(
        xs, cos_s, sin_s, mixer_weights,
        (state_ret[layer], state_mlstm_C[layer], state_mlstm_n[layer], m_blocks),
        seg=dec_seq, n_blk=SAMPLE_MIXER_BLOCKS_PER_STEP)
    y_s, buf_s = _sample_ffn(x1_s, ffn_weights, cache_ffn_conv[layer], seg=dec_seq,
                             n_blk=SAMPLE_FFN_BLOCKS_PER_STEP)
    y_s = y_s.reshape(x_sample.shape)
    m_s = jnp.transpose(m_s[:, :, :seqs_per_blk], (0, 2, 1)).reshape(dec_batch, N_ML_HEADS)

    return (y_p, y_s, s_p[None], s_s[None], c_p[None], c_s[None], n_p[None], n_s[None],
            m_p[None], m_s[None], buf_p[None], buf_s[None])
```

```python
import functools
import math

import jax
import jax.numpy as jnp
from jax import lax
from jax.experimental import pallas as pl
from jax.experimental.pallas import tpu as pltpu

D_MODEL = 1024
HEAD_DIM = 128
N_RET_HEADS = 4
N_ML_HEADS = 4
RET_W = N_RET_HEADS * HEAD_DIM
ML_W = N_ML_HEADS * HEAD_DIM
D_FF = 2816
CONV_W = 3
PAST_LEN = 16384
ROPE_BASE = 10000.0
EPS = 1e-6
M_INIT = -1e30

MXU_WIDTH = 256
SUBLANES = 8
V7X_VMEM_BYTES = 64 * 1024 * 1024
VMEM_LIMIT_BYTES = V7X_VMEM_BYTES - 8 * 1024 * 1024

BLK = 128
HEADS_PER_GROUP = MXU_WIDTH // HEAD_DIM
GROUP_W = HEADS_PER_GROUP * HEAD_DIM
FF_BLK = MXU_WIDTH

PROMPT_BLOCKS_PER_STEP = 8
SAMPLE_MIXER_BLOCKS_PER_STEP = 1
SAMPLE_FFN_BLOCKS_PER_STEP = 2

F32 = jnp.float32
BF16 = jnp.bfloat16


def _dot(a, b, precision=None):
    return jnp.dot(a, b, preferred_element_type=F32, precision=precision)


def _dot_nt(a, b):
    return lax.dot_general(a, b, (((1,), (1,)), ((), ())), preferred_element_type=F32)


def _rmsnorm(x, g):
    return x * lax.rsqrt(jnp.mean(x * x, axis=-1, keepdims=True) + EPS) * g


def _groupnorm(h, g):
    mu = jnp.mean(h, axis=-1, keepdims=True)
    d = h - mu
    var = jnp.mean(d * d, axis=-1, keepdims=True)
    return d * lax.rsqrt(var + EPS) * g


def _ret_log_gamma(h):
    return math.log(1.0 - 2.0 ** (-5.0 - h))


class _Masks:
    def __init__(self, seg):
        shift = seg.bit_length() - 1
        r = lax.broadcasted_iota(jnp.int32, (BLK, BLK), 0)
        c = lax.broadcasted_iota(jnp.int32, (BLK, BLK), 1)
        rseq = r >> shift
        self.causal = (rseq == (c >> shift)) & (r >= c)
        self.tpos = (r & (seg - 1)).astype(F32)
        self.diff = (r - c).astype(F32)
        self.col_is_last_of_rowseq = c == (rseq * seg + (seg - 1))
        self.col_is_rowseq = c == rseq
        self.row_is_last_of_colseq = r == (c * seg + (seg - 1))


def _rows_from_seq(mask_col_is_rowseq, seq_row):
    return jnp.sum(jnp.where(mask_col_is_rowseq, seq_row, 0.0), axis=1, keepdims=True)


def _seq_from_last_row(mask_row_is_last_of_colseq, col):
    return jnp.sum(jnp.where(mask_row_is_last_of_colseq, col, 0.0), axis=0, keepdims=True)


def _mixer_kernel(*refs, seg, n_blk, fresh_state):
    n_seq = BLK // seg
    assert fresh_state == (n_seq == 1)
    if fresh_state:
        (x_ref, cos_ref, sin_ref, g_pre_ref, w_in_ref, w_gate_ref, b_gate_ref, g_ret_ref, g_ml_ref,
         w_out_ref, g_post_ref,
         y_ref, s_out, c_out, n_out, m_out,
         h_ref, proj_ref, mix_ref, mixed_ref, gate_ref, bcum_ref, gate_t_ref, bcum_t_ref, st_ref) = refs
        c_in, n_in, m_in = c_out, n_out, m_out

        @pl.when(pl.program_id(1) == 0)
        def _():
            st_ref[...] = jnp.zeros(st_ref.shape, F32)
            c_out[...] = jnp.zeros(c_out.shape, F32)
            n_out[...] = jnp.zeros(n_out.shape, F32)
            m_out[...] = jnp.full(m_out.shape, M_INIT, F32)
    else:
        (x_ref, cos_ref, sin_ref, g_pre_ref, w_in_ref, w_gate_ref, b_gate_ref, g_ret_ref, g_ml_ref,
         w_out_ref, g_post_ref, s_in, c_in, n_in, m_in,
         y_ref, s_out, c_out, n_out, m_out,
         h_ref, proj_ref, mix_ref, mixed_ref, gate_ref, bcum_ref, gate_t_ref, bcum_t_ref) = refs

    masks = _Masks(seg)
    scale = HEAD_DIM ** -0.5
    tri = jnp.where(masks.causal, 1.0, 0.0).astype(F32)
    cos = cos_ref[...]
    sin = sin_ref[...]

    def pre_norm(blocks):
        for blk in blocks:
            rows = pl.ds(blk * BLK, BLK)
            h_ref[rows, :] = _rmsnorm(x_ref[blk], g_pre_ref[...]).astype(BF16)

    def gate_projection(rows_all):
        gate_ref[rows_all, :] = _dot(h_ref[rows_all, :], w_gate_ref[...]) + b_gate_ref[...]

    def gate_cumsums(blocks):
        for blk in blocks:
            rows = pl.ds(blk * BLK, BLK)
            gates = gate_ref[rows, :]
            bcum = _dot(tri, jax.nn.log_sigmoid(gates), precision=lax.Precision.HIGHEST)
            bcum_ref[rows, :] = bcum
            gate_t_ref[blk] = gates.T
            bcum_t_ref[blk] = bcum.T

    def rotary(t):
        return t * cos + pltpu.roll(t, HEAD_DIM // 2, axis=1) * sin

    def seq_rows(j):
        return slice(j * seg, (j + 1) * seg)

    def lane_mask(j):
        c = lax.broadcasted_iota(jnp.int32, (BLK, BLK), 1)
        return (c >= j * seg) & (c < (j + 1) * seg)

    def ret_head(proj, blk, head, col):
        rows = pl.ds(blk * BLK, BLK)
        lg = _ret_log_gamma(head)
        q = rotary(proj[rows,col:col + HEAD_DIM])
        k = rotary(proj[rows,GROUP_W + col:GROUP_W + col + HEAD_DIM]) * scale
        v = proj[rows,2 * GROUP_W + col:2 * GROUP_W + col + HEAD_DIM]
        g = proj[rows,3 * GROUP_W + col:3 * GROUP_W + col + HEAD_DIM]
        qb, kb, vb = q.astype(BF16), k.astype(BF16), v.astype(BF16)
        dmat = jnp.where(masks.causal, jnp.exp(jnp.where(masks.causal, masks.diff * lg, 0.0)), 0.0)
        xi = jnp.exp((masks.tpos + 1.0) * lg)
        zeta = jnp.exp((float(seg - 1) - masks.tpos) * lg)
        chunk_decay = math.exp(seg * lg)
        scores = _dot_nt(qb, kb)
        kz_t = (k * zeta).T
        inter = []
        kz_tb = kz_t.astype(BF16)
        updates = _dot(jnp.concatenate(
            [jnp.where(lane_mask(j), kz_tb, jnp.zeros_like(kz_tb)) for j in range(n_seq)], axis=0), vb)
        for j in range(n_seq):
            seq = blk * n_seq + j
            s_prev = s_in[seq, head]
            inter.append(_dot(q[seq_rows(j)].astype(BF16), s_prev.astype(BF16)))
            s_out[seq, head] = s_prev * chunk_decay + updates[j * HEAD_DIM:(j + 1) * HEAD_DIM]
        inter = jnp.concatenate(inter, axis=0)
        yield
        out = _dot((scores * dmat).astype(BF16), vb)
        yield
        out = out + inter * xi
        y = _groupnorm(out, g_ret_ref[:, head * HEAD_DIM:(head + 1) * HEAD_DIM]) * (g * jax.nn.sigmoid(g))
        mix_ref[rows, head * HEAD_DIM:(head + 1) * HEAD_DIM] = y.astype(BF16)

    def ml_head(proj, blk, head, col):
        rows = pl.ds(blk * BLK, BLK)
        q = proj[rows,col:col + HEAD_DIM]
        k = proj[rows,GROUP_W + col:GROUP_W + col + HEAD_DIM] * scale
        v = proj[rows,2 * GROUP_W + col:2 * GROUP_W + col + HEAD_DIM]
        o = proj[rows,3 * GROUP_W + col:3 * GROUP_W + col + HEAD_DIM]
        qb, kb, vb = q.astype(BF16), k.astype(BF16), v.astype(BF16)
        i_col = gate_ref[rows, head:head + 1]
        b_col = bcum_ref[rows, N_ML_HEADS + head:N_ML_HEADS + head + 1]
        i_row = gate_t_ref[blk, head:head + 1, :]
        b_row = bcum_t_ref[blk, N_ML_HEADS + head:N_ML_HEADS + head + 1, :]
        m_prev_seq = m_in[blk, head:head + 1, :]
        m_prev = _rows_from_seq(masks.col_is_rowseq, m_prev_seq)
        logw = jnp.where(masks.causal, b_col - b_row + i_row, -jnp.inf)
        m_t = jnp.maximum(b_col + m_prev, jnp.max(logw, axis=1, keepdims=True))
        w = jnp.exp(logw - m_t)
        inter_w = jnp.exp(b_col + m_prev - m_t)
        s = _dot_nt(qb, kb)
        m_new_seq = _seq_from_last_row(masks.row_is_last_of_colseq, m_t)
        b_last_seq = _seq_from_last_row(masks.row_is_last_of_colseq, b_col)
        decay_seq = jnp.exp(b_last_seq + m_prev_seq - m_new_seq)
        m_new = _rows_from_seq(masks.col_is_rowseq, m_new_seq)
        b_last = jnp.sum(jnp.where(masks.col_is_last_of_rowseq, b_row, 0.0), axis=1, keepdims=True)
        wl = jnp.exp(b_last - b_col + i_col - m_new)
        vw_t = (v * wl).T
        kw = k * wl
        qc, qn = [], []
        for j in range(n_seq):
            seq = blk * n_seq + j
            c_prev = c_in[seq, head]
            n_prev = n_in[seq, head:head + 1, :]
            qc.append(_dot_nt(q[seq_rows(j)].astype(BF16), c_prev.astype(BF16)))
            qn.append(jnp.sum(q[seq_rows(j)] * n_prev, axis=1, keepdims=True))
            decay = decay_seq[:, j:j + 1]
            n_out[seq, head:head + 1, :] = decay * n_prev + jnp.sum(kw[seq_rows(j)], axis=0, keepdims=True)
        qc = jnp.concatenate(qc, axis=0)
        qn = jnp.concatenate(qn, axis=0)
        m_out[blk, head:head + 1, :] = m_new_seq
        yield
        s = s * w
        num = _dot(s.astype(BF16), vb)
        den = jnp.sum(s, axis=1, keepdims=True)
        vw_tb = vw_t.astype(BF16)
        updates = _dot(jnp.concatenate(
            [jnp.where(lane_mask(j), vw_tb, jnp.zeros_like(vw_tb)) for j in range(n_seq)], axis=0), kb)
        for j in range(n_seq):
            seq = blk * n_seq + j
            c_out[seq, head] = (decay_seq[:, j:j + 1] * c_in[seq, head]
                                + updates[j * HEAD_DIM:(j + 1) * HEAD_DIM])
        yield
        num = num + inter_w * qc
        den = jnp.maximum(jnp.abs(den + inter_w * qn), jnp.exp(-m_t))
        hout = num / den
        y = _groupnorm(hout, g_ml_ref[:, head * HEAD_DIM:(head + 1) * HEAD_DIM]) * jax.nn.sigmoid(o)
        mix_ref[rows, RET_W + head * HEAD_DIM:RET_W + (head + 1) * HEAD_DIM] = y.astype(BF16)

    r_idx = lax.broadcasted_iota(jnp.int32, (BLK, BLK), 0)
    c_idx = lax.broadcasted_iota(jnp.int32, (BLK, BLK), 1)
    causal_t = r_idx <= c_idx
    lane_pos = lax.broadcasted_iota(jnp.int32, (1, BLK), 1).astype(F32)
    head_consts = {}

    def ret_consts(head):
        if head not in head_consts:
            lg = _ret_log_gamma(head)
            dmat_t = jnp.where(causal_t, jnp.exp(jnp.where(causal_t, (c_idx - r_idx).astype(F32) * lg, 0.0)), 0.0)
            xi_row = jnp.exp((lane_pos + 1.0) * lg)
            zeta = jnp.exp((float(BLK - 1) - r_idx.astype(F32)) * lg)
            head_consts[head] = (dmat_t, xi_row, zeta, math.exp(BLK * lg))
        return head_consts[head]

    def channel_norm_t(h_t):
        mu = jnp.mean(h_t, axis=0, keepdims=True)
        d = h_t - mu
        var = jnp.mean(d * d, axis=0, keepdims=True)
        return d * lax.rsqrt(var + EPS)

    def queries_of_pair(q_pair):
        zero = jnp.zeros_like(q_pair[0])
        return jnp.concatenate([jnp.concatenate([q_pair[0], zero], axis=1),
                                jnp.concatenate([zero, q_pair[1]], axis=1)], axis=0)

    def pair_cols(i):
        return slice(i * HEAD_DIM, (i + 1) * HEAD_DIM)

    def ret_pair_single(proj, blk, first_head):
        rows = pl.ds(blk * BLK, BLK)
        pair = range(HEADS_PER_GROUP)
        consts = [ret_consts(first_head + i) for i in pair]
        q = [rotary(proj[rows, pair_cols(i)]) for i in pair]
        k = [rotary(proj[rows, GROUP_W + i * HEAD_DIM:GROUP_W + (i + 1) * HEAD_DIM]) * scale for i in pair]
        v_tb = [proj[rows, 2 * GROUP_W + i * HEAD_DIM:2 * GROUP_W + (i + 1) * HEAD_DIM].T.astype(BF16)
                for i in pair]
        lhs = jnp.concatenate([
            jnp.concatenate([k[i].astype(BF16) for i in pair], axis=1),
            jnp.concatenate([st_ref[blk, first_head + i].astype(BF16) for i in pair], axis=1)], axis=0)
        first = _dot_nt(lhs, queries_of_pair([q[i].astype(BF16) for i in pair]))
        kz_b = [(k[i] * consts[i][2]).astype(BF16) for i in pair]
        yield
        out_t = []
        for i in pair:
            dmat_t, _, _, chunk_decay = consts[i]
            scores_t = first[:BLK, pair_cols(i)]
            both = _dot(v_tb[i], jnp.concatenate([kz_b[i], (scores_t * dmat_t).astype(BF16)], axis=1))
            st_ref[blk, first_head + i] = st_ref[blk, first_head + i] * chunk_decay + both[:, :HEAD_DIM]
            out_t.append(both[:, HEAD_DIM:])
        yield
        for i in pair:
            head = first_head + i
            out = out_t[i] + first[BLK:, pair_cols(i)] * consts[i][1]
            g = proj[rows, 3 * GROUP_W + i * HEAD_DIM:3 * GROUP_W + (i + 1) * HEAD_DIM]
            y = channel_norm_t(out).T * g_ret_ref[:, head * HEAD_DIM:(head + 1) * HEAD_DIM] * (g * jax.nn.sigmoid(g))
            mix_ref[rows, head * HEAD_DIM:(head + 1) * HEAD_DIM] = y.astype(BF16)

    def ml_pair_single(proj, blk, first_head):
        rows = pl.ds(blk * BLK, BLK)
        pair = range(HEADS_PER_GROUP)
        q = [proj[rows, pair_cols(i)] for i in pair]
        k = [proj[rows, GROUP_W + i * HEAD_DIM:GROUP_W + (i + 1) * HEAD_DIM] * scale for i in pair]
        kb = [k[i].astype(BF16) for i in pair]
        n_prev = [n_in[blk, first_head + i:first_head + i + 1, :] for i in pair]
        lhs = jnp.concatenate([
            jnp.concatenate(kb, axis=1),
            jnp.concatenate([c_in[blk, first_head + i].astype(BF16) for i in pair], axis=1),
            jnp.concatenate([jnp.broadcast_to(n_prev[i], (2 * SUBLANES, HEAD_DIM)).astype(BF16) for i in pair],
                            axis=1)], axis=0)
        first = _dot_nt(lhs, queries_of_pair([q[i].astype(BF16) for i in pair]))
        heads = [ml_head_single(proj, blk, first_head + i, i * HEAD_DIM, k[i], kb[i], n_prev[i],
                                first[:BLK, pair_cols(i)], first[BLK:2 * BLK, pair_cols(i)],
                                first[2 * BLK:2 * BLK + 1, pair_cols(i)]) for i in pair]
        for _ in range(3):
            for head in heads:
                next(head, None)
            yield

    def ml_head_single(proj, blk, head, col, k, kb, n_prev, s_t, qc_t, qn):
        rows = pl.ds(blk * BLK, BLK)
        v_t = proj[rows, 2 * GROUP_W + col:2 * GROUP_W + col + HEAD_DIM].T
        i_row = gate_t_ref[blk, head:head + 1, :]
        b_row = bcum_t_ref[blk, N_ML_HEADS + head:N_ML_HEADS + head + 1, :]
        key_term = gate_ref[rows, head:head + 1] - bcum_ref[rows, N_ML_HEADS + head:N_ML_HEADS + head + 1]
        m_prev = m_in[blk, head:head + 1, 0:1]
        logw_t = jnp.where(causal_t, b_row + key_term, -jnp.inf)
        m_t = jnp.maximum(b_row + m_prev, jnp.max(logw_t, axis=0, keepdims=True))
        w_t = jnp.exp(logw_t - m_t)
        inter_w = jnp.exp(b_row + m_prev - m_t)
        m_new = m_t[:, BLK - 1:BLK]
        b_last = b_row[:, BLK - 1:BLK]
        decay = jnp.exp(b_last + m_prev - m_new)
        wl = jnp.exp(b_last - b_row + i_row - m_new)
        m_out[blk, head:head + 1, :] = jnp.broadcast_to(m_new, (1, BLK))
        wk = _dot(jnp.broadcast_to(wl, (2 * SUBLANES, BLK)).astype(BF16), kb)[0:1, :]
        n_out[blk, head:head + 1, :] = decay * n_prev + wk
        kw_b = (k * jnp.exp(key_term + (b_last - m_new))).astype(BF16)
        v_tb = v_t.astype(BF16)
        yield
        s_t = s_t * w_t
        den = jnp.sum(s_t, axis=0, keepdims=True)
        both = _dot(v_tb, jnp.concatenate([kw_b, s_t.astype(BF16)], axis=1))
        c_out[blk, head] = decay * c_in[blk, head] + both[:, :HEAD_DIM]
        num_t = both[:, HEAD_DIM:]
        yield
        num_t = num_t + inter_w * qc_t
        den = jnp.maximum(jnp.abs(den + inter_w * qn), jnp.exp(-m_t))
        o = proj[rows, 3 * GROUP_W + col:3 * GROUP_W + col + HEAD_DIM]
        y = (channel_norm_t(num_t / den).T * g_ml_ref[:, head * HEAD_DIM:(head + 1) * HEAD_DIM]
             * jax.nn.sigmoid(o))
        mix_ref[rows, RET_W + head * HEAD_DIM:RET_W + (head + 1) * HEAD_DIM] = y.astype(BF16)

    n_ret_groups = N_RET_HEADS // HEADS_PER_GROUP
    n_groups = n_ret_groups + N_ML_HEADS // HEADS_PER_GROUP

    def first_head_of(grp):
        return (grp if grp < n_ret_groups else grp - n_ret_groups) * HEADS_PER_GROUP

    def project(grp, kind, rows_all):
        is_ret = grp < n_ret_groups
        c0 = (0 if is_ret else 4 * RET_W) + kind * (RET_W if is_ret else ML_W) + first_head_of(grp) * HEAD_DIM
        proj_ref[grp % 2, rows_all, kind * GROUP_W:(kind + 1) * GROUP_W] = _dot(
            h_ref[rows_all, :], w_in_ref[:, c0:c0 + GROUP_W])

    def out_project(grp, rows_all):
        cols = slice(grp * GROUP_W, (grp + 1) * GROUP_W)
        part = _dot(mix_ref[rows_all, cols], w_out_ref[cols, :])
        if grp == 0:
            mixed_ref[rows_all, :] = part
        else:
            mixed_ref[rows_all, :] += part

    def row_pipeline(blocks):
        rows_all = pl.ds(blocks[0] * BLK, len(blocks) * BLK)
        pre_norm(blocks)
        for kind in range(4):
            project(0, kind, rows_all)
        yield
        for grp in range(n_groups):
            if n_seq == 1:
                pair_fn = ret_pair_single if grp < n_ret_groups else ml_pair_single
                heads = [pair_fn(proj_ref.at[grp % 2], blk, first_head_of(grp)) for blk in blocks]
            else:
                head_fn = ret_head if grp < n_ret_groups else ml_head
                heads = [head_fn(proj_ref.at[grp % 2], blk, first_head_of(grp) + hh, hh * HEAD_DIM)
                         for blk in blocks for hh in range(HEADS_PER_GROUP)]
            fillers = {0: [functools.partial(project, grp + 1, kind, rows_all) for kind in range(4)]
                       if grp + 1 < n_groups else [],
                       1: [functools.partial(out_project, grp - 1, rows_all)] if grp >= 1 else [],
                       2: []}
            if grp == 0:
                fillers[0].append(functools.partial(gate_projection, rows_all))
                fillers[1].append(functools.partial(gate_cumsums, blocks))
            for phase in range(3):
                todo = list(fillers[phase])
                every = max(1, len(heads) // max(1, len(todo)))
                for i, head in enumerate(heads):
                    next(head, None)
                    if todo and (i + 1) % every == 0:
                        todo.pop(0)()
                for filler in todo:
                    filler()
            yield
        last_cols = slice((n_groups - 1) * GROUP_W, n_groups * GROUP_W)
        last = _dot(mix_ref[rows_all, last_cols], w_out_ref[last_cols, :])
        for i, blk in enumerate(blocks):
            rows = pl.ds(blk * BLK, BLK)
            mixed = mixed_ref[rows, :] + last[i * BLK:(i + 1) * BLK]
            y_ref[blk] = x_ref[blk] + _rmsnorm(mixed, g_post_ref[...])

    n_runs = 2 if n_blk % 2 == 0 else 1
    per_run = n_blk // n_runs
    pipelines = [row_pipeline(list(range(i * per_run, (i + 1) * per_run))) for i in range(n_runs)]
    n_stages = n_groups + 2
    for step in range(n_stages + n_runs - 1):
        for lag, pipeline in enumerate(pipelines):
            if 0 <= step - lag < n_stages:
                next(pipeline, None)

    if fresh_state:
        @pl.when(pl.program_id(1) == pl.num_programs(1) - 1)
        def _():
            for blk in range(n_blk):
                for head in range(N_RET_HEADS):
                    s_out[blk, head] = st_ref[blk, head].T


def _ffn_kernel(*refs, seg, n_blk, fresh_state):
    n_seq_tile = n_blk * BLK // seg
    rows_tile = n_blk * BLK
    if fresh_state:
        (x_ref, g_pre_ref, w_up_ref, conv_w_ref, conv_b_ref, w_down_ref, g_post_ref,
         y_ref, buf_out, h_ref, act_ref, ffn_ref) = refs
        buf_in = buf_out

        @pl.when(pl.program_id(1) == 0)
        def _():
            buf_out[...] = jnp.zeros(buf_out.shape, F32)
    else:
        (x_ref, g_pre_ref, w_up_ref, conv_w_ref, conv_b_ref, w_down_ref, g_post_ref, buf_in,
         y_ref, buf_out, h_ref, act_ref, ffn_ref) = refs

    for blk in range(n_blk):
        rows = pl.ds(blk * BLK, BLK)
        h_ref[rows, :] = _rmsnorm(x_ref[blk], g_pre_ref[...]).astype(BF16)

    half_w = FF_BLK // 2
    tpos = lax.broadcasted_iota(jnp.int32, (n_seq_tile, seg, half_w), 1)

    def conv(up, cols):
        prev0 = buf_in[:, 0:1, cols]
        prev1 = buf_in[:, 1:2, cols]
        up3 = up.reshape(n_seq_tile, seg, half_w)
        sh1 = pltpu.roll(up, 1, axis=0).reshape(n_seq_tile, seg, half_w)
        sh2 = pltpu.roll(up, 2, axis=0).reshape(n_seq_tile, seg, half_w)
        sh1 = jnp.where(tpos == 0, prev1, sh1)
        sh2 = jnp.where(tpos == 0, prev0, jnp.where(tpos == 1, prev1, sh2))
        buf_out[:, 0:1, cols] = up3[:, seg - 2:seg - 1, :]
        buf_out[:, 1:2, cols] = up3[:, seg - 1:seg, :]
        w = conv_w_ref[:, cols]
        out = sh2 * w[0:1, :] + sh1 * w[1:2, :] + up3 * w[2:3, :] + conv_b_ref[:, cols]
        return out.reshape(rows_tile, half_w)

    def activation(gate, val):
        c0 = -2.0 * math.log2(math.e) * math.sqrt(2.0 / math.pi)
        c1 = 0.044715 * c0
        return ((gate * val) / (1.0 + jnp.exp2(gate * (gate * gate * c1 + c0)))).astype(BF16)

    for j in range(D_FF // FF_BLK):
        up_gate = _dot(h_ref[...], w_up_ref[:, j * FF_BLK:(j + 1) * FF_BLK])
        up_val = _dot(h_ref[...], w_up_ref[:, D_FF + j * FF_BLK:D_FF + (j + 1) * FF_BLK])
        for half in range(2):
            lanes = slice(half * half_w, (half + 1) * half_w)
            cols = slice(j * FF_BLK + half * half_w, j * FF_BLK + (half + 1) * half_w)
            gate = conv(up_gate[:, lanes], cols)
            val = conv(up_val[:, lanes], slice(D_FF + cols.start, D_FF + cols.stop))
            act_ref[:, cols] = activation(gate, val)

    ffn_ref[...] = _dot(act_ref[...], w_down_ref[...])
    for blk in range(n_blk):
        rows = pl.ds(blk * BLK, BLK)
        y_ref[blk] = x_ref[blk] + _rmsnorm(ffn_ref[rows, :], g_post_ref[...])


def _const_spec(shape):
    zeros = (0,) * len(shape)
    return pl.BlockSpec(shape, lambda *_: zeros, pipeline_mode=pl.Buffered(1))


def _compiler_params(n_grid_dims):
    return pltpu.CompilerParams(
        dimension_semantics=("arbitrary",) * n_grid_dims,
        vmem_limit_bytes=VMEM_LIMIT_BYTES)


def _mixer_scratch(n_blk, fresh_state):
    rows = n_blk * BLK
    carried = [pltpu.VMEM((n_blk, N_RET_HEADS, HEAD_DIM, HEAD_DIM), F32)] if fresh_state else []
    return [
        pltpu.VMEM((rows, D_MODEL), BF16),
        pltpu.VMEM((2, rows, 4 * GROUP_W), F32),
        pltpu.VMEM((rows, RET_W + ML_W), BF16),
        pltpu.VMEM((rows, D_MODEL), F32),
        pltpu.VMEM((rows, BLK), F32),
        pltpu.VMEM((rows, BLK), F32),
        pltpu.VMEM((n_blk, BLK, BLK), F32),
        pltpu.VMEM((n_blk, BLK, BLK), F32),
    ] + carried


def _mixer_weight_specs():
    return [
        _const_spec((1, D_MODEL)),
        _const_spec((D_MODEL, 4 * RET_W + 4 * ML_W + 2 * N_ML_HEADS)),
        _const_spec((D_MODEL, BLK)),
        _const_spec((1, BLK)),
        _const_spec((1, RET_W)),
        _const_spec((1, ML_W)),
        _const_spec((RET_W + ML_W, D_MODEL)),
        _const_spec((1, D_MODEL)),
    ]


def _state_shapes(n_seqs, n_blocks):
    return [
        jax.ShapeDtypeStruct((n_seqs, N_RET_HEADS, HEAD_DIM, HEAD_DIM), F32),
        jax.ShapeDtypeStruct((n_seqs, N_ML_HEADS, HEAD_DIM, HEAD_DIM), F32),
        jax.ShapeDtypeStruct((n_seqs, N_ML_HEADS, HEAD_DIM), F32),
        jax.ShapeDtypeStruct((n_blocks, N_ML_HEADS, BLK), F32),
    ]


def _state_specs(n_seq_tile, n_blk, index):
    return [
        pl.BlockSpec((n_seq_tile, N_RET_HEADS, HEAD_DIM, HEAD_DIM), lambda *g: (index(*g), 0, 0, 0)),
        pl.BlockSpec((n_seq_tile, N_ML_HEADS, HEAD_DIM, HEAD_DIM), lambda *g: (index(*g), 0, 0, 0)),
        pl.BlockSpec((n_seq_tile, N_ML_HEADS, HEAD_DIM), lambda *g: (index(*g), 0, 0)),
        pl.BlockSpec((n_blk, N_ML_HEADS, BLK), lambda *g: (index(*g), 0, 0)),
    ]


def _prompt_mixer(x, cos, sin, weights, n_blk):
    batch, seq_len, _ = x.shape
    grid = (batch // n_blk, seq_len // BLK)
    x_spec = pl.BlockSpec((n_blk, BLK, D_MODEL), lambda g, c: (g, c, 0))
    rope_spec = pl.BlockSpec((BLK, HEAD_DIM), lambda g, c: (c, 0))
    return pl.pallas_call(
        functools.partial(_mixer_kernel, seg=BLK, n_blk=n_blk, fresh_state=True),
        grid=grid,
        in_specs=[x_spec, rope_spec, rope_spec] + _mixer_weight_specs(),
        out_specs=[x_spec] + _state_specs(n_blk, n_blk, lambda g, c: g),
        out_shape=[jax.ShapeDtypeStruct(x.shape, F32)] + _state_shapes(batch, batch),
        scratch_shapes=_mixer_scratch(n_blk, fresh_state=True),
        compiler_params=_compiler_params(2),
        name="prompt_mixer",
    )(x, cos, sin, *weights)


def _sample_mixer(x, cos, sin, weights, states, seg, n_blk):
    n_blocks = x.shape[0]
    n_seq_tile = n_blk * BLK // seg
    grid = (n_blocks // n_blk,)
    x_spec = pl.BlockSpec((n_blk, BLK, D_MODEL), lambda g: (g, 0, 0))
    rope_spec = _const_spec((BLK, HEAD_DIM))
    state_specs = _state_specs(n_seq_tile, n_blk, lambda g: g)
    return pl.pallas_call(
        functools.partial(_mixer_kernel, seg=seg, n_blk=n_blk, fresh_state=False),
        grid=grid,
        in_specs=[x_spec, rope_spec, rope_spec] + _mixer_weight_specs() + state_specs,
        out_specs=[x_spec] + state_specs,
        out_shape=[jax.ShapeDtypeStruct(x.shape, F32)]
        + _state_shapes(n_blocks * BLK // seg, n_blocks),
        scratch_shapes=_mixer_scratch(n_blk, fresh_state=False),
        compiler_params=_compiler_params(1),
        name="sample_mixer",
    )(x, cos, sin, *weights, *states)


def _ffn_scratch(n_blk):
    rows = n_blk * BLK
    return [
        pltpu.VMEM((rows, D_MODEL), BF16),
        pltpu.VMEM((rows, D_FF), BF16),
        pltpu.VMEM((rows, D_MODEL), F32),
    ]


def _ffn_weight_specs():
    return [
        _const_spec((1, D_MODEL)),
        _const_spec((D_MODEL, 2 * D_FF)),
        _const_spec((CONV_W, 2 * D_FF)),
        _const_spec((1, 2 * D_FF)),
        _const_spec((D_FF, D_MODEL)),
        _const_spec((1, D_MODEL)),
    ]


def _prompt_ffn(x, weights, n_blk):
    batch, seq_len, _ = x.shape
    grid = (batch // n_blk, seq_len // BLK)
    x_spec = pl.BlockSpec((n_blk, BLK, D_MODEL), lambda g, c: (g, c, 0))
    buf_spec = pl.BlockSpec((n_blk, CONV_W - 1, 2 * D_FF), lambda g, c: (g, 0, 0))
    return pl.pallas_call(
        functools.partial(_ffn_kernel, seg=BLK, n_blk=n_blk, fresh_state=True),
        grid=grid,
        in_specs=[x_spec] + _ffn_weight_specs(),
        out_specs=[x_spec, buf_spec],
        out_shape=[jax.ShapeDtypeStruct(x.shape, F32),
                   jax.ShapeDtypeStruct((batch, CONV_W - 1, 2 * D_FF), F32)],
        scratch_shapes=_ffn_scratch(n_blk),
        compiler_params=_compiler_params(2),
        name="prompt_ffn",
    )(x, *weights)


def _sample_ffn(x, weights, conv_buf, seg, n_blk):
    n_blocks = x.shape[0]
    n_seq_tile = n_blk * BLK // seg
    grid = (n_blocks // n_blk,)
    x_spec = pl.BlockSpec((n_blk, BLK, D_MODEL), lambda g: (g, 0, 0))
    buf_spec = pl.BlockSpec((n_seq_tile, CONV_W - 1, 2 * D_FF), lambda g: (g, 0, 0))
    return pl.pallas_call(
        functools.partial(_ffn_kernel, seg=seg, n_blk=n_blk, fresh_state=False),
        grid=grid,
        in_specs=[x_spec] + _ffn_weight_specs() + [buf_spec],
        out_specs=[x_spec, buf_spec],
        out_shape=[jax.ShapeDtypeStruct(x.shape, F32),
                   jax.ShapeDtypeStruct(conv_buf.shape, F32)],
        scratch_shapes=_ffn_scratch(n_blk),
        compiler_params=_compiler_params(1),
        name="sample_ffn",
    )(x, *weights, conv_buf)


def _rope_freqs():
    return ROPE_BASE ** (-jnp.arange(0, HEAD_DIM, 2, dtype=F32) / HEAD_DIM)


def _rope_layout(cos, sin):
    return jnp.concatenate([cos, cos], axis=-1), jnp.concatenate([-sin, sin], axis=-1)


def _rope_tables(pos):
    ang = pos.astype(F32)[:, None] * _rope_freqs()[None, :]
    return _rope_layout(jnp.cos(ang), jnp.sin(ang))


def _rope_tables_by_chunk(n_chunks):
    freqs = _rope_freqs()[None, :]
    base = (BLK * jnp.arange(n_chunks, dtype=jnp.int32)).astype(F32)[:, None] * freqs
    step = jnp.arange(BLK, dtype=jnp.int32).astype(F32)[:, None] * freqs
    cos_b, sin_b = jnp.cos(base)[:, None, :], jnp.sin(base)[:, None, :]
    cos_s, sin_s = jnp.cos(step)[None], jnp.sin(step)[None]
    cos = (cos_b * cos_s - sin_b * sin_s).reshape(n_chunks * BLK, HEAD_DIM // 2)
    sin = (sin_b * cos_s + cos_b * sin_s).reshape(n_chunks * BLK, HEAD_DIM // 2)
    return _rope_layout(cos, sin)


def kernel(x_prompt, x_sample, state_ret, state_mlstm_C, state_mlstm_n, state_mlstm_m, cache_ffn_conv, pre_mix_gain, w_in, b_gates, ret_head_gain, mlstm_head_gain, w_out, post_mix_gain, pre_ffn_gain, w_up, conv_w, conv_b, w_down, post_ffn_gain):
    depth = w_in.shape[0]
    assert depth == 1
    batch, seq_len, _ = x_prompt.shape
    dec_batch, dec_seq, _ = x_sample.shape
    assert seq_len % BLK == 0 and BLK % dec_seq == 0 and (dec_batch * dec_seq) % BLK == 0
    seqs_per_blk = BLK // dec_seq
    n_sample_blocks = dec_batch * dec_seq // BLK
    layer = 0

    n_gate = 2 * N_ML_HEADS
    w_gate = jnp.pad(w_in[layer][:, -n_gate:], ((0, 0), (0, BLK - n_gate))).astype(BF16)
    b_gate = jnp.pad(b_gates[layer], (0, BLK - n_gate)).reshape(1, BLK)
    mixer_weights = (
        pre_mix_gain[layer].reshape(1, D_MODEL), w_in[layer].astype(BF16), w_gate, b_gate,
        ret_head_gain[layer].reshape(1, RET_W), mlstm_head_gain[layer].reshape(1, ML_W),
        w_out[layer].astype(BF16), post_mix_gain[layer].reshape(1, D_MODEL))
    ffn_weights = (
        pre_ffn_gain[layer].reshape(1, D_MODEL), w_up[layer].astype(BF16), conv_w[layer],
        conv_b[layer].reshape(1, 2 * D_FF), w_down[layer].astype(BF16),
        post_ffn_gain[layer].reshape(1, D_MODEL))

    cos_p, sin_p = _rope_tables_by_chunk(seq_len // BLK)
    assert batch % PROMPT_BLOCKS_PER_STEP == 0
    x1_p, s_p, c_p, n_p, m_p = _prompt_mixer(x_prompt, cos_p, sin_p, mixer_weights, n_blk=PROMPT_BLOCKS_PER_STEP)
    y_p, buf_p = _prompt_ffn(x1_p, ffn_weights, n_blk=PROMPT_BLOCKS_PER_STEP)
    m_p = m_p[:, :, 0]

    pos_s = PAST_LEN + jnp.arange(dec_seq, dtype=jnp.int32)
    cos_s, sin_s = _rope_tables(jnp.tile(pos_s, seqs_per_blk))
    m_blocks = jnp.transpose(state_mlstm_m[layer].reshape(n_sample_blocks, seqs_per_blk, N_ML_HEADS), (0, 2, 1))
    m_blocks = jnp.pad(m_blocks, ((0, 0), (0, 0), (0, BLK - seqs_per_blk)))
    xs = x_sample.reshape(n_sample_blocks, BLK, D_MODEL)
    x1_s, s_s, c_s, n_s, m_s = _sample_mixer(
        xs, cos_s, sin_s, mixer_weights,
        (state_ret[layer], state_mlstm_C[layer], state_mlstm_n[layer], m_blocks),
        seg=dec_seq, n_blk=SAMPLE_MIXER_BLOCKS_PER_STEP)
    y_s, buf_s = _sample_ffn(x1_s, ffn_weights, cache_ffn_conv[layer], seg=dec_seq,
                             n_blk=SAMPLE_FFN_BLOCKS_PER_STEP)
    y_s = y_s.reshape(x_sample.shape)
    m_s = jnp.transpose(m_s[:, :, :seqs_per_blk], (0, 2, 1)).reshape(dec_batch, N_ML_HEADS)

    return (y_p, y_s, s_p[None], s_s[None], c_p[None], c_s[None], n_p[None], n_s[None],
            m_p[None], m_s[None], buf_p[None], buf_s[None])
```

```python
import functools
import math

import jax
import jax.numpy as jnp
from jax import lax
from jax.experimental import pallas as pl
from jax.experimental.pallas import tpu as pltpu

D_MODEL = 1024
HEAD_DIM = 128
N_RET_HEADS = 4
N_ML_HEADS = 4
RET_W = N_RET_HEADS * HEAD_DIM
ML_W = N_ML_HEADS * HEAD_DIM
D_FF = 2816
CONV_W = 3
PAST_LEN = 16384
ROPE_BASE = 10000.0
EPS = 1e-6
M_INIT = -1e30

MXU_WIDTH = 256
SUBLANES = 8
V7X_VMEM_BYTES = 64 * 1024 * 1024
VMEM_LIMIT_BYTES = V7X_VMEM_BYTES - 8 * 1024 * 1024

BLK = 128
HEADS_PER_GROUP = MXU_WIDTH // HEAD_DIM
GROUP_W = HEADS_PER_GROUP * HEAD_DIM
FF_BLK = MXU_WIDTH

PROMPT_BLOCKS_PER_STEP = 8
SAMPLE_MIXER_BLOCKS_PER_STEP = 1
SAMPLE_FFN_BLOCKS_PER_STEP = 2

F32 = jnp.float32
BF16 = jnp.bfloat16


def _dot(a, b, precision=None):
    return jnp.dot(a, b, preferred_element_type=F32, precision=precision)


def _dot_nt(a, b):
    return lax.dot_general(a, b, (((1,), (1,)), ((), ())), preferred_element_type=F32)


def _rmsnorm(x, g):
    return x * lax.rsqrt(jnp.mean(x * x, axis=-1, keepdims=True) + EPS) * g


def _groupnorm(h, g):
    mu = jnp.mean(h, axis=-1, keepdims=True)
    d = h - mu
    var = jnp.mean(d * d, axis=-1, keepdims=True)
    return d * lax.rsqrt(var + EPS) * g


def _ret_log_gamma(h):
    return math.log(1.0 - 2.0 ** (-5.0 - h))


class _Masks:
    def __init__(self, seg):
        shift = seg.bit_length() - 1
        r = lax.broadcasted_iota(jnp.int32, (BLK, BLK), 0)
        c = lax.broadcasted_iota(jnp.int32, (BLK, BLK), 1)
        rseq = r >> shift
        self.causal = (rseq == (c >> shift)) & (r >= c)
        self.tpos = (r & (seg - 1)).astype(F32)
        self.diff = (r - c).astype(F32)
        self.col_is_last_of_rowseq = c == (rseq * seg + (seg - 1))
        self.col_is_rowseq = c == rseq
        self.row_is_last_of_colseq = r == (c * seg + (seg - 1))


def _rows_from_seq(mask_col_is_rowseq, seq_row):
    return jnp.sum(jnp.where(mask_col_is_rowseq, seq_row, 0.0), axis=1, keepdims=True)


def _seq_from_last_row(mask_row_is_last_of_colseq, col):
    return jnp.sum(jnp.where(mask_row_is_last_of_colseq, col, 0.0), axis=0, keepdims=True)


def _mixer_kernel(*refs, seg, n_blk, fresh_state):
    n_seq = BLK // seg
    assert fresh_state == (n_seq == 1)
    if fresh_state:
        (x_ref, cos_ref, sin_ref, g_pre_ref, w_in_ref, w_gate_ref, b_gate_ref, g_ret_ref, g_ml_ref,
         w_out_ref, g_post_ref,
         y_ref, s_out, c_out, n_out, m_out,
         h_ref, proj_ref, mix_ref, mixed_ref, gate_ref, bcum_ref, gate_t_ref, bcum_t_ref, st_ref) = refs
        c_in, n_in, m_in = c_out, n_out, m_out

        @pl.when(pl.program_id(1) == 0)
        def _():
            st_ref[...] = jnp.zeros(st_ref.shape, F32)
            c_out[...] = jnp.zeros(c_out.shape, F32)
            n_out[...] = jnp.zeros(n_out.shape, F32)
            m_out[...] = jnp.full(m_out.shape, M_INIT, F32)
    else:
        (x_ref, cos_ref, sin_ref, g_pre_ref, w_in_ref, w_gate_ref, b_gate_ref, g_ret_ref, g_ml_ref,
         w_out_ref, g_post_ref, s_in, c_in, n_in, m_in,
         y_ref, s_out, c_out, n_out, m_out,
         h_ref, proj_ref, mix_ref, mixed_ref, gate_ref, bcum_ref, gate_t_ref, bcum_t_ref) = refs

    masks = _Masks(seg)
    scale = HEAD_DIM ** -0.5
    tri = jnp.where(masks.causal, 1.0, 0.0).astype(F32)
    cos = cos_ref[...]
    sin = sin_ref[...]

    def pre_norm(blocks):
        for blk in blocks:
            rows = pl.ds(blk * BLK, BLK)
            h_ref[rows, :] = _rmsnorm(x_ref[blk], g_pre_ref[...]).astype(BF16)

    def gate_projection(rows_all):
        gate_ref[rows_all, :] = _dot(h_ref[rows_all, :], w_gate_ref[...]) + b_gate_ref[...]

    def gate_cumsums(blocks):
        for blk in blocks:
            rows = pl.ds(blk * BLK, BLK)
            gates = gate_ref[rows, :]
            bcum = _dot(tri, jax.nn.log_sigmoid(gates), precision=lax.Precision.HIGHEST)
            bcum_ref[rows, :] = bcum
            gate_t_ref[blk] = gates.T
            bcum_t_ref[blk] = bcum.T

    def rotary(t):
        return t * cos + pltpu.roll(t, HEAD_DIM // 2, axis=1) * sin

    def seq_rows(j):
        return slice(j * seg, (j + 1) * seg)

    def lane_mask(j):
        c = lax.broadcasted_iota(jnp.int32, (BLK, BLK), 1)
        return (c >= j * seg) & (c < (j + 1) * seg)

    def ret_head(proj, blk, head, col):
        rows = pl.ds(blk * BLK, BLK)
        lg = _ret_log_gamma(head)
        q = rotary(proj[rows,col:col + HEAD_DIM])
        k = rotary(proj[rows,GROUP_W + col:GROUP_W + col + HEAD_DIM]) * scale
        v = proj[rows,2 * GROUP_W + col:2 * GROUP_W + col + HEAD_DIM]
        g = proj[rows,3 * GROUP_W + col:3 * GROUP_W + col + HEAD_DIM]
        qb, kb, vb = q.astype(BF16), k.astype(BF16), v.astype(BF16)
        dmat = jnp.where(masks.causal, jnp.exp(jnp.where(masks.causal, masks.diff * lg, 0.0)), 0.0)
        xi = jnp.exp((masks.tpos + 1.0) * lg)
        zeta = jnp.exp((float(seg - 1) - masks.tpos) * lg)
        chunk_decay = math.exp(seg * lg)
        scores = _dot_nt(qb, kb)
        kz_t = (k * zeta).T
        inter = []
        kz_tb = kz_t.astype(BF16)
        updates = _dot(jnp.concatenate(
            [jnp.where(lane_mask(j), kz_tb, jnp.zeros_like(kz_tb)) for j in range(n_seq)], axis=0), vb)
        for j in range(n_seq):
            seq = blk * n_seq + j
            s_prev = s_in[seq, head]
            inter.append(_dot(q[seq_rows(j)].astype(BF16), s_prev.astype(BF16)))
            s_out[seq, head] = s_prev * chunk_decay + updates[j * HEAD_DIM:(j + 1) * HEAD_DIM]
        inter = jnp.concatenate(inter, axis=0)
        yield
        out = _dot((scores * dmat).astype(BF16), vb)
        yield
        out = out + inter * xi
        y = _groupnorm(out, g_ret_ref[:, head * HEAD_DIM:(head + 1) * HEAD_DIM]) * (g * jax.nn.sigmoid(g))
        mix_ref[rows, head * HEAD_DIM:(head + 1) * HEAD_DIM] = y.astype(BF16)

    def ml_head(proj, blk, head, col):
        rows = pl.ds(blk * BLK, BLK)
        q = proj[rows,col:col + HEAD_DIM]
        k = proj[rows,GROUP_W + col:GROUP_W + col + HEAD_DIM] * scale
        v = proj[rows,2 * GROUP_W + col:2 * GROUP_W + col + HEAD_DIM]
        o = proj[rows,3 * GROUP_W + col:3 * GROUP_W + col + HEAD_DIM]
        qb, kb, vb = q.astype(BF16), k.astype(BF16), v.astype(BF16)
        i_col = gate_ref[rows, head:head + 1]
        b_col = bcum_ref[rows, N_ML_HEADS + head:N_ML_HEADS + head + 1]
        i_row = gate_t_ref[blk, head:head + 1, :]
        b_row = bcum_t_ref[blk, N_ML_HEADS + head:N_ML_HEADS + head + 1, :]
        m_prev_seq = m_in[blk, head:head + 1, :]
        m_prev = _rows_from_seq(masks.col_is_rowseq, m_prev_seq)
        logw = jnp.where(masks.causal, b_col - b_row + i_row, -jnp.inf)
        m_t = jnp.maximum(b_col + m_prev, jnp.max(logw, axis=1, keepdims=True))
        w = jnp.exp(logw - m_t)
        inter_w = jnp.exp(b_col + m_prev - m_t)
        s = _dot_nt(qb, kb)
        m_new_seq = _seq_from_last_row(masks.row_is_last_of_colseq, m_t)
        b_last_seq = _seq_from_last_row(masks.row_is_last_of_colseq, b_col)
        decay_seq = jnp.exp(b_last_seq + m_prev_seq - m_new_seq)
        m_new = _rows_from_seq(masks.col_is_rowseq, m_new_seq)
        b_last = jnp.sum(jnp.where(masks.col_is_last_of_rowseq, b_row, 0.0), axis=1, keepdims=True)
        wl = jnp.exp(b_last - b_col + i_col - m_new)
        vw_t = (v * wl).T
        kw = k * wl
        qc, qn = [], []
        for j in range(n_seq):
            seq = blk * n_seq + j
            c_prev = c_in[seq, head]
            n_prev = n_in[seq, head:head + 1, :]
            qc.append(_dot_nt(q[seq_rows(j)].astype(BF16), c_prev.astype(BF16)))
            qn.append(jnp.sum(q[seq_rows(j)] * n_prev, axis=1, keepdims=True))
            decay = decay_seq[:, j:j + 1]
            n_out[seq, head:head + 1, :] = decay * n_prev + jnp.sum(kw[seq_rows(j)], axis=0, keepdims=True)
        qc = jnp.concatenate(qc, axis=0)
        qn = jnp.concatenate(qn, axis=0)
        m_out[blk, head:head + 1, :] = m_new_seq
        yield
        s = s * w
        num = _dot(s.astype(BF16), vb)
        den = jnp.sum(s, axis=1, keepdims=True)
        vw_tb = vw_t.astype(BF16)
        updates = _dot(jnp.concatenate(
            [jnp.where(lane_mask(j), vw_tb, jnp.zeros_like(vw_tb)) for j in range(n_seq)], axis=0), kb)
        for j in range(n_seq):
            seq = blk * n_seq + j
            c_out[seq, head] = (decay_seq[:, j:j + 1] * c_in[seq, head]
                                + updates[j * HEAD_DIM:(j + 1) * HEAD_DIM])
        yield
        num = num + inter_w * qc
        den = jnp.maximum(jnp.abs(den + inter_w * qn), jnp.exp(-m_t))
        hout = num / den
        y = _groupnorm(hout, g_ml_ref[:, head * HEAD_DIM:(head + 1) * HEAD_DIM]) * jax.nn.sigmoid(o)
        mix_ref[rows, RET_W + head * HEAD_DIM:RET_W + (head + 1) * HEAD_DIM] = y.astype(BF16)

    r_idx = lax.broadcasted_iota(jnp.int32, (BLK, BLK), 0)
    c_idx = lax.broadcasted_iota(jnp.int32, (BLK, BLK), 1)
    causal_t = r_idx <= c_idx
    lane_pos = lax.broadcasted_iota(jnp.int32, (1, BLK), 1).astype(F32)
    head_consts = {}

    def ret_consts(head):
        lg = _ret_log_gamma(head)
        dmat_t = jnp.where(causal_t, jnp.exp(jnp.where(causal_t, (c_idx - r_idx).astype(F32) * lg, 0.0)), 0.0)
        xi_row = jnp.exp((lane_pos + 1.0) * lg)
        zeta = jnp.exp((float(BLK - 1) - r_idx.astype(F32)) * lg)
        return dmat_t, xi_row, zeta, math.exp(BLK * lg)

    def channel_norm_t(h_t):
        mu = jnp.mean(h_t, axis=0, keepdims=True)
        d = h_t - mu
        var = jnp.mean(d * d, axis=0, keepdims=True)
        return d * lax.rsqrt(var + EPS)

    def queries_of_pair(q_pair):
        zero = jnp.zeros_like(q_pair[0])
        return jnp.concatenate([jnp.concatenate([q_pair[0], zero], axis=1),
                                jnp.concatenate([zero, q_pair[1]], axis=1)], axis=0)

    def pair_cols(i):
        return slice(i * HEAD_DIM, (i + 1) * HEAD_DIM)

    def ret_pair_single(proj, blk, first_head):
        rows = pl.ds(blk * BLK, BLK)
        pair = range(HEADS_PER_GROUP)
        consts = [ret_consts(first_head + i) for i in pair]
        q = [rotary(proj[rows, pair_cols(i)]) for i in pair]
        k = [rotary(proj[rows, GROUP_W + i * HEAD_DIM:GROUP_W + (i + 1) * HEAD_DIM]) * scale for i in pair]
        v_tb = [proj[rows, 2 * GROUP_W + i * HEAD_DIM:2 * GROUP_W + (i + 1) * HEAD_DIM].T.astype(BF16)
                for i in pair]
        lhs = jnp.concatenate([
            jnp.concatenate([k[i].astype(BF16) for i in pair], axis=1),
            jnp.concatenate([st_ref[blk, first_head + i].astype(BF16) for i in pair], axis=1)], axis=0)
        first = _dot_nt(lhs, queries_of_pair([q[i].astype(BF16) for i in pair]))
        kz_b = [(k[i] * consts[i][2]).astype(BF16) for i in pair]
        yield
        out_t = []
        for i in pair:
            dmat_t, _, _, chunk_decay = consts[i]
            scores_t = first[:BLK, pair_cols(i)]
            both = _dot(v_tb[i], jnp.concatenate([kz_b[i], (scores_t * dmat_t).astype(BF16)], axis=1))
            st_ref[blk, first_head + i] = st_ref[blk, first_head + i] * chunk_decay + both[:, :HEAD_DIM]
            out_t.append(both[:, HEAD_DIM:])
        yield
        for i in pair:
            head = first_head + i
            out = out_t[i] + first[BLK:, pair_cols(i)] * consts[i][1]
            g = proj[rows, 3 * GROUP_W + i * HEAD_DIM:3 * GROUP_W + (i + 1) * HEAD_DIM]
            y = channel_norm_t(out).T * g_ret_ref[:, head * HEAD_DIM:(head + 1) * HEAD_DIM] * (g * jax.nn.sigmoid(g))
            mix_ref[rows, head * HEAD_DIM:(head + 1) * HEAD_DIM] = y.astype(BF16)

    def ml_pair_single(proj, blk, first_head):
        rows = pl.ds(blk * BLK, BLK)
        pair = range(HEADS_PER_GROUP)
        q = [proj[rows, pair_cols(i)] for i in pair]
        k = [proj[rows, GROUP_W + i * HEAD_DIM:GROUP_W + (i + 1) * HEAD_DIM] * scale for i in pair]
        kb = [k[i].astype(BF16) for i in pair]
        n_prev = [n_in[blk, first_head + i:first_head + i + 1, :] for i in pair]
        lhs = jnp.concatenate([
            jnp.concatenate(kb, axis=1),
            jnp.concatenate([c_in[blk, first_head + i].astype(BF16) for i in pair], axis=1),
            jnp.concatenate([jnp.broadcast_to(n_prev[i], (2 * SUBLANES, HEAD_DIM)).astype(BF16) for i in pair],
                            axis=1)], axis=0)
        first = _dot_nt(lhs, queries_of_pair([q[i].astype(BF16) for i in pair]))
        heads = [ml_head_single(proj, blk, first_head + i, i * HEAD_DIM, k[i], kb[i], n_prev[i],
                                first[:BLK, pair_cols(i)], first[BLK:2 * BLK, pair_cols(i)],
                                first[2 * BLK:2 * BLK + 1, pair_cols(i)]) for i in pair]
        for _ in range(3):
            for head in heads:
                next(head, None)
            yield

    def ml_head_single(proj, blk, head, col, k, kb, n_prev, s_t, qc_t, qn):
        rows = pl.ds(blk * BLK, BLK)
        v_t = proj[rows, 2 * GROUP_W + col:2 * GROUP_W + col + HEAD_DIM].T
        i_row = gate_t_ref[blk, head:head + 1, :]
        b_row = bcum_t_ref[blk, N_ML_HEADS + head:N_ML_HEADS + head + 1, :]
        key_term = gate_ref[rows, head:head + 1] - bcum_ref[rows, N_ML_HEADS + head:N_ML_HEADS + head + 1]
        m_prev = m_in[blk, head:head + 1, 0:1]
        logw_t = jnp.where(causal_t, b_row + key_term, -jnp.inf)
        m_t = jnp.maximum(b_row + m_prev, jnp.max(logw_t, axis=0, keepdims=True))
        w_t = jnp.exp(logw_t - m_t)
        inter_w = jnp.exp(b_row + m_prev - m_t)
        m_new = m_t[:, BLK - 1:BLK]
        b_last = b_row[:, BLK - 1:BLK]
        decay = jnp.exp(b_last + m_prev - m_new)
        wl = jnp.exp(b_last - b_row + i_row - m_new)
        m_out[blk, head:head + 1, :] = jnp.broadcast_to(m_new, (1, BLK))
        wk = _dot(jnp.broadcast_to(wl, (2 * SUBLANES, BLK)).astype(BF16), kb)[0:1, :]
        n_out[blk, head:head + 1, :] = decay * n_prev + wk
        kw_b = (k * jnp.exp(key_term + (b_last - m_new))).astype(BF16)
        v_tb = v_t.astype(BF16)
        yield
        s_t = s_t * w_t
        den = jnp.sum(s_t, axis=0, keepdims=True)
        both = _dot(v_tb, jnp.concatenate([kw_b, s_t.astype(BF16)], axis=1))
        c_out[blk, head] = decay * c_in[blk, head] + both[:, :HEAD_DIM]
        num_t = both[:, HEAD_DIM:]
        yield
        num_t = num_t + inter_w * qc_t
        den = jnp.maximum(jnp.abs(den + inter_w * qn), jnp.exp(-m_t))
        o = proj[rows, 3 * GROUP_W + col:3 * GROUP_W + col + HEAD_DIM]
        y = (channel_norm_t(num_t / den).T * g_ml_ref[:, head * HEAD_DIM:(head + 1) * HEAD_DIM]
             * jax.nn.sigmoid(o))
        mix_ref[rows, RET_W + head * HEAD_DIM:RET_W + (head + 1) * HEAD_DIM] = y.astype(BF16)

    n_ret_groups = N_RET_HEADS // HEADS_PER_GROUP
    n_groups = n_ret_groups + N_ML_HEADS // HEADS_PER_GROUP

    def first_head_of(grp):
        return (grp if grp < n_ret_groups else grp - n_ret_groups) * HEADS_PER_GROUP

    def project(grp, kind, rows_all):
        is_ret = grp < n_ret_groups
        c0 = (0 if is_ret else 4 * RET_W) + kind * (RET_W if is_ret else ML_W) + first_head_of(grp) * HEAD_DIM
        proj_ref[grp % 2, rows_all, kind * GROUP_W:(kind + 1) * GROUP_W] = _dot(
            h_ref[rows_all, :], w_in_ref[:, c0:c0 + GROUP_W])

    def out_project(grp, rows_all):
        cols = slice(grp * GROUP_W, (grp + 1) * GROUP_W)
        part = _dot(mix_ref[rows_all, cols], w_out_ref[cols, :])
        if grp == 0:
            mixed_ref[rows_all, :] = part
        else:
            mixed_ref[rows_all, :] += part

    def row_pipeline(blocks):
        rows_all = pl.ds(blocks[0] * BLK, len(blocks) * BLK)
        pre_norm(blocks)
        for kind in range(4):
            project(0, kind, rows_all)
        yield
        for grp in range(n_groups):
            if n_seq == 1:
                pair_fn = ret_pair_single if grp < n_ret_groups else ml_pair_single
                heads = [pair_fn(proj_ref.at[grp % 2], blk, first_head_of(grp)) for blk in blocks]
            else:
                head_fn = ret_head if grp < n_ret_groups else ml_head
                heads = [head_fn(proj_ref.at[grp % 2], blk, first_head_of(grp) + hh, hh * HEAD_DIM)
                         for blk in blocks for hh in range(HEADS_PER_GROUP)]
            fillers = {0: [functools.partial(project, grp + 1, kind, rows_all) for kind in range(4)]
                       if grp + 1 < n_groups else [],
                       1: [functools.partial(out_project, grp - 1, rows_all)] if grp >= 1 else [],
                       2: []}
            if grp == 0:
                fillers[0].append(functools.partial(gate_projection, rows_all))
                fillers[1].append(functools.partial(gate_cumsums, blocks))
            for phase in range(3):
                todo = list(fillers[phase])
                every = max(1, len(heads) // max(1, len(todo)))
                for i, head in enumerate(heads):
                    next(head, None)
                    if todo and (i + 1) % every == 0:
                        todo.pop(0)()
                for filler in todo:
                    filler()
            yield
        last_cols = slice((n_groups - 1) * GROUP_W, n_groups * GROUP_W)
        last = _dot(mix_ref[rows_all, last_cols], w_out_ref[last_cols, :])
        for i, blk in enumerate(blocks):
            rows = pl.ds(blk * BLK, BLK)
            mixed = mixed_ref[rows, :] + last[i * BLK:(i + 1) * BLK]
            y_ref[blk] = x_ref[blk] + _rmsnorm(mixed, g_post_ref[...])

    n_runs = 2 if n_blk % 2 == 0 else 1
    per_run = n_blk // n_runs
    pipelines = [row_pipeline(list(range(i * per_run, (i + 1) * per_run))) for i in range(n_runs)]
    n_stages = n_groups + 2
    for step in range(n_stages + n_runs - 1):
        for lag, pipeline in enumerate(pipelines):
            if 0 <= step - lag < n_stages:
                next(pipeline, None)

    if fresh_state:
        @pl.when(pl.program_id(1) == pl.num_programs(1) - 1)
        def _():
            for blk in range(n_blk):
                for head in range(N_RET_HEADS):
                    s_out[blk, head] = st_ref[blk, head].T


def _ffn_kernel(*refs, seg, n_blk, fresh_state):
    n_seq_tile = n_blk * BLK // seg
    rows_tile = n_blk * BLK
    if fresh_state:
        (x_ref, g_pre_ref, w_up_ref, conv_w_ref, conv_b_ref, w_down_ref, g_post_ref,
         y_ref, buf_out, h_ref, act_ref, ffn_ref) = refs
        buf_in = buf_out

        @pl.when(pl.program_id(1) == 0)
        def _():
            buf_out[...] = jnp.zeros(buf_out.shape, F32)
    else:
        (x_ref, g_pre_ref, w_up_ref, conv_w_ref, conv_b_ref, w_down_ref, g_post_ref, buf_in,
         y_ref, buf_out, h_ref, act_ref, ffn_ref) = refs

    for blk in range(n_blk):
        rows = pl.ds(blk * BLK, BLK)
        h_ref[rows, :] = _rmsnorm(x_ref[blk], g_pre_ref[...]).astype(BF16)

    tpos = lax.broadcasted_iota(jnp.int32, (n_seq_tile, seg, FF_BLK), 1)

    def conv(cols):
        up = _dot(h_ref[...], w_up_ref[:, cols])
        prev0 = buf_in[:, 0:1, cols]
        prev1 = buf_in[:, 1:2, cols]
        up3 = up.reshape(n_seq_tile, seg, FF_BLK)
        sh1 = pltpu.roll(up, 1, axis=0).reshape(n_seq_tile, seg, FF_BLK)
        sh2 = pltpu.roll(up, 2, axis=0).reshape(n_seq_tile, seg, FF_BLK)
        sh1 = jnp.where(tpos == 0, prev1, sh1)
        sh2 = jnp.where(tpos == 0, prev0, jnp.where(tpos == 1, prev1, sh2))
        buf_out[:, 0:1, cols] = up3[:, seg - 2:seg - 1, :]
        buf_out[:, 1:2, cols] = up3[:, seg - 1:seg, :]
        w = conv_w_ref[:, cols]
        out = sh2 * w[0:1, :] + sh1 * w[1:2, :] + up3 * w[2:3, :] + conv_b_ref[:, cols]
        return out.reshape(rows_tile, FF_BLK)

    def activation(gate, val):
        c0 = -2.0 * math.log2(math.e) * math.sqrt(2.0 / math.pi)
        c1 = 0.044715 * c0
        return ((gate * val) / (1.0 + jnp.exp2(gate * (gate * gate * c1 + c0)))).astype(BF16)

    for j in range(D_FF // FF_BLK):
        cols = slice(j * FF_BLK, (j + 1) * FF_BLK)
        gate = conv(cols)
        val = conv(slice(D_FF + j * FF_BLK, D_FF + (j + 1) * FF_BLK))
        act_ref[:, cols] = activation(gate, val)

    ffn_ref[...] = _dot(act_ref[...], w_down_ref[...])
    for blk in range(n_blk):
        rows = pl.ds(blk * BLK, BLK)
        y_ref[blk] = x_ref[blk] + _rmsnorm(ffn_ref[rows, :], g_post_ref[...])


def _const_spec(shape):
    zeros = (0,) * len(shape)
    return pl.BlockSpec(shape, lambda *_: zeros, pipeline_mode=pl.Buffered(1))


def _compiler_params(n_grid_dims):
    return pltpu.CompilerParams(
        dimension_semantics=("arbitrary",) * n_grid_dims,
        vmem_limit_bytes=VMEM_LIMIT_BYTES)


def _mixer_scratch(n_blk, fresh_state):
    rows = n_blk * BLK
    carried = [pltpu.VMEM((n_blk, N_RET_HEADS, HEAD_DIM, HEAD_DIM), F32)] if fresh_state else []
    return [
        pltpu.VMEM((rows, D_MODEL), BF16),
        pltpu.VMEM((2, rows, 4 * GROUP_W), F32),
        pltpu.VMEM((rows, RET_W + ML_W), BF16),
        pltpu.VMEM((rows, D_MODEL), F32),
        pltpu.VMEM((rows, BLK), F32),
        pltpu.VMEM((rows, BLK), F32),
        pltpu.VMEM((n_blk, BLK, BLK), F32),
        pltpu.VMEM((n_blk, BLK, BLK), F32),
    ] + carried


def _mixer_weight_specs():
    return [
        _const_spec((1, D_MODEL)),
        _const_spec((D_MODEL, 4 * RET_W + 4 * ML_W + 2 * N_ML_HEADS)),
        _const_spec((D_MODEL, BLK)),
        _const_spec((1, BLK)),
        _const_spec((1, RET_W)),
        _const_spec((1, ML_W)),
        _const_spec((RET_W + ML_W, D_MODEL)),
        _const_spec((1, D_MODEL)),
    ]


def _state_shapes(n_seqs, n_blocks):
    return [
        jax.ShapeDtypeStruct((n_seqs, N_RET_HEADS, HEAD_DIM, HEAD_DIM), F32),
        jax.ShapeDtypeStruct((n_seqs, N_ML_HEADS, HEAD_DIM, HEAD_DIM), F32),
        jax.ShapeDtypeStruct((n_seqs, N_ML_HEADS, HEAD_DIM), F32),
        jax.ShapeDtypeStruct((n_blocks, N_ML_HEADS, BLK), F32),
    ]


def _state_specs(n_seq_tile, n_blk, index):
    return [
        pl.BlockSpec((n_seq_tile, N_RET_HEADS, HEAD_DIM, HEAD_DIM), lambda *g: (index(*g), 0, 0, 0)),
        pl.BlockSpec((n_seq_tile, N_ML_HEADS, HEAD_DIM, HEAD_DIM), lambda *g: (index(*g), 0, 0, 0)),
        pl.BlockSpec((n_seq_tile, N_ML_HEADS, HEAD_DIM), lambda *g: (index(*g), 0, 0)),
        pl.BlockSpec((n_blk, N_ML_HEADS, BLK), lambda *g: (index(*g), 0, 0)),
    ]


def _prompt_mixer(x, cos, sin, weights, n_blk):
    batch, seq_len, _ = x.shape
    grid = (batch // n_blk, seq_len // BLK)
    x_spec = pl.BlockSpec((n_blk, BLK, D_MODEL), lambda g, c: (g, c, 0))
    rope_spec = pl.BlockSpec((BLK, HEAD_DIM), lambda g, c: (c, 0))
    return pl.pallas_call(
        functools.partial(_mixer_kernel, seg=BLK, n_blk=n_blk, fresh_state=True),
        grid=grid,
        in_specs=[x_spec, rope_spec, rope_spec] + _mixer_weight_specs(),
        out_specs=[x_spec] + _state_specs(n_blk, n_blk, lambda g, c: g),
        out_shape=[jax.ShapeDtypeStruct(x.shape, F32)] + _state_shapes(batch, batch),
        scratch_shapes=_mixer_scratch(n_blk, fresh_state=True),
        compiler_params=_compiler_params(2),
        name="prompt_mixer",
    )(x, cos, sin, *weights)


def _sample_mixer(x, cos, sin, weights, states, seg, n_blk):
    n_blocks = x.shape[0]
    n_seq_tile = n_blk * BLK // seg
    grid = (n_blocks // n_blk,)
    x_spec = pl.BlockSpec((n_blk, BLK, D_MODEL), lambda g: (g, 0, 0))
    rope_spec = _const_spec((BLK, HEAD_DIM))
    state_specs = _state_specs(n_seq_tile, n_blk, lambda g: g)
    return pl.pallas_call(
        functools.partial(_mixer_kernel, seg=seg, n_blk=n_blk, fresh_state=False),
        grid=grid,
        in_specs=[x_spec, rope_spec, rope_spec] + _mixer_weight_specs() + state_specs,
        out_specs=[x_spec] + state_specs,
        out_shape=[jax.ShapeDtypeStruct(x.shape, F32)]
        + _state_shapes(n_blocks * BLK // seg, n_blocks),
        scratch_shapes=_mixer_scratch(n_blk, fresh_state=False),
        compiler_params=_compiler_params(1),
        name="sample_mixer",
    )(x, cos, sin, *weights, *states)


def _ffn_scratch(n_blk):
    rows = n_blk * BLK
    return [
        pltpu.VMEM((rows, D_MODEL), BF16),
        pltpu.VMEM((rows, D_FF), BF16),
        pltpu.VMEM((rows, D_MODEL), F32),
    ]


def _ffn_weight_specs():
    return [
        _const_spec((1, D_MODEL)),
        _const_spec((D_MODEL, 2 * D_FF)),
        _const_spec((CONV_W, 2 * D_FF)),
        _const_spec((1, 2 * D_FF)),
        _const_spec((D_FF, D_MODEL)),
        _const_spec((1, D_MODEL)),
    ]


def _prompt_ffn(x, weights, n_blk):
    batch, seq_len, _ = x.shape
    grid = (batch // n_blk, seq_len // BLK)
    x_spec = pl.BlockSpec((n_blk, BLK, D_MODEL), lambda g, c: (g, c, 0))
    buf_spec = pl.BlockSpec((n_blk, CONV_W - 1, 2 * D_FF), lambda g, c: (g, 0, 0))
    return pl.pallas_call(
        functools.partial(_ffn_kernel, seg=BLK, n_blk=n_blk, fresh_state=True),
        grid=grid,
        in_specs=[x_spec] + _ffn_weight_specs(),
        out_specs=[x_spec, buf_spec],
        out_shape=[jax.ShapeDtypeStruct(x.shape, F32),
                   jax.ShapeDtypeStruct((batch, CONV_W - 1, 2 * D_FF), F32)],
        scratch_shapes=_ffn_scratch(n_blk),
        compiler_params=_compiler_params(2),
        name="prompt_ffn",
    )(x, *weights)


def _sample_ffn(x, weights, conv_buf, seg, n_blk):
    n_blocks = x.shape[0]
    n_seq_tile = n_blk * BLK // seg
    grid = (n_blocks // n_blk,)
    x_spec = pl.BlockSpec((n_blk, BLK, D_MODEL), lambda g: (g, 0, 0))
    buf_spec = pl.BlockSpec((n_seq_tile, CONV_W - 1, 2 * D_FF), lambda g: (g, 0, 0))
    return pl.pallas_call(
        functools.partial(_ffn_kernel, seg=seg, n_blk=n_blk, fresh_state=False),
        grid=grid,
        in_specs=[x_spec] + _ffn_weight_specs() + [buf_spec],
        out_specs=[x_spec, buf_spec],
        out_shape=[jax.ShapeDtypeStruct(x.shape, F32),
                   jax.ShapeDtypeStruct(conv_buf.shape, F32)],
        scratch_shapes=_ffn_scratch(n_blk),
        compiler_params=_compiler_params(1),
        name="sample_ffn",
    )(x, *weights, conv_buf)


def _rope_freqs():
    return ROPE_BASE ** (-jnp.arange(0, HEAD_DIM, 2, dtype=F32) / HEAD_DIM)


def _rope_layout(cos, sin):
    return jnp.concatenate([cos, cos], axis=-1), jnp.concatenate([-sin, sin], axis=-1)


def _rope_tables(pos):
    ang = pos.astype(F32)[:, None] * _rope_freqs()[None, :]
    return _rope_layout(jnp.cos(ang), jnp.sin(ang))


def _rope_tables_by_chunk(n_chunks):
    freqs = _rope_freqs()[None, :]
    base = (BLK * jnp.arange(n_chunks, dtype=jnp.int32)).astype(F32)[:, None] * freqs
    step = jnp.arange(BLK, dtype=jnp.int32).astype(F32)[:, None] * freqs
    cos_b, sin_b = jnp.cos(base)[:, None, :], jnp.sin(base)[:, None, :]
    cos_s, sin_s = jnp.cos(step)[None], jnp.sin(step)[None]
    cos = (cos_b * cos_s - sin_b * sin_s).reshape(n_chunks * BLK, HEAD_DIM // 2)
    sin = (sin_b * cos_s + cos_b * sin_s).reshape(n_chunks * BLK, HEAD_DIM // 2)
    return _rope_layout(cos, sin)


def kernel(x_prompt, x_sample, state_ret, state_mlstm_C, state_mlstm_n, state_mlstm_m, cache_ffn_conv, pre_mix_gain, w_in, b_gates, ret_head_gain, mlstm_head_gain, w_out, post_mix_gain, pre_ffn_gain, w_up, conv_w, conv_b, w_down, post_ffn_gain):
    depth = w_in.shape[0]
    assert depth == 1
    batch, seq_len, _ = x_prompt.shape
    dec_batch, dec_seq, _ = x_sample.shape
    assert seq_len % BLK == 0 and BLK % dec_seq == 0 and (dec_batch * dec_seq) % BLK == 0
    seqs_per_blk = BLK // dec_seq
    n_sample_blocks = dec_batch * dec_seq // BLK
    layer = 0

    n_gate = 2 * N_ML_HEADS
    w_gate = jnp.pad(w_in[layer][:, -n_gate:], ((0, 0), (0, BLK - n_gate))).astype(BF16)
    b_gate = jnp.pad(b_gates[layer], (0, BLK - n_gate)).reshape(1, BLK)
    mixer_weights = (
        pre_mix_gain[layer].reshape(1, D_MODEL), w_in[layer].astype(BF16), w_gate, b_gate,
        ret_head_gain[layer].reshape(1, RET_W), mlstm_head_gain[layer].reshape(1, ML_W),
        w_out[layer].astype(BF16), post_mix_gain[layer].reshape(1, D_MODEL))
    ffn_weights = (
        pre_ffn_gain[layer].reshape(1, D_MODEL), w_up[layer].astype(BF16), conv_w[layer],
        conv_b[layer].reshape(1, 2 * D_FF), w_down[layer].astype(BF16),
        post_ffn_gain[layer].reshape(1, D_MODEL))

    cos_p, sin_p = _rope_tables_by_chunk(seq_len // BLK)
    assert batch % PROMPT_BLOCKS_PER_STEP == 0
    x1_p, s_p, c_p, n_p, m_p = _prompt_mixer(x_prompt, cos_p, sin_p, mixer_weights, n_blk=PROMPT_BLOCKS_PER_STEP)
    y_p, buf_p = _prompt_ffn(x1_p, ffn_weights, n_blk=PROMPT_BLOCKS_PER_STEP)
    m_p = m_p[:, :, 0]

    pos_s = PAST_LEN + jnp.arange(dec_seq, dtype=jnp.int32)
    cos_s, sin_s = _rope_tables(jnp.tile(pos_s, seqs_per_blk))
    m_blocks = jnp.transpose(state_mlstm_m[layer].reshape(n_sample_blocks, seqs_per_blk, N_ML_HEADS), (0, 2, 1))
    m_blocks = jnp.pad(m_blocks, ((0, 0), (0, 0), (0, BLK - seqs_per_blk)))
    xs = x_sample.reshape(n_sample_blocks, BLK, D_MODEL)
    x1_s, s_s, c_s, n_s, m_s = _sample_mixer(
        xs, cos_s, sin_s, mixer_weights,
        (state_ret[layer], state_mlstm_C[layer], state_mlstm_n[layer], m_blocks),
        seg=dec_seq, n_blk=SAMPLE_MIXER_BLOCKS_PER_STEP)
    y_s, buf_s = _sample_ffn(x1_s, ffn_weights, cache_ffn_conv[layer], seg=dec_seq,
                             n_blk=SAMPLE_FFN_BLOCKS_PER_STEP)
    y_s = y_s.reshape(x_sample.shape)
    m_s = jnp.transpose(m_s[:, :, :seqs_per_blk], (0, 2, 1)).reshape(dec_batch, N_ML_HEADS)

    return (y_p, y_s, s_p[None], s_s[None], c_p[None], c_s[None], n_p[None], n_s[None],
            m_p[None], m_s[None], buf_p[None], buf_s[None])
```
